```python
import jax
import jax.numpy as jnp
from jax import lax
import numpy as np

D_MODEL = 2048
BATCH = 32
SEQ = 256
DEPTH = 2
DEC_BATCH = 8
DEC_SEQ = 4096
PAST_LEN = 512

GRID_W = 64
N_MIXERS = 2
N_MLSTM = (DEPTH + 1) // 2
N_ATTN = DEPTH // 2
EPS = 1e-6
MLSTM_HEADS = 8
MLSTM_DK = D_MODEL // 2 // MLSTM_HEADS
MLSTM_DV = D_MODEL // MLSTM_HEADS
MLSTM_CHUNK = 64
MLSTM_F_BIAS = 3.0
MLSTM_PROJ = 2 * MLSTM_HEADS * MLSTM_DK + 2 * MLSTM_HEADS * MLSTM_DV + 4 * MLSTM_HEADS
HEAD_DIM = 128
ATTN_Q_HEADS = D_MODEL // HEAD_DIM
ATTN_KV_HEADS = 4
ATTN_GROUPS = ATTN_Q_HEADS // ATTN_KV_HEADS
WINDOW = 128
QBLK = 128
ROPE_THETA = 10000.0
ATTN_PROJ = (ATTN_Q_HEADS + 2 * ATTN_KV_HEADS) * HEAD_DIM
N_EXPERTS = 64
TOP_K = 8
N_EXPERT_GROUPS = 8
TOPK_GROUPS = 4
EXPERT_FF = D_MODEL // 4
SHARED_FF = D_MODEL // 4
ROUTED_SCALE = 2.5
MOE_BLOCK = 256

kernel_name = 'hybrid_mlstm_swa_moe_diffusion_step'


def _rmsnorm(x, g):
    xf = x.astype(jnp.float32)
    y = xf * lax.rsqrt(jnp.mean(xf * xf, axis=-1, keepdims=True) + EPS)
    return (y * g.astype(jnp.float32)).astype(x.dtype)


def _modulate(h, shift, scale):
    return h * (1 + scale) + shift


def _swiglu(x, w1, w3, w2):
    return (jax.nn.silu(x @ w1) * (x @ w3)) @ w2


def _mlstm_chunked(q, k, v, ig, lf, C0, n0, m0):
    B, S, H, DK = q.shape
    DV = v.shape[-1]
    L = MLSTM_CHUNK
    nc = S // L

    def chunks(a):
        a = a.reshape((B, nc, L) + a.shape[2:])
        return jnp.moveaxis(jnp.moveaxis(a, 1, 0), 3, 2)

    tri = jnp.tril(jnp.ones((L, L), bool))

    def step(carry, xs):
        C, n, m = carry
        qc, kc, vc, ic, fc = xs
        b = jnp.cumsum(fc, axis=-1)
        dmat = jnp.where(tri, b[..., :, None] - b[..., None, :] + ic[..., None, :], -jnp.inf)
        inter = b + m[..., None]
        m_out = jnp.maximum(inter, dmat.max(-1))
        w = jnp.exp(dmat - m_out[..., None]) * jnp.einsum('bhtd,bhsd->bhts', qc, kc)
        dec = jnp.exp(inter - m_out)
        num = jnp.einsum('bhts,bhsv->bhtv', w, vc) + dec[..., None] * jnp.einsum('bhtd,bhdv->bhtv', qc, C)
        den = w.sum(-1) + dec * jnp.einsum('bhtd,bhd->bht', qc, n)
        hc = num / jnp.maximum(jnp.abs(den), jnp.exp(-m_out))[..., None]
        b_end = b[..., -1]
        to_end = b_end[..., None] - b + ic
        m_new = jnp.maximum(b_end + m, to_end.max(-1))
        wk = jnp.exp(to_end - m_new[..., None])
        dec_end = jnp.exp(b_end + m - m_new)
        C_new = dec_end[..., None, None] * C + jnp.einsum('bhsd,bhsv->bhdv', kc * wk[..., None], vc)
        n_new = dec_end[..., None] * n + jnp.einsum('bhs,bhsd->bhd', wk, kc)
        return (C_new, n_new, m_new), hc

    (C, n, m), hs = lax.scan(step, (C0, n0, m0), (chunks(q), chunks(k), chunks(v), chunks(ig), chunks(lf)))
    hs = jnp.moveaxis(jnp.moveaxis(hs, 2, 3), 0, 1).reshape(B, S, H, DV)
    return hs, C, n, m


def _mlstm_mixer(h, w_in, gate_b, head_g, w_out, C0, n0, m0):
    B, S, _ = h.shape
    H, DK, DV = MLSTM_HEADS, MLSTM_DK, MLSTM_DV
    f32 = jnp.float32
    proj = h @ w_in
    q, k, v, o, g = jnp.split(proj, [H * DK, 2 * H * DK, 2 * H * DK + H * DV, 2 * H * DK + 2 * H * DV], axis=-1)
    q = q.reshape(B, S, H, DK).astype(f32) * (DK ** -0.5)
    k = k.reshape(B, S, H, DK).astype(f32)
    v = v.reshape(B, S, H, DV).astype(f32)
    g = (g.astype(f32) + gate_b.astype(f32)).reshape(B, S, 4, H)
    C0, n0, m0 = C0.astype(f32), n0.astype(f32), m0.astype(f32)
    h_f, Cf, nf, mf = _mlstm_chunked(q, k, v, g[:, :, 0], jax.nn.log_sigmoid(g[:, :, 1]), C0[:, 0], n0[:, 0], m0[:, 0])
    flip = lambda a: jnp.flip(a, axis=1)
    h_b, Cb, nb, mb = _mlstm_chunked(flip(q), flip(k), flip(v), flip(g[:, :, 2]),
                                     flip(jax.nn.log_sigmoid(g[:, :, 3])), C0[:, 1], n0[:, 1], m0[:, 1])
    hs = h_f + flip(h_b)
    hs = hs * lax.rsqrt(jnp.mean(hs * hs, axis=-1, keepdims=True) + EPS) * head_g.astype(f32).reshape(H, DV)
    out = (jax.nn.sigmoid(o.astype(f32)) * hs.reshape(B, S, H * DV)).astype(h.dtype) @ w_out
    return out, jnp.stack([Cf, Cb], axis=1), jnp.stack([nf, nb], axis=1), jnp.stack([mf, mb], axis=1)


def _axial_angles(n_tokens):
    rows = n_tokens // GRID_W
    row_id = jnp.repeat(jnp.arange(rows), GRID_W).astype(jnp.float32)
    col_id = jnp.tile(jnp.arange(GRID_W), rows).astype(jnp.float32)
    quarter = HEAD_DIM // 4
    inv = ROPE_THETA ** (-jnp.arange(quarter, dtype=jnp.float32) / quarter)
    return row_id[:, None] * inv, col_id[:, None] * inv


def _rotate(x, ang):
    x1, x2 = jnp.split(x, 2, axis=-1)
    cos, sin = jnp.cos(ang).astype(x.dtype), jnp.sin(ang).astype(x.dtype)
    return jnp.concatenate([x1 * cos - x2 * sin, x2 * cos + x1 * sin], axis=-1)


def _rope_2d(x, ang_row, ang_col):
    half = HEAD_DIM // 2
    return jnp.concatenate([_rotate(x[..., :half], ang_row[None, :, None, :]),
                            _rotate(x[..., half:], ang_col[None, :, None, :])], axis=-1)


def _attn_qkv(h, w_qkv, q_g, k_g):
    B, S, _ = h.shape
    q, k, v = jnp.split(h @ w_qkv, [ATTN_Q_HEADS * HEAD_DIM, (ATTN_Q_HEADS + ATTN_KV_HEADS) * HEAD_DIM], axis=-1)
    q = _rmsnorm(q.reshape(B, S, ATTN_Q_HEADS, HEAD_DIM), q_g)
    k = _rmsnorm(k.reshape(B, S, ATTN_KV_HEADS, HEAD_DIM), k_g)
    v = v.reshape(B, S, ATTN_KV_HEADS, HEAD_DIM)
    return q, k, v


def _sink_attend(q, k, v, mask, sink):
    s = jnp.einsum('bqkgd,bskd->bkgqs', q, k).astype(jnp.float32) * (HEAD_DIM ** -0.5)
    if mask is not None:
        s = jnp.where(mask, s, -jnp.inf)
    sk = sink.astype(jnp.float32)[None, :, :, None, None]
    m = jnp.maximum(s.max(-1, keepdims=True), sk)
    p = jnp.exp(s - m)
    p = p / (p.sum(-1, keepdims=True) + jnp.exp(sk - m))
    return jnp.einsum('bkgqs,bskd->bqkgd', p.astype(v.dtype), v)


def _attn_context(h, w_qkv, q_g, k_g, sink, w_o):
    B, S, _ = h.shape
    q, k, v = _attn_qkv(h, w_qkv, q_g, k_g)
    nb = S // QBLK
    qb = jnp.moveaxis(q.reshape(B, nb, QBLK, ATTN_KV_HEADS, ATTN_GROUPS, HEAD_DIM), 1, 0)
    sink_kg = sink.reshape(ATTN_KV_HEADS, ATTN_GROUPS)
    o = lax.map(lambda qi: _sink_attend(qi, k, v, None, sink_kg), qb)
    o = jnp.moveaxis(o, 0, 1).reshape(B, S, ATTN_Q_HEADS * HEAD_DIM)
    return o @ w_o, k, v


def _attn_latent(h, ctx_k, ctx_v, w_qkv, q_g, k_g, sink, w_o):
    B, S, _ = h.shape
    q, k, v = _attn_qkv(h, w_qkv, q_g, k_g)
    ang_r, ang_c = _axial_angles(S)
    q = _rope_2d(q, ang_r, ang_c).reshape(B, S, ATTN_KV_HEADS, ATTN_GROUPS, HEAD_DIM)
    k = _rope_2d(k, ang_r, ang_c)
    pad = ((0, 0), (WINDOW, WINDOW), (0, 0), (0, 0))
    k_pad, v_pad = jnp.pad(k, pad), jnp.pad(v, pad)
    span = QBLK + 2 * WINDOW
    rel = jnp.arange(QBLK)[:, None] + WINDOW - jnp.arange(span)[None, :]
    band = jnp.abs(rel) <= WINDOW
    ctx_ok = jnp.ones((QBLK, ctx_k.shape[1]), bool)
    sink_kg = sink.reshape(ATTN_KV_HEADS, ATTN_GROUPS)

    def block(b):
        start = b * QBLK
        qb = lax.dynamic_slice_in_dim(q, start, QBLK, axis=1)
        kb = lax.dynamic_slice_in_dim(k_pad, start, span, axis=1)
        vb = lax.dynamic_slice_in_dim(v_pad, start, span, axis=1)
        key_pos = start - WINDOW + jnp.arange(span)
        valid = band & ((key_pos >= 0) & (key_pos < S))[None, :]
        mask = jnp.concatenate([valid, ctx_ok], axis=1)
        kk = jnp.concatenate([kb, ctx_k.astype(kb.dtype)], axis=1)
        vv = jnp.concatenate([vb, ctx_v.astype(vb.dtype)], axis=1)
        return _sink_attend(qb, kk, vv, mask, sink_kg)

    o = lax.map(block, jnp.arange(S // QBLK))
    o = jnp.moveaxis(o, 0, 1).reshape(B, S, ATTN_Q_HEADS * HEAD_DIM)
    return o @ w_o


def _moe(h, router_w, router_b, w1, w3, w2, ws1, ws3, ws2):
    N, D = h.shape
    f32 = jnp.float32
    scores = jax.nn.sigmoid((h @ router_w).astype(f32))
    biased = scores + router_b.astype(f32)
    per_group = N_EXPERTS // N_EXPERT_GROUPS
    group_score = lax.top_k(biased.reshape(N, N_EXPERT_GROUPS, per_group), 2)[0].sum(-1)
    _, top_groups = lax.top_k(group_score, TOPK_GROUPS)
    group_ok = jnp.any(top_groups[:, :, None] == jnp.arange(N_EXPERT_GROUPS)[None, None, :], axis=1)
    expert_ok = jnp.repeat(group_ok, per_group, axis=1)
    _, top_e = lax.top_k(jnp.where(expert_ok, biased, -jnp.inf), TOP_K)
    gates = jnp.take_along_axis(scores, top_e, axis=1)
    gates = gates / gates.sum(-1, keepdims=True) * ROUTED_SCALE
    A = N * TOP_K
    e_flat = top_e.reshape(A)
    tok_flat = jnp.repeat(jnp.arange(N, dtype=jnp.int32), TOP_K)
    g_flat = gates.reshape(A)
    order = jnp.argsort(e_flat)
    e_sorted = e_flat[order]
    counts = jnp.bincount(e_flat, length=N_EXPERTS)
    padded = (counts + MOE_BLOCK - 1) // MOE_BLOCK * MOE_BLOCK
    pad_end = jnp.cumsum(padded)
    pad_start = pad_end - padded
    start = jnp.cumsum(counts) - counts
    dest = pad_start[e_sorted] + jnp.arange(A) - start[e_sorted]
    n_blocks = -(-A // MOE_BLOCK) + N_EXPERTS
    slots = n_blocks * MOE_BLOCK
    slot_tok = jnp.full((slots,), N, jnp.int32).at[dest].set(tok_flat[order])
    slot_gate = jnp.zeros((slots,), f32).at[dest].set(g_flat[order])
    block_expert = jnp.minimum(jnp.sum(pad_end[None, :] <= (jnp.arange(n_blocks) * MOE_BLOCK)[:, None], axis=1),
                               N_EXPERTS - 1)
    h_ext = jnp.concatenate([h, jnp.zeros((1, D), h.dtype)], axis=0)

    def add_block(acc, blk):
        tok, gate, e = blk
        yb = _swiglu(h_ext[tok], w1[e], w3[e], w2[e]) * gate[:, None].astype(h.dtype)
        return acc.at[tok].add(yb.astype(acc.dtype)), None

    routed, _ = lax.scan(add_block, jnp.zeros_like(h_ext),
                         (slot_tok.reshape(n_blocks, MOE_BLOCK), slot_gate.reshape(n_blocks, MOE_BLOCK), block_expert))
    return routed[:N] + _swiglu(h, ws1, ws3, ws2)


def setup_inputs(seed: int = 0) -> dict:
    key = jax.random.key(seed)
    ks = jax.random.split(key, 32)
    D = D_MODEL
    H = MLSTM_HEADS

    def nrm(k, shape, scale):
        return jax.random.normal(k, shape, jnp.float32) * scale

    gate_base = jnp.tile(jnp.repeat(jnp.array([0.0, MLSTM_F_BIAS], jnp.float32), H), 2)
    return {
        'x_prompt': nrm(ks[0], (BATCH, SEQ, D), 1.0),
        'x_sample': nrm(ks[1], (DEC_BATCH, DEC_SEQ, D), 1.0),
        'state_mlstm_C': nrm(ks[2], (DEC_BATCH, N_MLSTM, 2, H, MLSTM_DK, MLSTM_DV), 0.5),
        'state_mlstm_n': nrm(ks[3], (DEC_BATCH, N_MLSTM, 2, H, MLSTM_DK), 0.5),
        'state_mlstm_m': nrm(ks[4], (DEC_BATCH, N_MLSTM, 2, H), 0.5),
        'cache_attn_k': nrm(ks[5], (DEC_BATCH, N_ATTN, PAST_LEN, ATTN_KV_HEADS, HEAD_DIM), 1.0),
        'cache_attn_v': nrm(ks[6], (DEC_BATCH, N_ATTN, PAST_LEN, ATTN_KV_HEADS, HEAD_DIM), 1.0),
        'c': nrm(ks[7], (DEC_BATCH, D), 1.0),
        'c_ctx': nrm(ks[8], (D,), 1.0),
        'ada_w': nrm(ks[9], (DEPTH, D, 6 * D), 0.5 * D ** -0.5),
        'ada_b': nrm(ks[10], (DEPTH, 6 * D), 0.02),
        'norm_mix': 1.0 + nrm(ks[11], (DEPTH, D), 0.02),
        'norm_ffn': 1.0 + nrm(ks[12], (DEPTH, D), 0.02),
        'mlstm_w_in': nrm(ks[13], (N_MLSTM, D, MLSTM_PROJ), D ** -0.5),
        'mlstm_gate_b': gate_base[None, :] + nrm(ks[14], (N_MLSTM, 4 * H), 0.1),
        'mlstm_head_g': 1.0 + nrm(ks[15], (N_MLSTM, H * MLSTM_DV), 0.02),
        'mlstm_w_out': nrm(ks[16], (N_MLSTM, H * MLSTM_DV, D), (H * MLSTM_DV) ** -0.5),
        'attn_w_qkv': nrm(ks[17], (N_ATTN, D, ATTN_PROJ), D ** -0.5),
        'attn_q_g': 1.0 + nrm(ks[18], (N_ATTN, HEAD_DIM), 0.02),
        'attn_k_g': 1.0 + nrm(ks[19], (N_ATTN, HEAD_DIM), 0.02),
        'attn_sink': nrm(ks[20], (N_ATTN, ATTN_Q_HEADS), 0.5),
        'attn_w_o': nrm(ks[21], (N_ATTN, ATTN_Q_HEADS * HEAD_DIM, D), (ATTN_Q_HEADS * HEAD_DIM) ** -0.5),
        'moe_router_w': nrm(ks[22], (DEPTH, D, N_EXPERTS), D ** -0.5),
        'moe_router_b': nrm(ks[23], (DEPTH, N_EXPERTS), 0.01),
        'moe_w1': nrm(ks[24], (DEPTH, N_EXPERTS, D, EXPERT_FF), D ** -0.5),
        'moe_w3': nrm(ks[25], (DEPTH, N_EXPERTS, D, EXPERT_FF), D ** -0.5),
        'moe_w2': nrm(ks[26], (DEPTH, N_EXPERTS, EXPERT_FF, D), EXPERT_FF ** -0.5),
        'shared_w1': nrm(ks[27], (DEPTH, D, SHARED_FF), D ** -0.5),
        'shared_w3': nrm(ks[28], (DEPTH, D, SHARED_FF), D ** -0.5),
        'shared_w2': nrm(ks[29], (DEPTH, SHARED_FF, D), SHARED_FF ** -0.5),
    }


def reference(x_prompt, x_sample, state_mlstm_C, state_mlstm_n, state_mlstm_m, cache_attn_k, cache_attn_v,
              c, c_ctx, ada_w, ada_b, norm_mix, norm_ffn,
              mlstm_w_in, mlstm_gate_b, mlstm_head_g, mlstm_w_out,
              attn_w_qkv, attn_q_g, attn_k_g, attn_sink, attn_w_o,
              moe_router_w, moe_router_b, moe_w1, moe_w3, moe_w2, shared_w1, shared_w3, shared_w2):
    xp, xs = x_prompt, x_sample
    Bp = xp.shape[0]
    new_C, new_n, new_m, new_k, new_v = [], [], [], [], []
    for i in range(DEPTH):
        j = i // N_MIXERS
        mod_p = (jax.nn.silu(c_ctx) @ ada_w[i] + ada_b[i])[None, None, :]
        mod_s = (jax.nn.silu(c) @ ada_w[i] + ada_b[i])[:, None, :]
        sh1p, sc1p, g1p, sh2p, sc2p, g2p = jnp.split(mod_p, 6, axis=-1)
        sh1s, sc1s, g1s, sh2s, sc2s, g2s = jnp.split(mod_s, 6, axis=-1)
        hp = _modulate(_rmsnorm(xp, norm_mix[i]), sh1p, sc1p)
        hs = _modulate(_rmsnorm(xs, norm_mix[i]), sh1s, sc1s)
        if i % N_MIXERS == 0:
            zC = jnp.zeros((Bp, 2, MLSTM_HEADS, MLSTM_DK, MLSTM_DV), jnp.float32)
            zn = jnp.zeros((Bp, 2, MLSTM_HEADS, MLSTM_DK), jnp.float32)
            zm = jnp.zeros((Bp, 2, MLSTM_HEADS), jnp.float32)
            op, Cp, n_p, m_p = _mlstm_mixer(hp, mlstm_w_in[j], mlstm_gate_b[j], mlstm_head_g[j], mlstm_w_out[j],
                                            zC, zn, zm)
            os_, _, _, _ = _mlstm_mixer(hs, mlstm_w_in[j], mlstm_gate_b[j], mlstm_head_g[j], mlstm_w_out[j],
                                        state_mlstm_C[:, j], state_mlstm_n[:, j], state_mlstm_m[:, j])
            new_C.append(Cp)
            new_n.append(n_p)
            new_m.append(m_p)
        else:
            op, kp, vp = _attn_context(hp, attn_w_qkv[j], attn_q_g[j], attn_k_g[j], attn_sink[j], attn_w_o[j])
            os_ = _attn_latent(hs, cache_attn_k[:, j], cache_attn_v[:, j], attn_w_qkv[j], attn_q_g[j],
                               attn_k_g[j], attn_sink[j], attn_w_o[j])
            new_k.append(kp)
            new_v.append(vp)
        xp = xp + g1p * op
        xs = xs + g1s * os_
        hp = _modulate(_rmsnorm(xp, norm_ffn[i]), sh2p, sc2p)
        hs = _modulate(_rmsnorm(xs, norm_ffn[i]), sh2s, sc2s)
        moe_args = (moe_router_w[i], moe_router_b[i], moe_w1[i], moe_w3[i], moe_w2[i],
                    shared_w1[i], shared_w3[i], shared_w2[i])
        xp = xp + g2p * _moe(hp.reshape(-1, D_MODEL), *moe_args).reshape(xp.shape)
        xs = xs + g2s * _moe(hs.reshape(-1, D_MODEL), *moe_args).reshape(xs.shape)
    return (xp, xs, jnp.stack(new_C, axis=1), jnp.stack(new_n, axis=1), jnp.stack(new_m, axis=1),
            jnp.stack(new_k, axis=1), jnp.stack(new_v, axis=1))
```

```python
import functools

import jax
import jax.numpy as jnp
from jax import lax
from jax.experimental import pallas as pl
from jax.experimental.pallas import tpu as pltpu

F32 = jnp.float32
BF16 = jnp.bfloat16
I32 = jnp.int32

D_MODEL = 2048
DEPTH = 2
EPS = 1e-6
GRID_W = 64
MLSTM_HEADS = 8
MLSTM_DK = 128
MLSTM_DV = 256
MLSTM_QK = MLSTM_HEADS * MLSTM_DK
MLSTM_V = MLSTM_HEADS * MLSTM_DV
MLSTM_MAIN = 2 * MLSTM_QK + 2 * MLSTM_V
GATE_PAD = 128
HEAD_DIM = 128
ATTN_Q_HEADS = 16
ATTN_KV_HEADS = 4
ATTN_GROUPS = 4
WINDOW = 128
QBLK = 128
ROPE_THETA = 10000.0
ATTN_Q = ATTN_Q_HEADS * HEAD_DIM
ATTN_KV = ATTN_KV_HEADS * HEAD_DIM
ATTN_PROJ = ATTN_Q + 2 * ATTN_KV
N_EXPERTS = 64
TOP_K = 8
N_EXPERT_GROUPS = 8
TOPK_GROUPS = 4
GROUP_SIZE = N_EXPERTS // N_EXPERT_GROUPS
EXPERT_FF = 512
ROUTED_SCALE = 2.5
MOE_BLOCK = 256
COMBINE_TM = 128

V7X_VMEM_LIMIT = 56 * 1024 * 1024
NEG_INF = float("-inf")


def _cparams(sem):
    return pltpu.CompilerParams(dimension_semantics=sem, vmem_limit_bytes=V7X_VMEM_LIMIT)


def _split_hi_lo(a):
    hi = a.astype(BF16)
    lo = (a - hi.astype(F32)).astype(BF16)
    return hi, lo


def _dot(a, b):
    return jnp.dot(a, b, preferred_element_type=F32)


def _dot_nt(a, b):
    return lax.dot_general(a, b, (((1,), (1,)), ((), ())), preferred_element_type=F32)


def _dot_tn(a, b):
    return lax.dot_general(a, b, (((0,), (0,)), ((), ())), preferred_element_type=F32)


def _silu(x):
    return x * jax.nn.sigmoid(x)


def _row_tile(preferred, n_prompt, dec_seq):
    tm = preferred
    while n_prompt % tm or dec_seq % tm:
        tm //= 2
    return tm


def _mod_row(i, tm, n_prompt, dec_seq):
    r0 = i * tm
    return jnp.where(r0 < n_prompt, 0, 1 + (r0 - n_prompt) // dec_seq)


def _norm_modulate(x, gain, mod_ref, shift_idx, scale_idx):
    D = D_MODEL
    y = x * lax.rsqrt(jnp.mean(x * x, axis=-1, keepdims=True) + EPS) * gain
    shift = mod_ref[:, shift_idx * D:(shift_idx + 1) * D]
    scale = mod_ref[:, scale_idx * D:(scale_idx + 1) * D]
    return y * (1.0 + scale) + shift


def _ada_kernel(c_ref, w_ref, b_ref, o_ref):
    s = _silu(c_ref[...])
    s_hi, s_lo = _split_hi_lo(s)
    w_hi, w_lo = _split_hi_lo(w_ref[...])
    o_ref[...] = _dot(s_hi, w_hi) + _dot(s_hi, w_lo) + _dot(s_lo, w_hi) + b_ref[...]


def _ada_table(cvec, ada_w, ada_b):
    D = D_MODEL
    tn = 1024
    rows = cvec.shape[0]
    return pl.pallas_call(
        _ada_kernel,
        grid=(DEPTH, 6 * D // tn),
        in_specs=[
            pl.BlockSpec((rows, D), lambda l, j: (0, 0)),
            pl.BlockSpec((None, D, tn), lambda l, j: (l, 0, j)),
            pl.BlockSpec((None, 1, tn), lambda l, j: (l, 0, j)),
        ],
        out_specs=pl.BlockSpec((None, rows, tn), lambda l, j: (l, 0, j)),
        out_shape=jax.ShapeDtypeStruct((DEPTH, rows, 6 * D), F32),
        compiler_params=_cparams(("parallel", "parallel")),
        name="ada_table",
    )(cvec, ada_w, ada_b.reshape(DEPTH, 1, 6 * D))


def _nm_matmul_kernel(x_ref, gain_ref, mod_ref, w_ref, cs_ref, cb_ref, *rest, shift_idx, scale_idx, has_aux):
    if has_aux:
        wa_ref, ab_ref, o_ref, aux_ref, h_scr = rest
    else:
        o_ref, h_scr = rest

    @pl.when(pl.program_id(1) == 0)
    def _():
        h = _norm_modulate(x_ref[...], gain_ref[...], mod_ref, shift_idx, scale_idx).astype(BF16)
        h_scr[...] = h
        if has_aux:
            aux_ref[...] = _dot(h, wa_ref[...]) + ab_ref[...]

    acc = _dot(h_scr[...], w_ref[...])
    o_ref[...] = (acc * cs_ref[...] + cb_ref[...]).astype(o_ref.dtype)


def _nm_matmul(x, gain, mod, w, col_scale, col_bias, *, shift_idx, scale_idx, n_prompt, dec_seq,
               out_dtype, tm, tn, aux_w=None, aux_b=None, name):
    N, D = x.shape
    P = w.shape[1]
    assert N % tm == 0 and P % tn == 0 and n_prompt % tm == 0 and dec_seq % tm == 0
    has_aux = aux_w is not None
    row = functools.partial(_mod_row, tm=tm, n_prompt=n_prompt, dec_seq=dec_seq)
    in_specs = [
        pl.BlockSpec((tm, D), lambda i, j: (i, 0)),
        pl.BlockSpec((1, D), lambda i, j: (0, 0)),
        pl.BlockSpec((None, 1, 6 * D), lambda i, j: (row(i), 0, 0)),
        pl.BlockSpec((D, tn), lambda i, j: (0, j)),
        pl.BlockSpec((1, tn), lambda i, j: (0, j)),
        pl.BlockSpec((1, tn), lambda i, j: (0, j)),
    ]
    args = [x, gain.reshape(1, D), mod, w, col_scale.reshape(1, P), col_bias.reshape(1, P)]
    out_specs = pl.BlockSpec((tm, tn), lambda i, j: (i, j))
    out_shape = jax.ShapeDtypeStruct((N, P), out_dtype)
    if has_aux:
        PA = aux_w.shape[1]
        in_specs += [pl.BlockSpec((D, PA), lambda i, j: (0, 0)), pl.BlockSpec((1, PA), lambda i, j: (0, 0))]
        args += [aux_w, aux_b.reshape(1, PA)]
        out_specs = [out_specs, pl.BlockSpec((tm, PA), lambda i, j: (i, 0))]
        out_shape = [out_shape, jax.ShapeDtypeStruct((N, PA), F32)]
    return pl.pallas_call(
        functools.partial(_nm_matmul_kernel, shift_idx=shift_idx, scale_idx=scale_idx, has_aux=has_aux),
        grid=(N // tm, P // tn),
        in_specs=in_specs,
        out_specs=out_specs,
        out_shape=out_shape,
        scratch_shapes=[pltpu.VMEM((tm, D), BF16)],
        compiler_params=_cparams(("parallel", "arbitrary")),
        name=name,
    )(*args)


def _mm_residual_kernel(*refs, mlstm_prologue):
    if mlstm_prologue:
        hf_ref, hb_ref, og_ref, hg_ref, w_ref, x_ref, gate_ref, o_ref, l_scr = refs

        @pl.when(pl.program_id(1) == 0)
        def _():
            hs = hf_ref[...] + hb_ref[...]
            og = jax.nn.sigmoid(og_ref[...].astype(F32))
            for h in range(MLSTM_HEADS):
                sl = slice(h * MLSTM_DV, (h + 1) * MLSTM_DV)
                hh = hs[:, sl]
                hn = hh * lax.rsqrt(jnp.mean(hh * hh, axis=-1, keepdims=True) + EPS) * hg_ref[:, sl]
                l_scr[:, sl] = (og[:, sl] * hn).astype(BF16)

        lhs = l_scr[...]
    else:
        l_ref, w_ref, x_ref, gate_ref, o_ref = refs
        lhs = l_ref[...]
    o_ref[...] = x_ref[...] + gate_ref[...] * _dot(lhs, w_ref[...])


def _mm_residual(lhs_args, w, x, mod, *, gate_idx, n_prompt, dec_seq, tm, tn, mlstm_prologue, name):
    N, D = x.shape
    K = w.shape[0]
    assert N % tm == 0 and D % tn == 0 and n_prompt % tm == 0 and dec_seq % tm == 0
    row = functools.partial(_mod_row, tm=tm, n_prompt=n_prompt, dec_seq=dec_seq)
    if mlstm_prologue:
        hf, hb, qkvo, head_g = lhs_args
        o_blk = (2 * MLSTM_QK + MLSTM_V) // MLSTM_V
        in_specs = [
            pl.BlockSpec((tm, K), lambda i, j: (i, 0)),
            pl.BlockSpec((tm, K), lambda i, j: (i, 0)),
            pl.BlockSpec((tm, MLSTM_V), lambda i, j: (i, o_blk)),
            pl.BlockSpec((1, K), lambda i, j: (0, 0)),
        ]
        args = [hf, hb, qkvo, head_g.reshape(1, K)]
        scratch = [pltpu.VMEM((tm, K), BF16)]
    else:
        (lhs,) = lhs_args
        in_specs = [pl.BlockSpec((tm, K), lambda i, j: (i, 0))]
        args = [lhs]
        scratch = []
    in_specs += [
        pl.BlockSpec((K, tn), lambda i, j: (0, j)),
        pl.BlockSpec((tm, tn), lambda i, j: (i, j)),
        pl.BlockSpec((None, 1, tn), lambda i, j: (row(i), 0, gate_idx * (D // tn) + j)),
    ]
    args += [w, x, mod]
    return pl.pallas_call(
        functools.partial(_mm_residual_kernel, mlstm_prologue=mlstm_prologue),
        grid=(N // tm, D // tn),
        in_specs=in_specs,
        out_specs=pl.BlockSpec((tm, tn), lambda i, j: (i, j)),
        out_shape=jax.ShapeDtypeStruct((N, D), F32),
        scratch_shapes=scratch,
        compiler_params=_cparams(("parallel", "arbitrary")),
        name=name,
    )(*args)


def _log_sigmoid(x):
    return jnp.minimum(x, 0.0) - jnp.log(1.0 + jnp.exp(-jnp.abs(x)))


def _mlstm_direction(d, q_ref, k_ref, v_ref, g_ref, h_ref, C_scr, n_scr, m_scr, L):
    H, DK, DV = MLSTM_HEADS, MLSTM_DK, MLSTM_DV
    g = g_ref[...]
    lf = _log_sigmoid(g)
    r = lax.broadcasted_iota(I32, (L, L), 0)
    c = lax.broadcasted_iota(I32, (L, L), 1)
    causal = (c <= r) if d == 0 else (c >= r)
    tri = jnp.where(causal, 1.0, 0.0).astype(BF16)
    lf1 = lf.astype(BF16)
    rem = lf - lf1.astype(F32)
    lf2 = rem.astype(BF16)
    lf3 = (rem - lf2.astype(F32)).astype(BF16)
    bsum = _dot(tri, lf1) + _dot(tri, lf2) + _dot(tri, lf3)
    g_t = g.T
    b_t = bsum.T
    end = L - 1 if d == 0 else 0
    for h in range(H):
        ci, cf = d * 2 * H + h, d * 2 * H + H + h
        b_col, i_col = bsum[:, cf:cf + 1], g[:, ci:ci + 1]
        b_row, i_row = b_t[cf:cf + 1, :], g_t[ci:ci + 1, :]
        b_end = bsum[end:end + 1, cf:cf + 1]
        m = m_scr[d, h]
        C = C_scr[d, h]
        n = n_scr[d, h]
        qh = q_ref[:, h * DK:(h + 1) * DK]
        kh = k_ref[:, h * DK:(h + 1) * DK]
        vh = v_ref[:, h * DV:(h + 1) * DV]
        dmat = jnp.where(causal, b_col - b_row + i_row, NEG_INF)
        inter = b_col + m
        m_out = jnp.maximum(inter, jnp.max(dmat, axis=-1, keepdims=True))
        w = jnp.exp(dmat - m_out) * _dot_nt(qh, kh)
        dec = jnp.exp(inter - m_out)
        num = _dot(w.astype(BF16), vh) + dec * _dot(qh, C.astype(BF16))
        den = jnp.sum(w, axis=-1, keepdims=True) + dec * jnp.sum(qh.astype(F32) * n, axis=-1, keepdims=True)
        h_ref[:, h * DV:(h + 1) * DV] = num / jnp.maximum(jnp.abs(den), jnp.exp(-m_out))
        to_end = b_end - b_col + i_col
        m_new = jnp.maximum(b_end + m, jnp.max(to_end, axis=0, keepdims=True))
        wk = jnp.exp(to_end - m_new)
        dec_end = jnp.exp(b_end + m - m_new)
        kw = kh.astype(F32) * wk
        C_scr[d, h] = dec_end * C + _dot_tn(kw.astype(BF16), vh)
        n_scr[d, h] = dec_end * n + jnp.sum(kw, axis=0, keepdims=True)
        m_scr[d, h] = m_new


def _mlstm_kernel(*refs, L, has_state, emit_state):
    H = MLSTM_HEADS
    qf, kf, vf, gf, qb, kb, vb, gb = refs[:8]
    pos = 8
    if has_state:
        C0, n0, m0 = refs[pos:pos + 3]
        pos += 3
    hf, hb = refs[pos:pos + 2]
    pos += 2
    if emit_state:
        Co, no, mo = refs[pos:pos + 3]
        pos += 3
    C_scr, n_scr, m_scr = refs[pos:pos + 3]
    c = pl.program_id(1)

    @pl.when(c == 0)
    def _():
        if has_state:
            C_scr[...] = C0[...]
            for d in range(2):
                for h in range(H):
                    n_scr[d, h] = n0[d, h:h + 1, :]
                    m_scr[d, h] = m0[d:d + 1, h:h + 1]
        else:
            C_scr[...] = jnp.zeros_like(C_scr)
            n_scr[...] = jnp.zeros_like(n_scr)
            m_scr[...] = jnp.zeros_like(m_scr)

    _mlstm_direction(0, qf, kf, vf, gf, hf, C_scr, n_scr, m_scr, L)
    _mlstm_direction(1, qb, kb, vb, gb, hb, C_scr, n_scr, m_scr, L)

    if emit_state:
        @pl.when(c == pl.num_programs(1) - 1)
        def _():
            Co[...] = C_scr[...]
            for d in range(2):
                for h in range(H):
                    no[d, h:h + 1, :] = n_scr[d, h]
                    mo[d:d + 1, h:h + 1] = m_scr[d, h]


def _mlstm_scan(qkvo, gates, row0, B, S, L, state=None, emit_state=False, name="mlstm"):
    H, DK, DV = MLSTM_HEADS, MLSTM_DK, MLSTM_DV
    assert S % L == 0 and row0 % L == 0
    nc = S // L
    base = row0 // L
    fwd = lambda b, c: base + b * nc + c
    bwd = lambda b, c: base + b * nc + (nc - 1 - c)

    def specs(rowf):
        return [
            pl.BlockSpec((L, MLSTM_QK), lambda b, c: (rowf(b, c), 0)),
            pl.BlockSpec((L, MLSTM_QK), lambda b, c: (rowf(b, c), 1)),
            pl.BlockSpec((L, MLSTM_V), lambda b, c: (rowf(b, c), 1)),
            pl.BlockSpec((L, GATE_PAD), lambda b, c: (rowf(b, c), 0)),
        ]

    in_specs = specs(fwd) + specs(bwd)
    args = [qkvo, qkvo, qkvo, gates] * 2
    if state is not None:
        C0, n0, m0 = state
        in_specs += [
            pl.BlockSpec((None, 2, H, DK, DV), lambda b, c: (b, 0, 0, 0, 0)),
            pl.BlockSpec((None, 2, H, DK), lambda b, c: (b, 0, 0, 0)),
            pl.BlockSpec((None, 2, H), lambda b, c: (b, 0, 0)),
        ]
        args += [C0, n0, m0]
    out_specs = [
        pl.BlockSpec((L, MLSTM_V), lambda b, c: (b * nc + c, 0)),
        pl.BlockSpec((L, MLSTM_V), lambda b, c: (b * nc + (nc - 1 - c), 0)),
    ]
    out_shape = [jax.ShapeDtypeStruct((B * S, MLSTM_V), F32)] * 2
    if emit_state:
        out_specs += [
            pl.BlockSpec((None, 2, H, DK, DV), lambda b, c: (b, 0, 0, 0, 0)),
            pl.BlockSpec((None, 2, H, DK), lambda b, c: (b, 0, 0, 0)),
            pl.BlockSpec((None, 2, H), lambda b, c: (b, 0, 0)),
        ]
        out_shape += [
            jax.ShapeDtypeStruct((B, 2, H, DK, DV), F32),
            jax.ShapeDtypeStruct((B, 2, H, DK), F32),
            jax.ShapeDtypeStruct((B, 2, H), F32),
        ]
    return pl.pallas_call(
        functools.partial(_mlstm_kernel, L=L, has_state=state is not None, emit_state=emit_state),
        grid=(B, nc),
        in_specs=in_specs,
        out_specs=out_specs,
        out_shape=out_shape,
        scratch_shapes=[
            pltpu.VMEM((2, H, DK, DV), F32),
            pltpu.VMEM((2, H, 1, DK), F32),
            pltpu.VMEM((2, H, 1, 1), F32),
        ],
        compiler_params=_cparams(("parallel", "arbitrary")),
        name=name,
    )(*args)


def _qk_prep_kernel(qkv_ref, qg_ref, kg_ref, cos_ref, sin_ref, *outs, emit_f32):
    if emit_f32:
        q_ref, k_ref, v_ref, kf_ref, vf_ref = outs
    else:
        q_ref, k_ref, v_ref = outs
    cos = cos_ref[...]
    sin = sin_ref[...]
    lane = lax.broadcasted_iota(I32, cos.shape, 1)
    first = (lane % (HEAD_DIM // 2)) < (HEAD_DIM // 4)

    def norm_rope(x, gain):
        xn = x * lax.rsqrt(jnp.mean(x * x, axis=-1, keepdims=True) + EPS) * gain
        partner = jnp.where(first, pltpu.roll(xn, HEAD_DIM - HEAD_DIM // 4, 1), pltpu.roll(xn, HEAD_DIM // 4, 1))
        return xn * cos + partner * sin

    for h in range(ATTN_Q_HEADS):
        sl = slice(h * HEAD_DIM, (h + 1) * HEAD_DIM)
        q_ref[:, sl] = norm_rope(qkv_ref[:, sl], qg_ref[...]).astype(BF16)
    for h in range(ATTN_KV_HEADS):
        sl = slice(h * HEAD_DIM, (h + 1) * HEAD_DIM)
        kx = norm_rope(qkv_ref[:, ATTN_Q + h * HEAD_DIM:ATTN_Q + (h + 1) * HEAD_DIM], kg_ref[...])
        k_ref[:, sl] = kx.astype(BF16)
        if emit_f32:
            kf_ref[:, sl] = kx
    vx = qkv_ref[:, ATTN_Q + ATTN_KV:]
    v_ref[...] = vx.astype(BF16)
    if emit_f32:
        vf_ref[...] = vx


def _qk_prep(qkv, q_g, k_g, cos, sin, row0, rows, tm, emit_f32, name):
    assert rows % tm == 0 and row0 % tm == 0 and cos.shape[0] % tm == 0
    base = row0 // tm
    nt = cos.shape[0] // tm
    out_specs = [
        pl.BlockSpec((tm, ATTN_Q), lambda i: (i, 0)),
        pl.BlockSpec((tm, ATTN_KV), lambda i: (i, 0)),
        pl.BlockSpec((tm, ATTN_KV), lambda i: (i, 0)),
    ]
    out_shape = [
        jax.ShapeDtypeStruct((rows, ATTN_Q), BF16),
        jax.ShapeDtypeStruct((rows, ATTN_KV), BF16),
        jax.ShapeDtypeStruct((rows, ATTN_KV), BF16),
    ]
    if emit_f32:
        out_specs += [pl.BlockSpec((tm, ATTN_KV), lambda i: (i, 0))] * 2
        out_shape += [jax.ShapeDtypeStruct((rows, ATTN_KV), F32)] * 2
    return pl.pallas_call(
        functools.partial(_qk_prep_kernel, emit_f32=emit_f32),
        grid=(rows // tm,),
        in_specs=[
            pl.BlockSpec((tm, ATTN_PROJ), lambda i: (base + i, 0)),
            pl.BlockSpec((1, HEAD_DIM), lambda i: (0, 0)),
            pl.BlockSpec((1, HEAD_DIM), lambda i: (0, 0)),
            pl.BlockSpec((tm, HEAD_DIM), lambda i: (i % nt, 0)),
            pl.BlockSpec((tm, HEAD_DIM), lambda i: (i % nt, 0)),
        ],
        out_specs=out_specs,
        out_shape=out_shape,
        compiler_params=_cparams(("parallel",)),
        name=name,
    )(qkv, q_g.reshape(1, HEAD_DIM), k_g.reshape(1, HEAD_DIM), cos, sin)


def _sink_column(sink_ref, kv, rows_per_head):
    parts = [jnp.full((rows_per_head, 1), sink_ref[kv * ATTN_GROUPS + g], F32) for g in range(ATTN_GROUPS)]
    return jnp.concatenate(parts, axis=0)


def _attn_ctx_kernel(sink_ref, q_ref, k_ref, v_ref, o_ref):
    S = q_ref.shape[0]
    scale = HEAD_DIM ** -0.5
    for kv in range(ATTN_KV_HEADS):
        q = jnp.concatenate(
            [q_ref[:, (kv * ATTN_GROUPS + g) * HEAD_DIM:(kv * ATTN_GROUPS + g + 1) * HEAD_DIM]
             for g in range(ATTN_GROUPS)], axis=0)
        ksl = slice(kv * HEAD_DIM, (kv + 1) * HEAD_DIM)
        s = _dot_nt(q, k_ref[:, ksl]) * scale
        sk = _sink_column(sink_ref, kv, S)
        m = jnp.maximum(jnp.max(s, axis=-1, keepdims=True), sk)
        p = jnp.exp(s - m)
        den = jnp.sum(p, axis=-1, keepdims=True) + jnp.exp(sk - m)
        o = _dot((p / den).astype(BF16), v_ref[:, ksl])
        for g in range(ATTN_GROUPS):
            hq = kv * ATTN_GROUPS + g
            o_ref[:, hq * HEAD_DIM:(hq + 1) * HEAD_DIM] = o[g * S:(g + 1) * S].astype(BF16)


def _attn_context(sink, q, k, v, B, S):
    return pl.pallas_call(
        _attn_ctx_kernel,
        grid_spec=pltpu.PrefetchScalarGridSpec(
            num_scalar_prefetch=0,
            grid=(B,),
            in_specs=[
                pl.BlockSpec(memory_space=pltpu.SMEM),
                pl.BlockSpec((S, ATTN_Q), lambda b: (b, 0)),
                pl.BlockSpec((S, ATTN_KV), lambda b: (b, 0)),
                pl.BlockSpec((S, ATTN_KV), lambda b: (b, 0)),
            ],
            out_specs=pl.BlockSpec((S, ATTN_Q), lambda b: (b, 0)),
        ),
        out_shape=jax.ShapeDtypeStruct((B * S, ATTN_Q), BF16),
        compiler_params=_cparams(("parallel",)),
        name="attn_context",
    )(sink, q, k, v)


def _attn_lat_kernel(sink_ref, q_ref, kp_ref, kc_ref, kn_ref, vp_ref, vc_ref, vn_ref, kx_ref, vx_ref, o_ref):
    i = pl.program_id(1)
    nb = pl.num_programs(1)
    scale = HEAD_DIM ** -0.5
    R = ATTN_GROUPS * QBLK
    r = lax.broadcasted_iota(I32, (R, QBLK), 0) % QBLK
    c = lax.broadcasted_iota(I32, (R, QBLK), 1)
    ok_prev = c >= r
    ok_next = c <= r
    edge_prev = jnp.where(i > 0, 0.0, NEG_INF)
    edge_next = jnp.where(i < nb - 1, 0.0, NEG_INF)
    for kv in range(ATTN_KV_HEADS):
        q = jnp.concatenate(
            [q_ref[:, (kv * ATTN_GROUPS + g) * HEAD_DIM:(kv * ATTN_GROUPS + g + 1) * HEAD_DIM]
             for g in range(ATTN_GROUPS)], axis=0)
        ksl = slice(kv * HEAD_DIM, (kv + 1) * HEAD_DIM)
        s_p = jnp.where(ok_prev, _dot_nt(q, kp_ref[:, ksl]) * scale + edge_prev, NEG_INF)
        s_c = _dot_nt(q, kc_ref[:, ksl]) * scale
        s_n = jnp.where(ok_next, _dot_nt(q, kn_ref[:, ksl]) * scale + edge_next, NEG_INF)
        s_x = _dot_nt(q, kx_ref[:, ksl]) * scale
        sk = _sink_column(sink_ref, kv, QBLK)
        m = jnp.maximum(
            jnp.maximum(jnp.max(s_p, axis=-1, keepdims=True), jnp.max(s_c, axis=-1, keepdims=True)),
            jnp.maximum(jnp.max(s_n, axis=-1, keepdims=True), jnp.max(s_x, axis=-1, keepdims=True)))
        m = jnp.maximum(m, sk)
        p_p, p_c, p_n, p_x = jnp.exp(s_p - m), jnp.exp(s_c - m), jnp.exp(s_n - m), jnp.exp(s_x - m)
        den = (jnp.sum(p_p, axis=-1, keepdims=True) + jnp.sum(p_c, axis=-1, keepdims=True)
               + jnp.sum(p_n, axis=-1, keepdims=True) + jnp.sum(p_x, axis=-1, keepdims=True) + jnp.exp(sk - m))
        inv = 1.0 / den
        o = (_dot((p_p * inv).astype(BF16), vp_ref[:, ksl]) + _dot((p_c * inv).astype(BF16), vc_ref[:, ksl])
             + _dot((p_n * inv).astype(BF16), vn_ref[:, ksl]) + _dot((p_x * inv).astype(BF16), vx_ref[:, ksl]))
        for g in range(ATTN_GROUPS):
            hq = kv * ATTN_GROUPS + g
            o_ref[:, hq * HEAD_DIM:(hq + 1) * HEAD_DIM] = o[g * QBLK:(g + 1) * QBLK].astype(BF16)


def _attn_latent(sink, q, k, v, ctx_k, ctx_v, B, S):
    nb = S // QBLK
    P = ctx_k.shape[1]
    prev = lambda b, i: (b * nb + jnp.maximum(i - 1, 0), 0)
    cur = lambda b, i: (b * nb + i, 0)
    nxt = lambda b, i: (b * nb + jnp.minimum(i + 1, nb - 1), 0)
    kv_spec = lambda f: pl.BlockSpec((QBLK, ATTN_KV), f)
    return pl.pallas_call(
        _attn_lat_kernel,
        grid_spec=pltpu.PrefetchScalarGridSpec(
            num_scalar_prefetch=0,
            grid=(B, nb),
            in_specs=[
                pl.BlockSpec(memory_space=pltpu.SMEM),
                pl.BlockSpec((QBLK, ATTN_Q), cur),
                kv_spec(prev), kv_spec(cur), kv_spec(nxt),
                kv_spec(prev), kv_spec(cur), kv_spec(nxt),
                pl.BlockSpec((None, P, ATTN_KV), lambda b, i: (b, 0, 0)),
                pl.BlockSpec((None, P, ATTN_KV), lambda b, i: (b, 0, 0)),
            ],
            out_specs=pl.BlockSpec((QBLK, ATTN_Q), cur),
        ),
        out_shape=jax.ShapeDtypeStruct((B * S, ATTN_Q), BF16),
        compiler_params=_cparams(("parallel", "parallel")),
        name="attn_latent",
    )(sink, q, k, k, k, v, v, v, ctx_k, ctx_v)


def _router_kernel(x_ref, gain_ref, mod_ref, wr_ref, rb_ref, h_ref, e_ref, g_ref, r_ref, cnt_ref, cnt_scr, *, tm):
    E, GS = N_EXPERTS, GROUP_SIZE
    i = pl.program_id(0)

    @pl.when(i == 0)
    def _():
        cnt_scr[...] = jnp.zeros_like(cnt_scr)

    h = _norm_modulate(x_ref[...], gain_ref[...], mod_ref, 3, 4)
    h_ref[...] = h
    h_hi, h_lo = _split_hi_lo(h)
    w_hi, w_lo = _split_hi_lo(wr_ref[...])
    logits = _dot_nt(w_hi, h_hi) + _dot_nt(w_lo, h_hi) + _dot_nt(w_hi, h_lo)
    scores = jax.nn.sigmoid(logits)
    biased = scores + rb_ref[...]
    sub = lax.broadcasted_iota(I32, (GS, tm), 0)
    gscore = []
    for gi in range(N_EXPERT_GROUPS):
        xg = biased[gi * GS:(gi + 1) * GS, :]
        m1 = jnp.max(xg, axis=0, keepdims=True)
        first = jnp.min(jnp.where(xg == m1, sub, GS), axis=0, keepdims=True)
        m2 = jnp.max(jnp.where(sub == first, NEG_INF, xg), axis=0, keepdims=True)
        gscore.append(m1 + m2)
    cur = jnp.concatenate(gscore, axis=0)
    gid = lax.broadcasted_iota(I32, (N_EXPERT_GROUPS, tm), 0)
    gsel = jnp.zeros((N_EXPERT_GROUPS, tm), F32)
    for _ in range(TOPK_GROUPS):
        mx = jnp.max(cur, axis=0, keepdims=True)
        first = jnp.min(jnp.where(cur == mx, gid, N_EXPERT_GROUPS), axis=0, keepdims=True)
        hit = gid == first
        gsel = jnp.where(hit, 1.0, gsel)
        cur = jnp.where(hit, NEG_INF, cur)
    ok = jnp.concatenate(
        [jnp.broadcast_to(gsel[gi:gi + 1, :], (GS, tm)) for gi in range(N_EXPERT_GROUPS)], axis=0)
    masked = jnp.where(ok > 0.5, biased, NEG_INF)
    eid = lax.broadcasted_iota(I32, (E, tm), 0)
    sel = jnp.zeros((E, tm), F32)
    picks, pick_scores = [], []
    for _ in range(TOP_K):
        mx = jnp.max(masked, axis=0, keepdims=True)
        first = jnp.min(jnp.where(masked == mx, eid, E), axis=0, keepdims=True)
        hit = eid == first
        picks.append(first)
        pick_scores.append(jnp.sum(jnp.where(hit, scores, 0.0), axis=0, keepdims=True))
        sel = jnp.where(hit, 1.0, sel)
        masked = jnp.where(hit, NEG_INF, masked)
    total = pick_scores[0]
    for s in pick_scores[1:]:
        total = total + s
    g_ref[...] = jnp.concatenate(pick_scores, axis=0) / total * ROUTED_SCALE
    e_ref[...] = jnp.concatenate(picks, axis=0)
    rr = lax.broadcasted_iota(I32, (tm, tm), 0)
    cc = lax.broadcasted_iota(I32, (tm, tm), 1)
    before = jnp.where(rr < cc, 1.0, 0.0).astype(BF16)
    rank = _dot(sel.astype(BF16), before) + cnt_scr[...]
    r_ref[...] = jnp.concatenate(
        [jnp.sum(jnp.where(eid == p, rank, 0.0), axis=0, keepdims=True) for p in picks], axis=0).astype(I32)
    cnt_scr[...] = cnt_scr[...] + jnp.sum(sel, axis=-1, keepdims=True)
    cnt_ref[...] = cnt_scr[...].astype(I32)


def _router(x, gain, mod, router_w_t, router_b, *, n_prompt, dec_seq, tm):
    N, D = x.shape
    E = N_EXPERTS
    assert N % tm == 0 and n_prompt % tm == 0 and dec_seq % tm == 0
    row = functools.partial(_mod_row, tm=tm, n_prompt=n_prompt, dec_seq=dec_seq)
    return pl.pallas_call(
        functools.partial(_router_kernel, tm=tm),
        grid=(N // tm,),
        in_specs=[
            pl.BlockSpec((tm, D), lambda i: (i, 0)),
            pl.BlockSpec((1, D), lambda i: (0, 0)),
            pl.BlockSpec((None, 1, 6 * D), lambda i: (row(i), 0, 0)),
            pl.BlockSpec((E, D), lambda i: (0, 0)),
            pl.BlockSpec((E, 1), lambda i: (0, 0)),
        ],
        out_specs=[
            pl.BlockSpec((tm, D), lambda i: (i, 0)),
            pl.BlockSpec((TOP_K, tm), lambda i: (0, i)),
            pl.BlockSpec((TOP_K, tm), lambda i: (0, i)),
            pl.BlockSpec((TOP_K, tm), lambda i: (0, i)),
            pl.BlockSpec((E, 1), lambda i: (0, 0)),
        ],
        out_shape=[
            jax.ShapeDtypeStruct((N, D), F32),
            jax.ShapeDtypeStruct((TOP_K, N), I32),
            jax.ShapeDtypeStruct((TOP_K, N), F32),
            jax.ShapeDtypeStruct((TOP_K, N), I32),
            jax.ShapeDtypeStruct((E, 1), I32),
        ],
        scratch_shapes=[pltpu.VMEM((E, 1), F32)],
        compiler_params=_cparams(("arbitrary",)),
        name="moe_router",
    )(x, gain.reshape(1, D), mod, router_w_t, router_b.reshape(E, 1))


def _row_copy(src, dst, sem, src_row, dst_row):
    return pltpu.make_async_copy(src.at[pl.ds(src_row, 1), :], dst.at[pl.ds(dst_row, 1), :], sem)


def _experts_kernel(be_ref, nused_ref, tok_ref, h_hbm, w1_ref, w3_ref, w2_ref, y_ref,
                    xbuf, sem, w1c, w3c, w2c, prev_e):
    b = pl.program_id(0)
    e = be_ref[b]
    BLK = xbuf.shape[0]
    U = 8

    @pl.when(b == 0)
    def _():
        prev_e[0] = -1

    @pl.when(b < nused_ref[0])
    def _():
        def issue(it, carry):
            for u in range(U):
                r = it * U + u
                _row_copy(h_hbm, xbuf, sem, tok_ref[0, r], r).start()
            return carry

        lax.fori_loop(0, BLK // U, issue, 0)

        @pl.when(e != prev_e[0])
        def _():
            w1c[...] = w1_ref[...].astype(BF16)
            w3c[...] = w3_ref[...].astype(BF16)
            w2c[...] = w2_ref[...].astype(BF16)
            prev_e[0] = e

        for r in range(BLK):
            _row_copy(h_hbm, xbuf, sem, 0, r).wait()
        x = xbuf[...].astype(BF16)
        mid = (_silu(_dot(x, w1c[...])) * _dot(x, w3c[...])).astype(BF16)
        y_ref[...] = _dot(mid, w2c[...])

    @pl.when(b >= nused_ref[0])
    def _():
        y_ref[...] = jnp.zeros_like(y_ref)


def _experts(block_expert, n_used, slot_tok, h, w1, w3, w2):
    n_blocks = block_expert.shape[0]
    BLK = MOE_BLOCK
    D = h.shape[1]
    FF = w1.shape[2]
    return pl.pallas_call(
        _experts_kernel,
        grid_spec=pltpu.PrefetchScalarGridSpec(
            num_scalar_prefetch=2,
            grid=(n_blocks,),
            in_specs=[
                pl.BlockSpec((None, 1, BLK), lambda b, be, nu: (b, 0, 0), memory_space=pltpu.SMEM),
                pl.BlockSpec(memory_space=pl.ANY),
                pl.BlockSpec((None, D, FF), lambda b, be, nu: (be[b], 0, 0)),
                pl.BlockSpec((None, D, FF), lambda b, be, nu: (be[b], 0, 0)),
                pl.BlockSpec((None, FF, D), lambda b, be, nu: (be[b], 0, 0)),
            ],
            out_specs=pl.BlockSpec((BLK, D), lambda b, be, nu: (b, 0)),
            scratch_shapes=[
                pltpu.VMEM((BLK, D), F32),
                pltpu.SemaphoreType.DMA,
                pltpu.VMEM((D, FF), BF16),
                pltpu.VMEM((D, FF), BF16),
                pltpu.VMEM((FF, D), BF16),
                pltpu.SMEM((1,), I32),
            ],
        ),
        out_shape=jax.ShapeDtypeStruct((n_blocks * BLK, D), F32),
        compiler_params=_cparams(("arbitrary",)),
        name="moe_experts",
    )(block_expert, n_used, slot_tok.reshape(n_blocks, 1, BLK), h, w1, w3, w2)


def _combine_kernel(dest_ref, gates_ref, y_hbm, h_ref, x_ref, gate_ref, ws1_ref, ws3_ref, ws2_ref, o_ref,
                    ybuf, sem):
    tm = h_ref.shape[0]
    U = 8

    def issue(it, carry):
        for k in range(TOP_K):
            for u in range(U):
                r = it * U + u
                _row_copy(y_hbm, ybuf.at[k], sem, dest_ref[k, r], r).start()
        return carry

    lax.fori_loop(0, tm // U, issue, 0)
    hb = h_ref[...].astype(BF16)
    mid = (_silu(_dot(hb, ws1_ref[...])) * _dot(hb, ws3_ref[...])).astype(BF16)
    acc = _dot(mid, ws2_ref[...])
    for k in range(TOP_K):
        for r in range(tm):
            _row_copy(y_hbm, ybuf.at[k], sem, 0, r).wait()
    for k in range(TOP_K):
        acc = acc + gates_ref[:, k:k + 1] * ybuf[k]
    o_ref[...] = x_ref[...] + gate_ref[...] * acc


def _combine(dest, gates, y, h, x, mod, ws1, ws3, ws2, *, n_prompt, dec_seq):
    N, D = x.shape
    tm = COMBINE_TM
    FF = ws1.shape[1]
    assert N % tm == 0 and n_prompt % tm == 0 and dec_seq % tm == 0
    row = functools.partial(_mod_row, tm=tm, n_prompt=n_prompt, dec_seq=dec_seq)
    dest3 = dest.reshape(TOP_K, N // tm, tm).transpose(1, 0, 2)
    return pl.pallas_call(
        _combine_kernel,
        grid=(N // tm,),
        in_specs=[
            pl.BlockSpec((None, TOP_K, tm), lambda i: (i, 0, 0), memory_space=pltpu.SMEM),
            pl.BlockSpec((tm, TOP_K), lambda i: (i, 0)),
            pl.BlockSpec(memory_space=pl.ANY),
            pl.BlockSpec((tm, D), lambda i: (i, 0)),
            pl.BlockSpec((tm, D), lambda i: (i, 0)),
            pl.BlockSpec((None, 1, D), lambda i: (row(i), 0, 5)),
            pl.BlockSpec((D, FF), lambda i: (0, 0)),
            pl.BlockSpec((D, FF), lambda i: (0, 0)),
            pl.BlockSpec((FF, D), lambda i: (0, 0)),
        ],
        out_specs=pl.BlockSpec((tm, D), lambda i: (i, 0)),
        out_shape=jax.ShapeDtypeStruct((N, D), F32),
        scratch_shapes=[pltpu.VMEM((TOP_K, tm, D), F32), pltpu.SemaphoreType.DMA],
        compiler_params=_cparams(("arbitrary",)),
        name="moe_combine",
    )(dest3, gates, y, h, x, mod, ws1, ws3, ws2)


def _moe_layer(x, gain, mod, router_w, router_b, w1, w3, w2, ws1, ws3, ws2, *, n_prompt, dec_seq):
    N = x.shape[0]
    E, BLK = N_EXPERTS, MOE_BLOCK
    h, top_e, gates, rank, counts = _router(x, gain, mod, router_w.T, router_b,
                                            n_prompt=n_prompt, dec_seq=dec_seq,
                                            tm=_row_tile(512, n_prompt, dec_seq))
    counts = counts[:, 0]
    padded = (counts + BLK - 1) // BLK * BLK
    pad_end = jnp.cumsum(padded)
    pad_start = pad_end - padded
    n_blocks = N * TOP_K // BLK + E
    block_expert = jnp.minimum(
        jnp.sum(pad_end[None, :] <= (jnp.arange(n_blocks, dtype=I32) * BLK)[:, None], axis=1), E - 1).astype(I32)
    n_used = (pad_end[-1:] // BLK).astype(I32)
    dest = pad_start[top_e] + rank
    tok = jnp.broadcast_to(jnp.arange(N, dtype=I32)[None, :], (TOP_K, N))
    slot_tok = jnp.zeros((n_blocks * BLK,), I32).at[dest.reshape(-1)].set(tok.reshape(-1))
    y = _experts(block_expert, n_used, slot_tok, h, w1, w3, w2)
    return _combine(dest, gates.T, y, h, x, mod, ws1, ws3, ws2, n_prompt=n_prompt, dec_seq=dec_seq)


def _rope_tables(S):
    quarter = HEAD_DIM // 4
    pos = jnp.arange(S)
    row_id = (pos // GRID_W).astype(F32)
    col_id = (pos % GRID_W).astype(F32)
    inv = ROPE_THETA ** (-jnp.arange(quarter, dtype=F32) / quarter)
    ar, ac = row_id[:, None] * inv, col_id[:, None] * inv
    cos = jnp.concatenate([jnp.cos(ar), jnp.cos(ar), jnp.cos(ac), jnp.cos(ac)], axis=-1)
    sin = jnp.concatenate([-jnp.sin(ar), jnp.sin(ar), -jnp.sin(ac), jnp.sin(ac)], axis=-1)
    return cos, sin


def kernel(x_prompt, x_sample, state_mlstm_C, state_mlstm_n, state_mlstm_m, cache_attn_k, cache_attn_v,
           c, c_ctx, ada_w, ada_b, norm_mix, norm_ffn,
           mlstm_w_in, mlstm_gate_b, mlstm_head_g, mlstm_w_out,
           attn_w_qkv, attn_q_g, attn_k_g, attn_sink, attn_w_o,
           moe_router_w, moe_router_b, moe_w1, moe_w3, moe_w2, shared_w1, shared_w3, shared_w2):
    D = D_MODEL
    Bp, Sp, _ = x_prompt.shape
    Bs, Ss, _ = x_sample.shape
    n_prompt = Bp * Sp
    N = n_prompt + Bs * Ss
    dims = dict(n_prompt=n_prompt, dec_seq=Ss)
    tm512 = _row_tile(512, n_prompt, Ss)

    x = jnp.concatenate([x_prompt.reshape(n_prompt, D), x_sample.reshape(Bs * Ss, D)], axis=0)
    rows = 16
    cvec = jnp.zeros((rows, D), F32).at[0].set(c_ctx).at[1:1 + Bs].set(c)
    mod_all = _ada_table(cvec, ada_w, ada_b)[:, :1 + Bs].reshape(DEPTH, 1 + Bs, 1, 6 * D)

    H = MLSTM_HEADS
    mod = mod_all[0]
    w_in = mlstm_w_in[0]
    w_main = w_in[:, :MLSTM_MAIN].astype(BF16)
    w_gate = jnp.pad(w_in[:, MLSTM_MAIN:], ((0, 0), (0, GATE_PAD - 4 * H))).astype(BF16)
    gate_b = jnp.pad(mlstm_gate_b[0], (0, GATE_PAD - 4 * H))
    col_scale = jnp.concatenate([jnp.full((MLSTM_QK,), MLSTM_DK ** -0.5, F32),
                                 jnp.ones((MLSTM_MAIN - MLSTM_QK,), F32)])
    qkvo, gates = _nm_matmul(x, norm_mix[0], mod, w_main, col_scale, jnp.zeros((MLSTM_MAIN,), F32),
                             shift_idx=0, scale_idx=1, out_dtype=BF16, tm=tm512, tn=1024,
                             aux_w=w_gate, aux_b=gate_b, name="mlstm_proj", **dims)
    L = 256
    hf_p, hb_p, C_p, n_p, m_p = _mlstm_scan(qkvo, gates, 0, Bp, Sp, L, emit_state=True, name="mlstm_prompt")
    state = (state_mlstm_C[:, 0], state_mlstm_n[:, 0], state_mlstm_m[:, 0])
    hf_s, hb_s = _mlstm_scan(qkvo, gates, n_prompt, Bs, Ss, L, state=state, name="mlstm_sample")
    hf = jnp.concatenate([hf_p, hf_s], axis=0)
    hb = jnp.concatenate([hb_p, hb_s], axis=0)
    x = _mm_residual((hf, hb, qkvo, mlstm_head_g[0]), mlstm_w_out[0].astype(BF16), x, mod, gate_idx=2,
                     tm=tm512, tn=1024, mlstm_prologue=True, name="mlstm_out", **dims)
    x = _moe_layer(x, norm_ffn[0], mod, moe_router_w[0], moe_router_b[0], moe_w1[0], moe_w3[0], moe_w2[0],
                   shared_w1[0].astype(BF16), shared_w3[0].astype(BF16), shared_w2[0].astype(BF16), **dims)

    mod = mod_all[1]
    qkv = _nm_matmul(x, norm_mix[1], mod, attn_w_qkv[0].astype(BF16), jnp.ones((ATTN_PROJ,), F32),
                     jnp.zeros((ATTN_PROJ,), F32), shift_idx=0, scale_idx=1, out_dtype=F32, tm=tm512, tn=1024,
                     name="attn_qkv", **dims)
    ident_cos = jnp.ones((Sp, HEAD_DIM), F32)
    ident_sin = jnp.zeros((Sp, HEAD_DIM), F32)
    q_p, k_p, v_p, kf_p, vf_p = _qk_prep(qkv, attn_q_g[0], attn_k_g[0], ident_cos, ident_sin, 0, n_prompt, Sp,
                                         True, "qk_prep_prompt")
    cos, sin = _rope_tables(Ss)
    q_s, k_s, v_s = _qk_prep(qkv, attn_q_g[0], attn_k_g[0], cos, sin, n_prompt, Bs * Ss, 256, False,
                             "qk_prep_latent")
    o_p = _attn_context(attn_sink[0], q_p, k_p, v_p, Bp, Sp)
    P = cache_attn_k.shape[2]
    ctx_k = cache_attn_k[:, 0].reshape(Bs, P, ATTN_KV).astype(BF16)
    ctx_v = cache_attn_v[:, 0].reshape(Bs, P, ATTN_KV).astype(BF16)
    o_s = _attn_latent(attn_sink[0], q_s, k_s, v_s, ctx_k, ctx_v, Bs, Ss)
    o = jnp.concatenate([o_p, o_s], axis=0)
    x = _mm_residual((o,), attn_w_o[0].astype(BF16), x, mod, gate_idx=2,
                     tm=_row_tile(1024, n_prompt, Ss), tn=1024,
                     mlstm_prologue=False, name="attn_out", **dims)
    x = _moe_layer(x, norm_ffn[1], mod, moe_router_w[1], moe_router_b[1], moe_w1[1], moe_w3[1], moe_w2[1],
                   shared_w1[1].astype(BF16), shared_w3[1].astype(BF16), shared_w2[1].astype(BF16), **dims)

    y_prompt = x[:n_prompt].reshape(Bp, Sp, D)
    y_sample = x[n_prompt:].reshape(Bs, Ss, D)
    return (y_prompt, y_sample, C_p[:, None], n_p[:, None], m_p[:, None],
            kf_p.reshape(Bp, 1, Sp, ATTN_KV_HEADS, HEAD_DIM), vf_p.reshape(Bp, 1, Sp, ATTN_KV_HEADS, HEAD_DIM))
```

```python
import functools

import jax
import jax.numpy as jnp
from jax import lax
from jax.experimental import pallas as pl
from jax.experimental.pallas import tpu as pltpu

F32 = jnp.float32
BF16 = jnp.bfloat16
I32 = jnp.int32

D_MODEL = 2048
DEPTH = 2
EPS = 1e-6
GRID_W = 64
MLSTM_HEADS = 8
MLSTM_DK = 128
MLSTM_DV = 256
MLSTM_QK = MLSTM_HEADS * MLSTM_DK
MLSTM_V = MLSTM_HEADS * MLSTM_DV
MLSTM_MAIN = 2 * MLSTM_QK + 2 * MLSTM_V
GATE_PAD = 128
HEAD_DIM = 128
ATTN_Q_HEADS = 16
ATTN_KV_HEADS = 4
ATTN_GROUPS = 4
WINDOW = 128
QBLK = 128
ROPE_THETA = 10000.0
ATTN_Q = ATTN_Q_HEADS * HEAD_DIM
ATTN_KV = ATTN_KV_HEADS * HEAD_DIM
ATTN_PROJ = ATTN_Q + 2 * ATTN_KV
N_EXPERTS = 64
TOP_K = 8
N_EXPERT_GROUPS = 8
TOPK_GROUPS = 4
GROUP_SIZE = N_EXPERTS // N_EXPERT_GROUPS
EXPERT_FF = 512
ROUTED_SCALE = 2.5
MOE_BLOCK = 512
COMBINE_TM = 128

V7X_VMEM_LIMIT = 56 * 1024 * 1024
NEG_INF = float("-inf")


def _cparams(sem):
    return pltpu.CompilerParams(dimension_semantics=("arbitrary",) * len(sem), vmem_limit_bytes=V7X_VMEM_LIMIT)


def _split_hi_lo(a):
    hi = a.astype(BF16)
    lo = (a - hi.astype(F32)).astype(BF16)
    return hi, lo


def _dot(a, b):
    return jnp.dot(a, b, preferred_element_type=F32)


def _dot_nt(a, b):
    return lax.dot_general(a, b, (((1,), (1,)), ((), ())), preferred_element_type=F32)


def _dot_tn(a, b):
    return lax.dot_general(a, b, (((0,), (0,)), ((), ())), preferred_element_type=F32)


def _silu(x):
    return x * jax.nn.sigmoid(x)


def _row_tile(preferred, n_prompt, dec_seq):
    tm = preferred
    while n_prompt % tm or dec_seq % tm:
        tm //= 2
    return tm


def _mod_row(i, tm, n_prompt, dec_seq):
    r0 = i * tm
    return jnp.where(r0 < n_prompt, 0, 1 + (r0 - n_prompt) // dec_seq)


def _norm_modulate(x, gain, mod_ref, shift_idx, scale_idx):
    D = D_MODEL
    y = x * lax.rsqrt(jnp.mean(x * x, axis=-1, keepdims=True) + EPS) * gain
    shift = mod_ref[:, shift_idx * D:(shift_idx + 1) * D]
    scale = mod_ref[:, scale_idx * D:(scale_idx + 1) * D]
    return y * (1.0 + scale) + shift


def _ada_kernel(c_ref, w_ref, b_ref, o_ref):
    s = _silu(c_ref[...])
    s_hi, s_lo = _split_hi_lo(s)
    w_hi, w_lo = _split_hi_lo(w_ref[...])
    o_ref[...] = _dot(s_hi, w_hi) + _dot(s_hi, w_lo) + _dot(s_lo, w_hi) + b_ref[...]


def _ada_table(cvec, ada_w, ada_b):
    D = D_MODEL
    tn = 1024
    rows = cvec.shape[0]
    return pl.pallas_call(
        _ada_kernel,
        grid=(DEPTH, 6 * D // tn),
        in_specs=[
            pl.BlockSpec((rows, D), lambda l, j: (0, 0)),
            pl.BlockSpec((None, D, tn), lambda l, j: (l, 0, j)),
            pl.BlockSpec((None, 1, tn), lambda l, j: (l, 0, j)),
        ],
        out_specs=pl.BlockSpec((None, rows, tn), lambda l, j: (l, 0, j)),
        out_shape=jax.ShapeDtypeStruct((DEPTH, rows, 6 * D), F32),
        compiler_params=_cparams(("parallel", "parallel")),
        name="ada_table",
    )(cvec, ada_w, ada_b.reshape(DEPTH, 1, 6 * D))


def _nm_matmul_kernel(x_ref, gain_ref, mod_ref, w_ref, cs_ref, cb_ref, *rest, shift_idx, scale_idx, has_aux):
    if has_aux:
        wa_ref, ab_ref, o_ref, aux_ref, h_scr = rest
    else:
        o_ref, h_scr = rest

    @pl.when(pl.program_id(1) == 0)
    def _():
        h = _norm_modulate(x_ref[...], gain_ref[...], mod_ref, shift_idx, scale_idx).astype(BF16)
        h_scr[...] = h
        if has_aux:
            aux_ref[...] = _dot(h, wa_ref[...]) + ab_ref[...]

    acc = _dot(h_scr[...], w_ref[...])
    o_ref[...] = (acc * cs_ref[...] + cb_ref[...]).astype(o_ref.dtype)


def _nm_matmul(x, gain, mod, w, col_scale, col_bias, *, shift_idx, scale_idx, n_prompt, dec_seq,
               out_dtype, tm, tn, aux_w=None, aux_b=None, name):
    N, D = x.shape
    P = w.shape[1]
    assert N % tm == 0 and P % tn == 0 and n_prompt % tm == 0 and dec_seq % tm == 0
    has_aux = aux_w is not None
    row = functools.partial(_mod_row, tm=tm, n_prompt=n_prompt, dec_seq=dec_seq)
    in_specs = [
        pl.BlockSpec((tm, D), lambda i, j: (i, 0)),
        pl.BlockSpec((1, D), lambda i, j: (0, 0)),
        pl.BlockSpec((None, 1, 6 * D), lambda i, j: (row(i), 0, 0)),
        pl.BlockSpec((D, tn), lambda i, j: (0, j)),
        pl.BlockSpec((1, tn), lambda i, j: (0, j)),
        pl.BlockSpec((1, tn), lambda i, j: (0, j)),
    ]
    args = [x, gain.reshape(1, D), mod, w, col_scale.reshape(1, P), col_bias.reshape(1, P)]
    out_specs = pl.BlockSpec((tm, tn), lambda i, j: (i, j))
    out_shape = jax.ShapeDtypeStruct((N, P), out_dtype)
    if has_aux:
        PA = aux_w.shape[1]
        in_specs += [pl.BlockSpec((D, PA), lambda i, j: (0, 0)), pl.BlockSpec((1, PA), lambda i, j: (0, 0))]
        args += [aux_w, aux_b.reshape(1, PA)]
        out_specs = [out_specs, pl.BlockSpec((tm, PA), lambda i, j: (i, 0))]
        out_shape = [out_shape, jax.ShapeDtypeStruct((N, PA), F32)]
    return pl.pallas_call(
        functools.partial(_nm_matmul_kernel, shift_idx=shift_idx, scale_idx=scale_idx, has_aux=has_aux),
        grid=(N // tm, P // tn),
        in_specs=in_specs,
        out_specs=out_specs,
        out_shape=out_shape,
        scratch_shapes=[pltpu.VMEM((tm, D), BF16)],
        compiler_params=_cparams(("parallel", "arbitrary")),
        name=name,
    )(*args)


def _mm_residual_kernel(*refs, mlstm_prologue, prompt_tiles):
    if mlstm_prologue:
        hf_ref, hb_ref, og_ref, hg_ref, w_ref, x_ref, gate_ref, o_ref, l_scr = refs

        @pl.when(pl.program_id(1) == 0)
        def _():
            hs = hf_ref[...] + hb_ref[...]
            og = jax.nn.sigmoid(og_ref[...].astype(F32))
            for h in range(MLSTM_HEADS):
                sl = slice(h * MLSTM_DV, (h + 1) * MLSTM_DV)
                hh = hs[:, sl]
                hn = hh * lax.rsqrt(jnp.mean(hh * hh, axis=-1, keepdims=True) + EPS) * hg_ref[:, sl]
                l_scr[:, sl] = (og[:, sl] * hn).astype(BF16)

        o_ref[...] = x_ref[...] + gate_ref[...] * _dot(l_scr[...], w_ref[...])
    else:
        lp_ref, ls_ref, w_ref, x_ref, gate_ref, o_ref = refs
        context = pl.program_id(0) < prompt_tiles

        @pl.when(context)
        def _():
            o_ref[...] = x_ref[...] + gate_ref[...] * _dot(lp_ref[...], w_ref[...])

        @pl.when(jnp.logical_not(context))
        def _():
            o_ref[...] = x_ref[...] + gate_ref[...] * _dot(ls_ref[...], w_ref[...])


def _mm_residual(lhs_args, w, x, mod, *, gate_idx, n_prompt, dec_seq, tm, tn, mlstm_prologue, name):
    N, D = x.shape
    K = w.shape[0]
    assert N % tm == 0 and D % tn == 0 and n_prompt % tm == 0 and dec_seq % tm == 0
    row = functools.partial(_mod_row, tm=tm, n_prompt=n_prompt, dec_seq=dec_seq)
    pt = n_prompt // tm
    if mlstm_prologue:
        hf, hb, qkvo, head_g = lhs_args
        o_blk = (2 * MLSTM_QK + MLSTM_V) // MLSTM_V
        in_specs = [
            pl.BlockSpec((tm, K), lambda i, j: (i, 0)),
            pl.BlockSpec((tm, K), lambda i, j: (i, 0)),
            pl.BlockSpec((tm, MLSTM_V), lambda i, j: (i, o_blk)),
            pl.BlockSpec((1, K), lambda i, j: (0, 0)),
        ]
        args = [hf, hb, qkvo, head_g.reshape(1, K)]
        scratch = [pltpu.VMEM((tm, K), BF16)]
    else:
        lhs_p, lhs_s = lhs_args
        in_specs = [pl.BlockSpec((tm, K), lambda i, j: (jnp.minimum(i, pt - 1), 0)),
                    pl.BlockSpec((tm, K), lambda i, j: (jnp.maximum(i - pt, 0), 0))]
        args = [lhs_p, lhs_s]
        scratch = []
    in_specs += [
        pl.BlockSpec((K, tn), lambda i, j: (0, j)),
        pl.BlockSpec((tm, tn), lambda i, j: (i, j)),
        pl.BlockSpec((None, 1, tn), lambda i, j: (row(i), 0, gate_idx * (D // tn) + j)),
    ]
    args += [w, x, mod]
    return pl.pallas_call(
        functools.partial(_mm_residual_kernel, mlstm_prologue=mlstm_prologue, prompt_tiles=pt),
        grid=(N // tm, D // tn),
        in_specs=in_specs,
        out_specs=pl.BlockSpec((tm, tn), lambda i, j: (i, j)),
        out_shape=jax.ShapeDtypeStruct((N, D), F32),
        scratch_shapes=scratch,
        compiler_params=_cparams(("parallel", "arbitrary")),
        name=name,
    )(*args)


def _log_sigmoid(x):
    return jnp.minimum(x, 0.0) - jnp.log(1.0 + jnp.exp(-jnp.abs(x)))


def _mlstm_direction(d, q_ref, k_ref, v_ref, g_ref, h_ref, C_scr, n_scr, m_scr, L):
    H, DK, DV = MLSTM_HEADS, MLSTM_DK, MLSTM_DV
    g = g_ref[...]
    lf = _log_sigmoid(g)
    r = lax.broadcasted_iota(I32, (L, L), 0)
    c = lax.broadcasted_iota(I32, (L, L), 1)
    causal = (c <= r) if d == 0 else (c >= r)
    tri = jnp.where(causal, 1.0, 0.0).astype(BF16)
    lf1 = lf.astype(BF16)
    rem = lf - lf1.astype(F32)
    lf2 = rem.astype(BF16)
    lf3 = (rem - lf2.astype(F32)).astype(BF16)
    bsum = _dot(tri, lf1) + _dot(tri, lf2) + _dot(tri, lf3)
    g_t = g.T
    b_t = bsum.T
    end = L - 1 if d == 0 else 0
    for h in range(H):
        ci, cf = d * 2 * H + h, d * 2 * H + H + h
        b_col, i_col = bsum[:, cf:cf + 1], g[:, ci:ci + 1]
        b_row, i_row = b_t[cf:cf + 1, :], g_t[ci:ci + 1, :]
        b_end = bsum[end:end + 1, cf:cf + 1]
        m = m_scr[d, h]
        C = C_scr[d, h]
        n = n_scr[d, h]
        qh = q_ref[:, h * DK:(h + 1) * DK]
        kh = k_ref[:, h * DK:(h + 1) * DK]
        vh = v_ref[:, h * DV:(h + 1) * DV]
        dmat = jnp.where(causal, b_col - b_row + i_row, NEG_INF)
        inter = b_col + m
        m_out = jnp.maximum(inter, jnp.max(dmat, axis=-1, keepdims=True))
        w = jnp.exp(dmat - m_out) * _dot_nt(qh, kh)
        dec = jnp.exp(inter - m_out)
        num = _dot(w.astype(BF16), vh) + dec * _dot(qh, C.astype(BF16))
        den = jnp.sum(w, axis=-1, keepdims=True) + dec * jnp.sum(qh.astype(F32) * n, axis=-1, keepdims=True)
        h_ref[:, h * DV:(h + 1) * DV] = num / jnp.maximum(jnp.abs(den), jnp.exp(-m_out))
        to_end = b_end - b_col + i_col
        m_new = jnp.maximum(b_end + m, jnp.max(to_end, axis=0, keepdims=True))
        wk = jnp.exp(to_end - m_new)
        dec_end = jnp.exp(b_end + m - m_new)
        kw = kh.astype(F32) * wk
        C_scr[d, h] = dec_end * C + _dot_tn(kw.astype(BF16), vh)
        n_scr[d, h] = dec_end * n + jnp.sum(kw, axis=0, keepdims=True)
        m_scr[d, h] = m_new


def _mlstm_kernel(fwd_ref, bwd_ref, bidx_ref, flag_ref, qf, kf, vf, gf, qb, kb, vb, gb, C0, n0, m0,
                  hf, hb, Co, no, mo, C_scr, n_scr, m_scr, *, L):
    H = MLSTM_HEADS
    flags = flag_ref[pl.program_id(0)]
    first, last, context = (flags & 1) != 0, (flags & 2) != 0, (flags & 4) != 0

    @pl.when(first & context)
    def _():
        C_scr[...] = jnp.zeros_like(C_scr)
        n_scr[...] = jnp.zeros_like(n_scr)
        m_scr[...] = jnp.zeros_like(m_scr)

    @pl.when(first & jnp.logical_not(context))
    def _():
        C_scr[...] = C0[...]
        for d in range(2):
            for h in range(H):
                n_scr[d, h] = n0[d, h:h + 1, :]
                m_scr[d, h] = m0[d:d + 1, h:h + 1]

    _mlstm_direction(0, qf, kf, vf, gf, hf, C_scr, n_scr, m_scr, L)
    _mlstm_direction(1, qb, kb, vb, gb, hb, C_scr, n_scr, m_scr, L)

    @pl.when(last & context)
    def _():
        Co[...] = C_scr[...]
        for d in range(2):
            for h in range(H):
                no[d, h:h + 1, :] = n_scr[d, h]
                mo[d:d + 1, h:h + 1] = m_scr[d, h]


def _mlstm_scan(qkvo, gates, Bp, Sp, Bs, Ss, L, state):
    H, DK, DV = MLSTM_HEADS, MLSTM_DK, MLSTM_DV
    assert Sp % L == 0 and Ss % L == 0
    fwd, bwd, bidx, flags = [], [], [], []
    for context, B, S, base in ((1, Bp, Sp, 0), (0, Bs, Ss, Bp * Sp // L)):
        nc = S // L
        for b in range(B):
            for c in range(nc):
                fwd.append(base + b * nc + c)
                bwd.append(base + b * nc + nc - 1 - c)
                bidx.append(b)
                flags.append((c == 0) * 1 + (c == nc - 1) * 2 + context * 4)
    tables = [jnp.asarray(t, I32) for t in (fwd, bwd, bidx, flags)]
    lat_b = lambda s, fl, bi: jnp.where((fl[s] & 4) != 0, 0, bi[s])
    ctx_b = lambda s, fl, bi: jnp.where((fl[s] & 4) != 0, bi[s], Bp - 1)

    def specs(tbl):
        return [
            pl.BlockSpec((L, MLSTM_QK), lambda s, f, w, bi, fl: ((f, w)[tbl][s], 0)),
            pl.BlockSpec((L, MLSTM_QK), lambda s, f, w, bi, fl: ((f, w)[tbl][s], 1)),
            pl.BlockSpec((L, MLSTM_V), lambda s, f, w, bi, fl: ((f, w)[tbl][s], 1)),
            pl.BlockSpec((L, GATE_PAD), lambda s, f, w, bi, fl: ((f, w)[tbl][s], 0)),
        ]

    def state_specs(which):
        return [
            pl.BlockSpec((None, 2, H, DK, DV), lambda s, f, w, bi, fl: (which(s, fl, bi), 0, 0, 0, 0)),
            pl.BlockSpec((None, 2, H, DK), lambda s, f, w, bi, fl: (which(s, fl, bi), 0, 0, 0)),
            pl.BlockSpec((None, 2, H), lambda s, f, w, bi, fl: (which(s, fl, bi), 0, 0)),
        ]

    N = qkvo.shape[0]
    return pl.pallas_call(
        functools.partial(_mlstm_kernel, L=L),
        grid_spec=pltpu.PrefetchScalarGridSpec(
            num_scalar_prefetch=4,
            grid=(len(fwd),),
            in_specs=specs(0) + specs(1) + state_specs(lat_b),
            out_specs=[
                pl.BlockSpec((L, MLSTM_V), lambda s, f, w, bi, fl: (f[s], 0)),
                pl.BlockSpec((L, MLSTM_V), lambda s, f, w, bi, fl: (w[s], 0)),
            ] + state_specs(ctx_b),
            scratch_shapes=[
                pltpu.VMEM((2, H, DK, DV), F32),
                pltpu.VMEM((2, H, 1, DK), F32),
                pltpu.VMEM((2, H, 1, 1), F32),
            ],
        ),
        out_shape=[
            jax.ShapeDtypeStruct((N, MLSTM_V), F32),
            jax.ShapeDtypeStruct((N, MLSTM_V), F32),
            jax.ShapeDtypeStruct((Bp, 2, H, DK, DV), F32),
            jax.ShapeDtypeStruct((Bp, 2, H, DK), F32),
            jax.ShapeDtypeStruct((Bp, 2, H), F32),
        ],
        compiler_params=_cparams(("arbitrary",)),
        name="mlstm_scan",
    )(*tables, qkvo, qkvo, qkvo, gates, qkvo, qkvo, qkvo, gates, *state)


def _qk_prep_kernel(qkv_ref, qg_ref, kg_ref, cos_ref, sin_ref, *outs, emit_f32):
    if emit_f32:
        q_ref, k_ref, v_ref, kf_ref, vf_ref = outs
    else:
        q_ref, k_ref, v_ref = outs
    cos = cos_ref[...]
    sin = sin_ref[...]
    lane = lax.broadcasted_iota(I32, cos.shape, 1)
    first = (lane % (HEAD_DIM // 2)) < (HEAD_DIM // 4)

    def norm_rope(x, gain):
        xn = x * lax.rsqrt(jnp.mean(x * x, axis=-1, keepdims=True) + EPS) * gain
        partner = jnp.where(first, pltpu.roll(xn, HEAD_DIM - HEAD_DIM // 4, 1), pltpu.roll(xn, HEAD_DIM // 4, 1))
        return xn * cos + partner * sin

    for h in range(ATTN_Q_HEADS):
        sl = slice(h * HEAD_DIM, (h + 1) * HEAD_DIM)
        q_ref[:, sl] = norm_rope(qkv_ref[:, sl], qg_ref[...]).astype(BF16)
    for h in range(ATTN_KV_HEADS):
        sl = slice(h * HEAD_DIM, (h + 1) * HEAD_DIM)
        kx = norm_rope(qkv_ref[:, ATTN_Q + h * HEAD_DIM:ATTN_Q + (h + 1) * HEAD_DIM], kg_ref[...])
        k_ref[:, sl] = kx.astype(BF16)
        if emit_f32:
            kf_ref[:, sl] = kx
    vx = qkv_ref[:, ATTN_Q + ATTN_KV:]
    v_ref[...] = vx.astype(BF16)
    if emit_f32:
        vf_ref[...] = vx


def _qk_prep(qkv, q_g, k_g, cos, sin, row0, rows, tm, emit_f32, name):
    assert rows % tm == 0 and row0 % tm == 0 and cos.shape[0] % tm == 0
    base = row0 // tm
    nt = cos.shape[0] // tm
    out_specs = [
        pl.BlockSpec((tm, ATTN_Q), lambda i: (i, 0)),
        pl.BlockSpec((tm, ATTN_KV), lambda i: (i, 0)),
        pl.BlockSpec((tm, ATTN_KV), lambda i: (i, 0)),
    ]
    out_shape = [
        jax.ShapeDtypeStruct((rows, ATTN_Q), BF16),
        jax.ShapeDtypeStruct((rows, ATTN_KV), BF16),
        jax.ShapeDtypeStruct((rows, ATTN_KV), BF16),
    ]
    if emit_f32:
        out_specs += [pl.BlockSpec((tm, ATTN_KV), lambda i: (i, 0))] * 2
        out_shape += [jax.ShapeDtypeStruct((rows, ATTN_KV), F32)] * 2
    return pl.pallas_call(
        functools.partial(_qk_prep_kernel, emit_f32=emit_f32),
        grid=(rows // tm,),
        in_specs=[
            pl.BlockSpec((tm, ATTN_PROJ), lambda i: (base + i, 0)),
            pl.BlockSpec((1, HEAD_DIM), lambda i: (0, 0)),
            pl.BlockSpec((1, HEAD_DIM), lambda i: (0, 0)),
            pl.BlockSpec((tm, HEAD_DIM), lambda i: (i % nt, 0)),
            pl.BlockSpec((tm, HEAD_DIM), lambda i: (i % nt, 0)),
        ],
        out_specs=out_specs,
        out_shape=out_shape,
        compiler_params=_cparams(("parallel",)),
        name=name,
    )(qkv, q_g.reshape(1, HEAD_DIM), k_g.reshape(1, HEAD_DIM), cos, sin)


def _sink_column(sink_ref, kv, rows_per_head):
    parts = [jnp.full((rows_per_head, 1), sink_ref[kv * ATTN_GROUPS + g], F32) for g in range(ATTN_GROUPS)]
    return jnp.concatenate(parts, axis=0)


def _attn_ctx_kernel(sink_ref, q_ref, k_ref, v_ref, o_ref):
    S = q_ref.shape[0]
    scale = HEAD_DIM ** -0.5
    for kv in range(ATTN_KV_HEADS):
        q = jnp.concatenate(
            [q_ref[:, (kv * ATTN_GROUPS + g) * HEAD_DIM:(kv * ATTN_GROUPS + g + 1) * HEAD_DIM]
             for g in range(ATTN_GROUPS)], axis=0)
        ksl = slice(kv * HEAD_DIM, (kv + 1) * HEAD_DIM)
        s = _dot_nt(q, k_ref[:, ksl]) * scale
        sk = _sink_column(sink_ref, kv, S)
        m = jnp.maximum(jnp.max(s, axis=-1, keepdims=True), sk)
        p = jnp.exp(s - m)
        den = jnp.sum(p, axis=-1, keepdims=True) + jnp.exp(sk - m)
        o = _dot((p / den).astype(BF16), v_ref[:, ksl])
        for g in range(ATTN_GROUPS):
            hq = kv * ATTN_GROUPS + g
            o_ref[:, hq * HEAD_DIM:(hq + 1) * HEAD_DIM] = o[g * S:(g + 1) * S].astype(BF16)


def _attn_context(sink, q, k, v, B, S):
    return pl.pallas_call(
        _attn_ctx_kernel,
        grid_spec=pltpu.PrefetchScalarGridSpec(
            num_scalar_prefetch=0,
            grid=(B,),
            in_specs=[
                pl.BlockSpec(memory_space=pltpu.SMEM),
                pl.BlockSpec((S, ATTN_Q), lambda b: (b, 0)),
                pl.BlockSpec((S, ATTN_KV), lambda b: (b, 0)),
                pl.BlockSpec((S, ATTN_KV), lambda b: (b, 0)),
            ],
            out_specs=pl.BlockSpec((S, ATTN_Q), lambda b: (b, 0)),
        ),
        out_shape=jax.ShapeDtypeStruct((B * S, ATTN_Q), BF16),
        compiler_params=_cparams(("parallel",)),
        name="attn_context",
    )(sink, q, k, v)


def _attn_lat_kernel(sink_ref, q_ref, kp_ref, kc_ref, kn_ref, vp_ref, vc_ref, vn_ref, kx_ref, vx_ref, o_ref):
    i = pl.program_id(1)
    nb = pl.num_programs(1)
    scale = HEAD_DIM ** -0.5
    R = ATTN_GROUPS * QBLK
    r = lax.broadcasted_iota(I32, (R, QBLK), 0) % QBLK
    c = lax.broadcasted_iota(I32, (R, QBLK), 1)
    ok_prev = c >= r
    ok_next = c <= r
    edge_prev = jnp.where(i > 0, 0.0, NEG_INF)
    edge_next = jnp.where(i < nb - 1, 0.0, NEG_INF)
    for kv in range(ATTN_KV_HEADS):
        q = jnp.concatenate(
            [q_ref[:, (kv * ATTN_GROUPS + g) * HEAD_DIM:(kv * ATTN_GROUPS + g + 1) * HEAD_DIM]
             for g in range(ATTN_GROUPS)], axis=0)
        ksl = slice(kv * HEAD_DIM, (kv + 1) * HEAD_DIM)
        s_p = jnp.where(ok_prev, _dot_nt(q, kp_ref[:, ksl]) * scale + edge_prev, NEG_INF)
        s_c = _dot_nt(q, kc_ref[:, ksl]) * scale
        s_n = jnp.where(ok_next, _dot_nt(q, kn_ref[:, ksl]) * scale + edge_next, NEG_INF)
        s_x = _dot_nt(q, kx_ref[:, ksl]) * scale
        sk = _sink_column(sink_ref, kv, QBLK)
        m = jnp.maximum(
            jnp.maximum(jnp.max(s_p, axis=-1, keepdims=True), jnp.max(s_c, axis=-1, keepdims=True)),
            jnp.maximum(jnp.max(s_n, axis=-1, keepdims=True), jnp.max(s_x, axis=-1, keepdims=True)))
        m = jnp.maximum(m, sk)
        p_p, p_c, p_n, p_x = jnp.exp(s_p - m), jnp.exp(s_c - m), jnp.exp(s_n - m), jnp.exp(s_x - m)
        den = (jnp.sum(p_p, axis=-1, keepdims=True) + jnp.sum(p_c, axis=-1, keepdims=True)
               + jnp.sum(p_n, axis=-1, keepdims=True) + jnp.sum(p_x, axis=-1, keepdims=True) + jnp.exp(sk - m))
        inv = 1.0 / den
        o = (_dot((p_p * inv).astype(BF16), vp_ref[:, ksl]) + _dot((p_c * inv).astype(BF16), vc_ref[:, ksl])
             + _dot((p_n * inv).astype(BF16), vn_ref[:, ksl]) + _dot((p_x * inv).astype(BF16), vx_ref[:, ksl]))
        for g in range(ATTN_GROUPS):
            hq = kv * ATTN_GROUPS + g
            o_ref[:, hq * HEAD_DIM:(hq + 1) * HEAD_DIM] = o[g * QBLK:(g + 1) * QBLK].astype(BF16)


def _attn_latent(sink, q, k, v, ctx_k, ctx_v, B, S):
    nb = S // QBLK
    P = ctx_k.shape[1]
    prev = lambda b, i: (b * nb + jnp.maximum(i - 1, 0), 0)
    cur = lambda b, i: (b * nb + i, 0)
    nxt = lambda b, i: (b * nb + jnp.minimum(i + 1, nb - 1), 0)
    kv_spec = lambda f: pl.BlockSpec((QBLK, ATTN_KV), f)
    return pl.pallas_call(
        _attn_lat_kernel,
        grid_spec=pltpu.PrefetchScalarGridSpec(
            num_scalar_prefetch=0,
            grid=(B, nb),
            in_specs=[
                pl.BlockSpec(memory_space=pltpu.SMEM),
                pl.BlockSpec((QBLK, ATTN_Q), cur),
                kv_spec(prev), kv_spec(cur), kv_spec(nxt),
                kv_spec(prev), kv_spec(cur), kv_spec(nxt),
                pl.BlockSpec((None, P, ATTN_KV), lambda b, i: (b, 0, 0)),
                pl.BlockSpec((None, P, ATTN_KV), lambda b, i: (b, 0, 0)),
            ],
            out_specs=pl.BlockSpec((QBLK, ATTN_Q), cur),
        ),
        out_shape=jax.ShapeDtypeStruct((B * S, ATTN_Q), BF16),
        compiler_params=_cparams(("parallel", "parallel")),
        name="attn_latent",
    )(sink, q, k, k, k, v, v, v, ctx_k, ctx_v)


def _router_kernel(x_ref, gain_ref, mod_ref, wr_ref, rb_ref, h_ref, e_ref, g_ref, r_ref, cnt_ref, cnt_scr, *, tm):
    E, GS = N_EXPERTS, GROUP_SIZE
    i = pl.program_id(0)

    @pl.when(i == 0)
    def _():
        cnt_scr[...] = jnp.zeros_like(cnt_scr)

    h = _norm_modulate(x_ref[...], gain_ref[...], mod_ref, 3, 4)
    h_ref[...] = h
    h_hi, h_lo = _split_hi_lo(h)
    w_hi, w_lo = _split_hi_lo(wr_ref[...])
    logits = _dot_nt(w_hi, h_hi) + _dot_nt(w_lo, h_hi) + _dot_nt(w_hi, h_lo)
    scores = jax.nn.sigmoid(logits)
    biased = scores + rb_ref[...]
    sub = lax.broadcasted_iota(I32, (GS, tm), 0)
    gscore = []
    for gi in range(N_EXPERT_GROUPS):
        xg = biased[gi * GS:(gi + 1) * GS, :]
        m1 = jnp.max(xg, axis=0, keepdims=True)
        first = jnp.min(jnp.where(xg == m1, sub, GS), axis=0, keepdims=True)
        m2 = jnp.max(jnp.where(sub == first, NEG_INF, xg), axis=0, keepdims=True)
        gscore.append(m1 + m2)
    cur = jnp.concatenate(gscore, axis=0)
    gid = lax.broadcasted_iota(I32, (N_EXPERT_GROUPS, tm), 0)
    gsel = jnp.zeros((N_EXPERT_GROUPS, tm), F32)
    for _ in range(TOPK_GROUPS):
        mx = jnp.max(cur, axis=0, keepdims=True)
        first = jnp.min(jnp.where(cur == mx, gid, N_EXPERT_GROUPS), axis=0, keepdims=True)
        hit = gid == first
        gsel = jnp.where(hit, 1.0, gsel)
        cur = jnp.where(hit, NEG_INF, cur)
    ok = jnp.concatenate(
        [jnp.broadcast_to(gsel[gi:gi + 1, :], (GS, tm)) for gi in range(N_EXPERT_GROUPS)], axis=0)
    masked = jnp.where(ok > 0.5, biased, NEG_INF)
    eid = lax.broadcasted_iota(I32, (E, tm), 0)
    sel = jnp.zeros((E, tm), F32)
    picks, pick_scores = [], []
    for _ in range(TOP_K):
        mx = jnp.max(masked, axis=0, keepdims=True)
        first = jnp.min(jnp.where(masked == mx, eid, E), axis=0, keepdims=True)
        hit = eid == first
        picks.append(first)
        pick_scores.append(jnp.sum(jnp.where(hit, scores, 0.0), axis=0, keepdims=True))
        sel = jnp.where(hit, 1.0, sel)
        masked = jnp.where(hit, NEG_INF, masked)
    total = pick_scores[0]
    for s in pick_scores[1:]:
        total = total + s
    g_ref[...] = jnp.concatenate(pick_scores, axis=0) / total * ROUTED_SCALE
    e_ref[...] = jnp.concatenate(picks, axis=0)
    rr = lax.broadcasted_iota(I32, (tm, tm), 0)
    cc = lax.broadcasted_iota(I32, (tm, tm), 1)
    before = jnp.where(rr < cc, 1.0, 0.0).astype(BF16)
    rank = _dot(sel.astype(BF16), before) + cnt_scr[...]
    r_ref[...] = jnp.concatenate(
        [jnp.sum(jnp.where(eid == p, rank, 0.0), axis=0, keepdims=True) for p in picks], axis=0).astype(I32)
    cnt_scr[...] = cnt_scr[...] + jnp.sum(sel, axis=-1, keepdims=True)
    cnt_ref[...] = cnt_scr[...].astype(I32)


def _router(x, gain, mod, router_w_t, router_b, *, n_prompt, dec_seq, tm):
    N, D = x.shape
    E = N_EXPERTS
    assert N % tm == 0 and n_prompt % tm == 0 and dec_seq % tm == 0
    row = functools.partial(_mod_row, tm=tm, n_prompt=n_prompt, dec_seq=dec_seq)
    return pl.pallas_call(
        functools.partial(_router_kernel, tm=tm),
        grid=(N // tm,),
        in_specs=[
            pl.BlockSpec((tm, D), lambda i: (i, 0)),
            pl.BlockSpec((1, D), lambda i: (0, 0)),
            pl.BlockSpec((None, 1, 6 * D), lambda i: (row(i), 0, 0)),
            pl.BlockSpec((E, D), lambda i: (0, 0)),
            pl.BlockSpec((E, 1), lambda i: (0, 0)),
        ],
        out_specs=[
            pl.BlockSpec((tm, D), lambda i: (i, 0)),
            pl.BlockSpec((TOP_K, tm), lambda i: (0, i)),
            pl.BlockSpec((TOP_K, tm), lambda i: (0, i)),
            pl.BlockSpec((TOP_K, tm), lambda i: (0, i)),
            pl.BlockSpec((E, 1), lambda i: (0, 0)),
        ],
        out_shape=[
            jax.ShapeDtypeStruct((N, D), F32),
            jax.ShapeDtypeStruct((TOP_K, N), I32),
            jax.ShapeDtypeStruct((TOP_K, N), F32),
            jax.ShapeDtypeStruct((TOP_K, N), I32),
            jax.ShapeDtypeStruct((E, 1), I32),
        ],
        scratch_shapes=[pltpu.VMEM((E, 1), F32)],
        compiler_params=_cparams(("arbitrary",)),
        name="moe_router",
    )(x, gain.reshape(1, D), mod, router_w_t, router_b.reshape(E, 1))


def _row_copy(src, dst, sem, src_row, dst_row):
    return pltpu.make_async_copy(src.at[pl.ds(src_row, 1), :], dst.at[pl.ds(dst_row, 1), :], sem)


def _experts_kernel(be_ref, nused_ref, tok_cur, tok_nxt, h_hbm, w1_ref, w3_ref, w2_ref, y_ref,
                    xbuf, sems, w1c, w3c, w2c, prev_e):
    b = pl.program_id(0)
    e = be_ref[b]
    n_used = nused_ref[0]
    BLK = xbuf.shape[1]
    slot = b % 2

    FF = w1c.shape[1]
    D = w2c.shape[1]
    CW = 256
    n_chunks = 2 * (FF // CW) + D // CW
    per_chunk = BLK // n_chunks + 1

    def gather(tok_ref, s, lo=0, hi=BLK):
        for r in range(lo, min(hi, BLK)):
            _row_copy(h_hbm, xbuf.at[s], sems.at[s], tok_ref[0, r], r).start()

    def drain(s):
        for r in range(BLK):
            _row_copy(h_hbm, xbuf.at[s], sems.at[s], 0, r).wait()

    @pl.when(b == 0)
    def _():
        prev_e[0] = -1
        gather(tok_cur, 0)

    @pl.when(b < n_used)
    def _():
        @pl.when(e != prev_e[0])
        def _():
            w1c[...] = w1_ref[...].astype(BF16)
            w3c[...] = w3_ref[...].astype(BF16)
            w2c[...] = w2_ref[...].astype(BF16)
            prev_e[0] = e

        drain(slot)
        chunk = [0]

        def gather_some():
            gather(tok_nxt, 1 - slot, chunk[0] * per_chunk, (chunk[0] + 1) * per_chunk)
            chunk[0] += 1

        x = xbuf[slot].astype(BF16)
        mids = []
        for c in range(FF // CW):
            sl = slice(c * CW, (c + 1) * CW)
            gather_some()
            a = _dot(x, w1c[:, sl])
            gather_some()
            mids.append((_silu(a) * _dot(x, w3c[:, sl])).astype(BF16))
        mid = jnp.concatenate(mids, axis=1)
        for c in range(D // CW):
            sl = slice(c * CW, (c + 1) * CW)
            gather_some()
            y_ref[:, sl] = _dot(mid, w2c[:, sl])

    @pl.when(b >= n_used)
    def _():
        @pl.when(b == n_used)
        def _():
            drain(slot)

        y_ref[...] = jnp.zeros_like(y_ref)


def _experts(block_expert, n_used, slot_tok, h, w1, w3, w2):
    n_blocks = block_expert.shape[0]
    BLK = MOE_BLOCK
    D = h.shape[1]
    FF = w1.shape[2]
    slot_tok3 = slot_tok.reshape(n_blocks, 1, BLK)
    return pl.pallas_call(
        _experts_kernel,
        grid_spec=pltpu.PrefetchScalarGridSpec(
            num_scalar_prefetch=2,
            grid=(n_blocks,),
            in_specs=[
                pl.BlockSpec((None, 1, BLK), lambda b, be, nu: (b, 0, 0), memory_space=pltpu.SMEM),
                pl.BlockSpec((None, 1, BLK), lambda b, be, nu: (jnp.minimum(b + 1, n_blocks - 1), 0, 0),
                             memory_space=pltpu.SMEM),
                pl.BlockSpec(memory_space=pl.ANY),
                pl.BlockSpec((None, D, FF), lambda b, be, nu: (be[b], 0, 0)),
                pl.BlockSpec((None, D, FF), lambda b, be, nu: (be[b], 0, 0)),
                pl.BlockSpec((None, FF, D), lambda b, be, nu: (be[b], 0, 0)),
            ],
            out_specs=pl.BlockSpec((BLK, D), lambda b, be, nu: (b, 0)),
            scratch_shapes=[
                pltpu.VMEM((2, BLK, D), F32),
                pltpu.SemaphoreType.DMA((2,)),
                pltpu.VMEM((D, FF), BF16),
                pltpu.VMEM((D, FF), BF16),
                pltpu.VMEM((FF, D), BF16),
                pltpu.SMEM((1,), I32),
            ],
        ),
        out_shape=jax.ShapeDtypeStruct((n_blocks * BLK, D), F32),
        compiler_params=_cparams(("arbitrary",)),
        name="moe_experts",
    )(block_expert, n_used, slot_tok3, slot_tok3, h, w1, w3, w2)


def _combine_kernel(dest_cur, dest_nxt, gates_ref, y_hbm, h_ref, x_ref, gate_ref, ws1_ref, ws3_ref, ws2_ref,
                    *rest, prompt_tiles):
    if prompt_tiles is None:
        o_ref, ybuf, sems = rest
    else:
        op_ref, os_ref, ybuf, sems = rest
    tm = h_ref.shape[0]
    i = pl.program_id(0)
    last = pl.num_programs(0) - 1
    slot = i % 2

    def gather(dest_ref, s):
        for k in range(TOP_K):
            for r in range(tm):
                _row_copy(y_hbm, ybuf.at[s, k], sems.at[s], dest_ref[k, r], r).start()

    def drain(s):
        for k in range(TOP_K):
            for r in range(tm):
                _row_copy(y_hbm, ybuf.at[s, k], sems.at[s], 0, r).wait()

    @pl.when(i == 0)
    def _():
        gather(dest_cur, 0)

    drain(slot)
    gather(dest_nxt, 1 - slot)
    hb = h_ref[...].astype(BF16)
    mid = (_silu(_dot(hb, ws1_ref[...])) * _dot(hb, ws3_ref[...])).astype(BF16)
    acc = _dot(mid, ws2_ref[...])
    for k in range(TOP_K):
        acc = acc + gates_ref[:, k:k + 1] * ybuf[slot, k]
    out = x_ref[...] + gate_ref[...] * acc
    if prompt_tiles is None:
        o_ref[...] = out
    else:
        @pl.when(i < prompt_tiles)
        def _():
            op_ref[...] = out

        @pl.when(i >= prompt_tiles)
        def _():
            os_ref[...] = out

    @pl.when(i == last)
    def _():
        drain(1 - slot)


def _combine(dest, gates, y, h, x, mod, ws1, ws3, ws2, *, n_prompt, dec_seq, split_outputs):
    N, D = x.shape
    tm = COMBINE_TM
    FF = ws1.shape[1]
    assert N % tm == 0 and n_prompt % tm == 0 and dec_seq % tm == 0
    nt = N // tm
    row = functools.partial(_mod_row, tm=tm, n_prompt=n_prompt, dec_seq=dec_seq)
    dest3 = dest.reshape(TOP_K, nt, tm).transpose(1, 0, 2)
    if split_outputs:
        pt = n_prompt // tm
        out_specs = [pl.BlockSpec((tm, D), lambda i: (jnp.minimum(i, pt - 1), 0)),
                     pl.BlockSpec((tm, D), lambda i: (jnp.maximum(i - pt, 0), 0))]
        out_shape = [jax.ShapeDtypeStruct((n_prompt, D), F32), jax.ShapeDtypeStruct((N - n_prompt, D), F32)]
    else:
        pt = None
        out_specs = pl.BlockSpec((tm, D), lambda i: (i, 0))
        out_shape = jax.ShapeDtypeStruct((N, D), F32)
    return pl.pallas_call(
        functools.partial(_combine_kernel, prompt_tiles=pt),
        grid=(nt,),
        in_specs=[
            pl.BlockSpec((None, TOP_K, tm), lambda i: (i, 0, 0), memory_space=pltpu.SMEM),
            pl.BlockSpec((None, TOP_K, tm), lambda i: (jnp.minimum(i + 1, nt - 1), 0, 0), memory_space=pltpu.SMEM),
            pl.BlockSpec((tm, TOP_K), lambda i: (i, 0)),
            pl.BlockSpec(memory_space=pl.ANY),
            pl.BlockSpec((tm, D), lambda i: (i, 0)),
            pl.BlockSpec((tm, D), lambda i: (i, 0)),
            pl.BlockSpec((None, 1, D), lambda i: (row(i), 0, 5)),
            pl.BlockSpec((D, FF), lambda i: (0, 0)),
            pl.BlockSpec((D, FF), lambda i: (0, 0)),
            pl.BlockSpec((FF, D), lambda i: (0, 0)),
        ],
        out_specs=out_specs,
        out_shape=out_shape,
        scratch_shapes=[pltpu.VMEM((2, TOP_K, tm, D), F32), pltpu.SemaphoreType.DMA((2,))],
        compiler_params=_cparams(("arbitrary",)),
        name="moe_combine",
    )(dest3, dest3, gates, y, h, x, mod, ws1, ws3, ws2)


def _moe_layer(x, gain, mod, router_w, router_b, w1, w3, w2, ws1, ws3, ws2, *, n_prompt, dec_seq,
               split_outputs=False):
    N = x.shape[0]
    E, BLK = N_EXPERTS, MOE_BLOCK
    h, top_e, gates, rank, counts = _router(x, gain, mod, router_w.T, router_b,
                                            n_prompt=n_prompt, dec_seq=dec_seq,
                                            tm=_row_tile(512, n_prompt, dec_seq))
    counts = counts[:, 0]
    padded = (counts + BLK - 1) // BLK * BLK
    pad_end = jnp.cumsum(padded)
    pad_start = pad_end - padded
    n_blocks = N * TOP_K // BLK + E
    block_expert = jnp.minimum(
        jnp.sum(pad_end[None, :] <= (jnp.arange(n_blocks, dtype=I32) * BLK)[:, None], axis=1), E - 1).astype(I32)
    n_used = (pad_end[-1:] // BLK).astype(I32)
    onehot = top_e[None] == jnp.arange(E, dtype=I32)[:, None, None]
    dest = jnp.sum(jnp.where(onehot, pad_start[:, None, None], 0), axis=0) + rank
    tok = jnp.broadcast_to(jnp.arange(N, dtype=I32)[None, :], (TOP_K, N))
    slot_tok = jnp.zeros((n_blocks * BLK,), I32).at[dest.reshape(-1)].set(
        tok.reshape(-1), unique_indices=True, mode="promise_in_bounds")
    y = _experts(block_expert, n_used, slot_tok, h, w1, w3, w2)
    return _combine(dest, gates.T, y, h, x, mod, ws1, ws3, ws2, n_prompt=n_prompt, dec_seq=dec_seq,
                    split_outputs=split_outputs)


def _rope_tables(S):
    quarter = HEAD_DIM // 4
    pos = jnp.arange(S)
    row_id = (pos // GRID_W).astype(F32)
    col_id = (pos % GRID_W).astype(F32)
    inv = ROPE_THETA ** (-jnp.arange(quarter, dtype=F32) / quarter)
    ar, ac = row_id[:, None] * inv, col_id[:, None] * inv
    cos = jnp.concatenate([jnp.cos(ar), jnp.cos(ar), jnp.cos(ac), jnp.cos(ac)], axis=-1)
    sin = jnp.concatenate([-jnp.sin(ar), jnp.sin(ar), -jnp.sin(ac), jnp.sin(ac)], axis=-1)
    return cos, sin


def kernel(x_prompt, x_sample, state_mlstm_C, state_mlstm_n, state_mlstm_m, cache_attn_k, cache_attn_v,
           c, c_ctx, ada_w, ada_b, norm_mix, norm_ffn,
           mlstm_w_in, mlstm_gate_b, mlstm_head_g, mlstm_w_out,
           attn_w_qkv, attn_q_g, attn_k_g, attn_sink, attn_w_o,
           moe_router_w, moe_router_b, moe_w1, moe_w3, moe_w2, shared_w1, shared_w3, shared_w2):
    D = D_MODEL
    Bp, Sp, _ = x_prompt.shape
    Bs, Ss, _ = x_sample.shape
    n_prompt = Bp * Sp
    N = n_prompt + Bs * Ss
    dims = dict(n_prompt=n_prompt, dec_seq=Ss)
    tm512 = _row_tile(512, n_prompt, Ss)

    x = jnp.concatenate([x_prompt.reshape(n_prompt, D), x_sample.reshape(Bs * Ss, D)], axis=0)
    rows = 16
    cvec = jnp.zeros((rows, D), F32).at[0].set(c_ctx).at[1:1 + Bs].set(c)
    mod_all = _ada_table(cvec, ada_w, ada_b)[:, :1 + Bs].reshape(DEPTH, 1 + Bs, 1, 6 * D)

    H = MLSTM_HEADS
    mod = mod_all[0]
    w_in = mlstm_w_in[0]
    w_main = w_in[:, :MLSTM_MAIN].astype(BF16)
    w_gate = jnp.pad(w_in[:, MLSTM_MAIN:], ((0, 0), (0, GATE_PAD - 4 * H))).astype(BF16)
    gate_b = jnp.pad(mlstm_gate_b[0], (0, GATE_PAD - 4 * H))
    col_scale = jnp.concatenate([jnp.full((MLSTM_QK,), MLSTM_DK ** -0.5, F32),
                                 jnp.ones((MLSTM_MAIN - MLSTM_QK,), F32)])
    qkvo, gates = _nm_matmul(x, norm_mix[0], mod, w_main, col_scale, jnp.zeros((MLSTM_MAIN,), F32),
                             shift_idx=0, scale_idx=1, out_dtype=BF16, tm=tm512, tn=1024,
                             aux_w=w_gate, aux_b=gate_b, name="mlstm_proj", **dims)
    L = 256
    state = (state_mlstm_C[:, 0], state_mlstm_n[:, 0], state_mlstm_m[:, 0])
    hf, hb, C_p, n_p, m_p = _mlstm_scan(qkvo, gates, Bp, Sp, Bs, Ss, L, state)
    x = _mm_residual((hf, hb, qkvo, mlstm_head_g[0]), mlstm_w_out[0].astype(BF16), x, mod, gate_idx=2,
                     tm=tm512, tn=1024, mlstm_prologue=True, name="mlstm_out", **dims)
    x = _moe_layer(x, norm_ffn[0], mod, moe_router_w[0], moe_router_b[0], moe_w1[0], moe_w3[0], moe_w2[0],
                   shared_w1[0].astype(BF16), shared_w3[0].astype(BF16), shared_w2[0].astype(BF16), **dims)

    mod = mod_all[1]
    qkv = _nm_matmul(x, norm_mix[1], mod, attn_w_qkv[0].astype(BF16), jnp.ones((ATTN_PROJ,), F32),
                     jnp.zeros((ATTN_PROJ,), F32), shift_idx=0, scale_idx=1, out_dtype=F32, tm=tm512, tn=1024,
                     name="attn_qkv", **dims)
    ident_cos = jnp.ones((Sp, HEAD_DIM), F32)
    ident_sin = jnp.zeros((Sp, HEAD_DIM), F32)
    q_p, k_p, v_p, kf_p, vf_p = _qk_prep(qkv, attn_q_g[0], attn_k_g[0], ident_cos, ident_sin, 0, n_prompt, Sp,
                                         True, "qk_prep_prompt")
    cos, sin = _rope_tables(Ss)
    q_s, k_s, v_s = _qk_prep(qkv, attn_q_g[0], attn_k_g[0], cos, sin, n_prompt, Bs * Ss, 256, False,
                             "qk_prep_latent")
    o_p = _attn_context(attn_sink[0], q_p, k_p, v_p, Bp, Sp)
    P = cache_attn_k.shape[2]
    ctx_k = cache_attn_k[:, 0].reshape(Bs, P, ATTN_KV).astype(BF16)
    ctx_v = cache_attn_v[:, 0].reshape(Bs, P, ATTN_KV).astype(BF16)
    o_s = _attn_latent(attn_sink[0], q_s, k_s, v_s, ctx_k, ctx_v, Bs, Ss)
    x = _mm_residual((o_p, o_s), attn_w_o[0].astype(BF16), x, mod, gate_idx=2,
                     tm=_row_tile(1024, n_prompt, Ss), tn=1024,
                     mlstm_prologue=False, name="attn_out", **dims)
    y_prompt, y_sample = _moe_layer(
        x, norm_ffn[1], mod, moe_router_w[1], moe_router_b[1], moe_w1[1], moe_w3[1], moe_w2[1],
        shared_w1[1].astype(BF16), shared_w3[1].astype(BF16), shared_w2[1].astype(BF16), split_outputs=True, **dims)
    return (y_prompt.reshape(Bp, Sp, D), y_sample.reshape(Bs, Ss, D), C_p[:, None], n_p[:, None], m_p[:, None],
            kf_p.reshape(Bp, 1, Sp, ATTN_KV_HEADS, HEAD_DIM), vf_p.reshape(Bp, 1, Sp, ATTN_KV_HEADS, HEAD_DIM))
```

```python
import functools

import jax
import jax.numpy as jnp
from jax import lax
from jax.experimental import pallas as pl
from jax.experimental.pallas import tpu as pltpu

F32 = jnp.float32
BF16 = jnp.bfloat16
I32 = jnp.int32

D_MODEL = 2048
DEPTH = 2
EPS = 1e-6
GRID_W = 64
MLSTM_HEADS = 8
MLSTM_DK = 128
MLSTM_DV = 256
MLSTM_QK = MLSTM_HEADS * MLSTM_DK
MLSTM_V = MLSTM_HEADS * MLSTM_DV
MLSTM_MAIN = 2 * MLSTM_QK + 2 * MLSTM_V
GATE_PAD = 128
HEAD_DIM = 128
ATTN_Q_HEADS = 16
ATTN_KV_HEADS = 4
ATTN_GROUPS = 4
WINDOW = 128
QBLK = 128
ROPE_THETA = 10000.0
ATTN_Q = ATTN_Q_HEADS * HEAD_DIM
ATTN_KV = ATTN_KV_HEADS * HEAD_DIM
ATTN_PROJ = ATTN_Q + 2 * ATTN_KV
N_EXPERTS = 64
TOP_K = 8
N_EXPERT_GROUPS = 8
TOPK_GROUPS = 4
GROUP_SIZE = N_EXPERTS // N_EXPERT_GROUPS
EXPERT_FF = 512
ROUTED_SCALE = 2.5
MOE_BLOCK = 512
COMBINE_TM = 128

V7X_VMEM_LIMIT = 56 * 1024 * 1024
NEG_INF = float("-inf")


def _cparams(sem):
    return pltpu.CompilerParams(dimension_semantics=("arbitrary",) * len(sem), vmem_limit_bytes=V7X_VMEM_LIMIT)


def _split_hi_lo(a):
    hi = a.astype(BF16)
    lo = (a - hi.astype(F32)).astype(BF16)
    return hi, lo


def _dot(a, b):
    return jnp.dot(a, b, preferred_element_type=F32)


def _dot_nt(a, b):
    return lax.dot_general(a, b, (((1,), (1,)), ((), ())), preferred_element_type=F32)


def _dot_tn(a, b):
    return lax.dot_general(a, b, (((0,), (0,)), ((), ())), preferred_element_type=F32)


def _silu(x):
    return x * jax.nn.sigmoid(x)


def _row_tile(preferred, n_prompt, dec_seq):
    tm = preferred
    while n_prompt % tm or dec_seq % tm:
        tm //= 2
    return tm


def _mod_row(i, tm, n_prompt, dec_seq):
    r0 = i * tm
    return jnp.where(r0 < n_prompt, 0, 1 + (r0 - n_prompt) // dec_seq)


def _norm_modulate(x, gain, mod_ref, shift_idx, scale_idx):
    D = D_MODEL
    y = x * lax.rsqrt(jnp.mean(x * x, axis=-1, keepdims=True) + EPS) * gain
    shift = mod_ref[:, shift_idx * D:(shift_idx + 1) * D]
    scale = mod_ref[:, scale_idx * D:(scale_idx + 1) * D]
    return y * (1.0 + scale) + shift


def _ada_kernel(c_ref, w_ref, b_ref, o_ref):
    s = _silu(c_ref[...])
    s_hi, s_lo = _split_hi_lo(s)
    w_hi, w_lo = _split_hi_lo(w_ref[...])
    o_ref[...] = _dot(s_hi, w_hi) + _dot(s_hi, w_lo) + _dot(s_lo, w_hi) + b_ref[...]


def _ada_table(cvec, ada_w, ada_b):
    D = D_MODEL
    tn = 1024
    rows = cvec.shape[0]
    return pl.pallas_call(
        _ada_kernel,
        grid=(DEPTH, 6 * D // tn),
        in_specs=[
            pl.BlockSpec((rows, D), lambda l, j: (0, 0)),
            pl.BlockSpec((None, D, tn), lambda l, j: (l, 0, j)),
            pl.BlockSpec((None, 1, tn), lambda l, j: (l, 0, j)),
        ],
        out_specs=pl.BlockSpec((None, rows, tn), lambda l, j: (l, 0, j)),
        out_shape=jax.ShapeDtypeStruct((DEPTH, rows, 6 * D), F32),
        compiler_params=_cparams(("parallel", "parallel")),
        name="ada_table",
    )(cvec, ada_w, ada_b.reshape(DEPTH, 1, 6 * D))


def _nm_matmul_kernel(x_ref, gain_ref, mod_ref, w_ref, cs_ref, cb_ref, *rest, shift_idx, scale_idx, has_aux):
    if has_aux:
        wa_ref, ab_ref, o_ref, aux_ref, h_scr = rest
    else:
        o_ref, h_scr = rest

    @pl.when(pl.program_id(1) == 0)
    def _():
        h = _norm_modulate(x_ref[...], gain_ref[...], mod_ref, shift_idx, scale_idx).astype(BF16)
        h_scr[...] = h
        if has_aux:
            aux_ref[...] = _dot(h, wa_ref[...]) + ab_ref[...]

    acc = _dot(h_scr[...], w_ref[...])
    o_ref[...] = (acc * cs_ref[...] + cb_ref[...]).astype(o_ref.dtype)


def _nm_matmul(x, gain, mod, w, col_scale, col_bias, *, shift_idx, scale_idx, n_prompt, dec_seq,
               out_dtype, tm, tn, aux_w=None, aux_b=None, name):
    N, D = x.shape
    P = w.shape[1]
    assert N % tm == 0 and P % tn == 0 and n_prompt % tm == 0 and dec_seq % tm == 0
    has_aux = aux_w is not None
    row = functools.partial(_mod_row, tm=tm, n_prompt=n_prompt, dec_seq=dec_seq)
    in_specs = [
        pl.BlockSpec((tm, D), lambda i, j: (i, 0)),
        pl.BlockSpec((1, D), lambda i, j: (0, 0)),
        pl.BlockSpec((None, 1, 6 * D), lambda i, j: (row(i), 0, 0)),
        pl.BlockSpec((D, tn), lambda i, j: (0, j)),
        pl.BlockSpec((1, tn), lambda i, j: (0, j)),
        pl.BlockSpec((1, tn), lambda i, j: (0, j)),
    ]
    args = [x, gain.reshape(1, D), mod, w, col_scale.reshape(1, P), col_bias.reshape(1, P)]
    out_specs = pl.BlockSpec((tm, tn), lambda i, j: (i, j))
    out_shape = jax.ShapeDtypeStruct((N, P), out_dtype)
    if has_aux:
        PA = aux_w.shape[1]
        in_specs += [pl.BlockSpec((D, PA), lambda i, j: (0, 0)), pl.BlockSpec((1, PA), lambda i, j: (0, 0))]
        args += [aux_w, aux_b.reshape(1, PA)]
        out_specs = [out_specs, pl.BlockSpec((tm, PA), lambda i, j: (i, 0))]
        out_shape = [out_shape, jax.ShapeDtypeStruct((N, PA), F32)]
    return pl.pallas_call(
        functools.partial(_nm_matmul_kernel, shift_idx=shift_idx, scale_idx=scale_idx, has_aux=has_aux),
        grid=(N // tm, P // tn),
        in_specs=in_specs,
        out_specs=out_specs,
        out_shape=out_shape,
        scratch_shapes=[pltpu.VMEM((tm, D), BF16)],
        compiler_params=_cparams(("parallel", "arbitrary")),
        name=name,
    )(*args)


def _mm_residual_kernel(*refs, mlstm_prologue, prompt_tiles):
    if mlstm_prologue:
        hf_ref, hb_ref, og_ref, hg_ref, w_ref, x_ref, gate_ref, o_ref, l_scr = refs

        @pl.when(pl.program_id(1) == 0)
        def _():
            hs = hf_ref[...] + hb_ref[...]
            og = jax.nn.sigmoid(og_ref[...].astype(F32))
            for h in range(MLSTM_HEADS):
                sl = slice(h * MLSTM_DV, (h + 1) * MLSTM_DV)
                hh = hs[:, sl]
                hn = hh * lax.rsqrt(jnp.mean(hh * hh, axis=-1, keepdims=True) + EPS) * hg_ref[:, sl]
                l_scr[:, sl] = (og[:, sl] * hn).astype(BF16)

        o_ref[...] = x_ref[...] + gate_ref[...] * _dot(l_scr[...], w_ref[...])
    else:
        lp_ref, ls_ref, w_ref, x_ref, gate_ref, o_ref = refs
        context = pl.program_id(0) < prompt_tiles

        @pl.when(context)
        def _():
            o_ref[...] = x_ref[...] + gate_ref[...] * _dot(lp_ref[...], w_ref[...])

        @pl.when(jnp.logical_not(context))
        def _():
            o_ref[...] = x_ref[...] + gate_ref[...] * _dot(ls_ref[...], w_ref[...])


def _mm_residual(lhs_args, w, x, mod, *, gate_idx, n_prompt, dec_seq, tm, tn, mlstm_prologue, name):
    N, D = x.shape
    K = w.shape[0]
    assert N % tm == 0 and D % tn == 0 and n_prompt % tm == 0 and dec_seq % tm == 0
    row = functools.partial(_mod_row, tm=tm, n_prompt=n_prompt, dec_seq=dec_seq)
    pt = n_prompt // tm
    if mlstm_prologue:
        hf, hb, qkvo, head_g = lhs_args
        o_blk = (2 * MLSTM_QK + MLSTM_V) // MLSTM_V
        in_specs = [
            pl.BlockSpec((tm, K), lambda i, j: (i, 0)),
            pl.BlockSpec((tm, K), lambda i, j: (i, 0)),
            pl.BlockSpec((tm, MLSTM_V), lambda i, j: (i, o_blk)),
            pl.BlockSpec((1, K), lambda i, j: (0, 0)),
        ]
        args = [hf, hb, qkvo, head_g.reshape(1, K)]
        scratch = [pltpu.VMEM((tm, K), BF16)]
    else:
        lhs_p, lhs_s = lhs_args
        in_specs = [pl.BlockSpec((tm, K), lambda i, j: (jnp.minimum(i, pt - 1), 0)),
                    pl.BlockSpec((tm, K), lambda i, j: (jnp.maximum(i - pt, 0), 0))]
        args = [lhs_p, lhs_s]
        scratch = []
    in_specs += [
        pl.BlockSpec((K, tn), lambda i, j: (0, j)),
        pl.BlockSpec((tm, tn), lambda i, j: (i, j)),
        pl.BlockSpec((None, 1, tn), lambda i, j: (row(i), 0, gate_idx * (D // tn) + j)),
    ]
    args += [w, x, mod]
    return pl.pallas_call(
        functools.partial(_mm_residual_kernel, mlstm_prologue=mlstm_prologue, prompt_tiles=pt),
        grid=(N // tm, D // tn),
        in_specs=in_specs,
        out_specs=pl.BlockSpec((tm, tn), lambda i, j: (i, j)),
        out_shape=jax.ShapeDtypeStruct((N, D), F32),
        scratch_shapes=scratch,
        compiler_params=_cparams(("parallel", "arbitrary")),
        name=name,
    )(*args)


def _log_sigmoid(x):
    return jnp.minimum(x, 0.0) - jnp.log(1.0 + jnp.exp(-jnp.abs(x)))


def _mlstm_direction(d, q_ref, k_ref, v_ref, g_ref, h_ref, C_scr, n_scr, m_scr, L):
    H, DK, DV = MLSTM_HEADS, MLSTM_DK, MLSTM_DV
    g = g_ref[...]
    lf = _log_sigmoid(g)
    r = lax.broadcasted_iota(I32, (L, L), 0)
    c = lax.broadcasted_iota(I32, (L, L), 1)
    causal = (c <= r) if d == 0 else (c >= r)
    tri = jnp.where(causal, 1.0, 0.0).astype(BF16)
    lf1 = lf.astype(BF16)
    rem = lf - lf1.astype(F32)
    lf2 = rem.astype(BF16)
    lf3 = (rem - lf2.astype(F32)).astype(BF16)
    bsum = _dot(tri, lf1) + _dot(tri, lf2) + _dot(tri, lf3)
    g_t = g.T
    b_t = bsum.T
    end = L - 1 if d == 0 else 0
    for h in range(H):
        ci, cf = d * 2 * H + h, d * 2 * H + H + h
        b_col, i_col = bsum[:, cf:cf + 1], g[:, ci:ci + 1]
        b_row, i_row = b_t[cf:cf + 1, :], g_t[ci:ci + 1, :]
        b_end = bsum[end:end + 1, cf:cf + 1]
        m = m_scr[d, h]
        C = C_scr[d, h]
        n = n_scr[d, h]
        qh = q_ref[:, h * DK:(h + 1) * DK]
        kh = k_ref[:, h * DK:(h + 1) * DK]
        vh = v_ref[:, h * DV:(h + 1) * DV]
        dmat = jnp.where(causal, b_col - b_row + i_row, NEG_INF)
        inter = b_col + m
        m_out = jnp.maximum(inter, jnp.max(dmat, axis=-1, keepdims=True))
        w = jnp.exp(dmat - m_out) * _dot_nt(qh, kh)
        dec = jnp.exp(inter - m_out)
        num = _dot(w.astype(BF16), vh) + dec * _dot(qh, C.astype(BF16))
        den = jnp.sum(w, axis=-1, keepdims=True) + dec * jnp.sum(qh.astype(F32) * n, axis=-1, keepdims=True)
        h_ref[:, h * DV:(h + 1) * DV] = num / jnp.maximum(jnp.abs(den), jnp.exp(-m_out))
        to_end = b_end - b_col + i_col
        m_new = jnp.maximum(b_end + m, jnp.max(to_end, axis=0, keepdims=True))
        wk = jnp.exp(to_end - m_new)
        dec_end = jnp.exp(b_end + m - m_new)
        kw = kh.astype(F32) * wk
        C_scr[d, h] = dec_end * C + _dot_tn(kw.astype(BF16), vh)
        n_scr[d, h] = dec_end * n + jnp.sum(kw, axis=0, keepdims=True)
        m_scr[d, h] = m_new


def _mlstm_kernel(fwd_ref, bwd_ref, bidx_ref, flag_ref, qf, kf, vf, gf, qb, kb, vb, gb, C0, n0, m0,
                  hf, hb, Co, no, mo, C_scr, n_scr, m_scr, *, L):
    H = MLSTM_HEADS
    flags = flag_ref[pl.program_id(0)]
    first, last, context = (flags & 1) != 0, (flags & 2) != 0, (flags & 4) != 0

    @pl.when(first & context)
    def _():
        C_scr[...] = jnp.zeros_like(C_scr)
        n_scr[...] = jnp.zeros_like(n_scr)
        m_scr[...] = jnp.zeros_like(m_scr)

    @pl.when(first & jnp.logical_not(context))
    def _():
        C_scr[...] = C0[...]
        for d in range(2):
            for h in range(H):
                n_scr[d, h] = n0[d, h:h + 1, :]
                m_scr[d, h] = m0[d:d + 1, h:h + 1]

    _mlstm_direction(0, qf, kf, vf, gf, hf, C_scr, n_scr, m_scr, L)
    _mlstm_direction(1, qb, kb, vb, gb, hb, C_scr, n_scr, m_scr, L)

    @pl.when(last & context)
    def _():
        Co[...] = C_scr[...]
        for d in range(2):
            for h in range(H):
                no[d, h:h + 1, :] = n_scr[d, h]
                mo[d:d + 1, h:h + 1] = m_scr[d, h]


def _mlstm_scan(qkvo, gates, Bp, Sp, Bs, Ss, L, state):
    H, DK, DV = MLSTM_HEADS, MLSTM_DK, MLSTM_DV
    assert Sp % L == 0 and Ss % L == 0
    fwd, bwd, bidx, flags = [], [], [], []
    for context, B, S, base in ((1, Bp, Sp, 0), (0, Bs, Ss, Bp * Sp // L)):
        nc = S // L
        for b in range(B):
            for c in range(nc):
                fwd.append(base + b * nc + c)
                bwd.append(base + b * nc + nc - 1 - c)
                bidx.append(b)
                flags.append((c == 0) * 1 + (c == nc - 1) * 2 + context * 4)
    tables = [jnp.asarray(t, I32) for t in (fwd, bwd, bidx, flags)]
    lat_b = lambda s, fl, bi: jnp.where((fl[s] & 4) != 0, 0, bi[s])
    ctx_b = lambda s, fl, bi: jnp.where((fl[s] & 4) != 0, bi[s], Bp - 1)

    def specs(tbl):
        return [
            pl.BlockSpec((L, MLSTM_QK), lambda s, f, w, bi, fl: ((f, w)[tbl][s], 0)),
            pl.BlockSpec((L, MLSTM_QK), lambda s, f, w, bi, fl: ((f, w)[tbl][s], 1)),
            pl.BlockSpec((L, MLSTM_V), lambda s, f, w, bi, fl: ((f, w)[tbl][s], 1)),
            pl.BlockSpec((L, GATE_PAD), lambda s, f, w, bi, fl: ((f, w)[tbl][s], 0)),
        ]

    def state_specs(which):
        return [
            pl.BlockSpec((None, 2, H, DK, DV), lambda s, f, w, bi, fl: (which(s, fl, bi), 0, 0, 0, 0)),
            pl.BlockSpec((None, 2, H, DK), lambda s, f, w, bi, fl: (which(s, fl, bi), 0, 0, 0)),
            pl.BlockSpec((None, 2, H), lambda s, f, w, bi, fl: (which(s, fl, bi), 0, 0)),
        ]

    N = qkvo.shape[0]
    return pl.pallas_call(
        functools.partial(_mlstm_kernel, L=L),
        grid_spec=pltpu.PrefetchScalarGridSpec(
            num_scalar_prefetch=4,
            grid=(len(fwd),),
            in_specs=specs(0) + specs(1) + state_specs(lat_b),
            out_specs=[
                pl.BlockSpec((L, MLSTM_V), lambda s, f, w, bi, fl: (f[s], 0)),
                pl.BlockSpec((L, MLSTM_V), lambda s, f, w, bi, fl: (w[s], 0)),
            ] + state_specs(ctx_b),
            scratch_shapes=[
                pltpu.VMEM((2, H, DK, DV), F32),
                pltpu.VMEM((2, H, 1, DK), F32),
                pltpu.VMEM((2, H, 1, 1), F32),
            ],
        ),
        out_shape=[
            jax.ShapeDtypeStruct((N, MLSTM_V), F32),
            jax.ShapeDtypeStruct((N, MLSTM_V), F32),
            jax.ShapeDtypeStruct((Bp, 2, H, DK, DV), F32),
            jax.ShapeDtypeStruct((Bp, 2, H, DK), F32),
            jax.ShapeDtypeStruct((Bp, 2, H), F32),
        ],
        compiler_params=_cparams(("arbitrary",)),
        name="mlstm_scan",
    )(*tables, qkvo, qkvo, qkvo, gates, qkvo, qkvo, qkvo, gates, *state)


def _qk_prep_kernel(qkv_ref, qg_ref, kg_ref, cos_ref, sin_ref, *outs, emit_f32):
    if emit_f32:
        q_ref, k_ref, v_ref, kf_ref, vf_ref = outs
    else:
        q_ref, k_ref, v_ref = outs
    cos = cos_ref[...]
    sin = sin_ref[...]
    lane = lax.broadcasted_iota(I32, cos.shape, 1)
    first = (lane % (HEAD_DIM // 2)) < (HEAD_DIM // 4)

    def norm_rope(x, gain):
        xn = x * lax.rsqrt(jnp.mean(x * x, axis=-1, keepdims=True) + EPS) * gain
        partner = jnp.where(first, pltpu.roll(xn, HEAD_DIM - HEAD_DIM // 4, 1), pltpu.roll(xn, HEAD_DIM // 4, 1))
        return xn * cos + partner * sin

    for h in range(ATTN_Q_HEADS):
        sl = slice(h * HEAD_DIM, (h + 1) * HEAD_DIM)
        q_ref[:, sl] = norm_rope(qkv_ref[:, sl], qg_ref[...]).astype(BF16)
    for h in range(ATTN_KV_HEADS):
        sl = slice(h * HEAD_DIM, (h + 1) * HEAD_DIM)
        kx = norm_rope(qkv_ref[:, ATTN_Q + h * HEAD_DIM:ATTN_Q + (h + 1) * HEAD_DIM], kg_ref[...])
        k_ref[:, sl] = kx.astype(BF16)
        if emit_f32:
            kf_ref[:, sl] = kx
    vx = qkv_ref[:, ATTN_Q + ATTN_KV:]
    v_ref[...] = vx.astype(BF16)
    if emit_f32:
        vf_ref[...] = vx


def _qk_prep(qkv, q_g, k_g, cos, sin, row0, rows, tm, emit_f32, name):
    assert rows % tm == 0 and row0 % tm == 0 and cos.shape[0] % tm == 0
    base = row0 // tm
    nt = cos.shape[0] // tm
    out_specs = [
        pl.BlockSpec((tm, ATTN_Q), lambda i: (i, 0)),
        pl.BlockSpec((tm, ATTN_KV), lambda i: (i, 0)),
        pl.BlockSpec((tm, ATTN_KV), lambda i: (i, 0)),
    ]
    out_shape = [
        jax.ShapeDtypeStruct((rows, ATTN_Q), BF16),
        jax.ShapeDtypeStruct((rows, ATTN_KV), BF16),
        jax.ShapeDtypeStruct((rows, ATTN_KV), BF16),
    ]
    if emit_f32:
        out_specs += [pl.BlockSpec((tm, ATTN_KV), lambda i: (i, 0))] * 2
        out_shape += [jax.ShapeDtypeStruct((rows, ATTN_KV), F32)] * 2
    return pl.pallas_call(
        functools.partial(_qk_prep_kernel, emit_f32=emit_f32),
        grid=(rows // tm,),
        in_specs=[
            pl.BlockSpec((tm, ATTN_PROJ), lambda i: (base + i, 0)),
            pl.BlockSpec((1, HEAD_DIM), lambda i: (0, 0)),
            pl.BlockSpec((1, HEAD_DIM), lambda i: (0, 0)),
            pl.BlockSpec((tm, HEAD_DIM), lambda i: (i % nt, 0)),
            pl.BlockSpec((tm, HEAD_DIM), lambda i: (i % nt, 0)),
        ],
        out_specs=out_specs,
        out_shape=out_shape,
        compiler_params=_cparams(("parallel",)),
        name=name,
    )(qkv, q_g.reshape(1, HEAD_DIM), k_g.reshape(1, HEAD_DIM), cos, sin)


def _sink_column(sink_ref, kv, rows_per_head):
    parts = [jnp.full((rows_per_head, 1), sink_ref[kv * ATTN_GROUPS + g], F32) for g in range(ATTN_GROUPS)]
    return jnp.concatenate(parts, axis=0)


def _attn_ctx_kernel(sink_ref, q_ref, k_ref, v_ref, o_ref):
    S = q_ref.shape[0]
    scale = HEAD_DIM ** -0.5
    for kv in range(ATTN_KV_HEADS):
        q = jnp.concatenate(
            [q_ref[:, (kv * ATTN_GROUPS + g) * HEAD_DIM:(kv * ATTN_GROUPS + g + 1) * HEAD_DIM]
             for g in range(ATTN_GROUPS)], axis=0)
        ksl = slice(kv * HEAD_DIM, (kv + 1) * HEAD_DIM)
        s = _dot_nt(q, k_ref[:, ksl]) * scale
        sk = _sink_column(sink_ref, kv, S)
        m = jnp.maximum(jnp.max(s, axis=-1, keepdims=True), sk)
        p = jnp.exp(s - m)
        den = jnp.sum(p, axis=-1, keepdims=True) + jnp.exp(sk - m)
        o = _dot((p / den).astype(BF16), v_ref[:, ksl])
        for g in range(ATTN_GROUPS):
            hq = kv * ATTN_GROUPS + g
            o_ref[:, hq * HEAD_DIM:(hq + 1) * HEAD_DIM] = o[g * S:(g + 1) * S].astype(BF16)


def _attn_context(sink, q, k, v, B, S):
    return pl.pallas_call(
        _attn_ctx_kernel,
        grid_spec=pltpu.PrefetchScalarGridSpec(
            num_scalar_prefetch=0,
            grid=(B,),
            in_specs=[
                pl.BlockSpec(memory_space=pltpu.SMEM),
                pl.BlockSpec((S, ATTN_Q), lambda b: (b, 0)),
                pl.BlockSpec((S, ATTN_KV), lambda b: (b, 0)),
                pl.BlockSpec((S, ATTN_KV), lambda b: (b, 0)),
            ],
            out_specs=pl.BlockSpec((S, ATTN_Q), lambda b: (b, 0)),
        ),
        out_shape=jax.ShapeDtypeStruct((B * S, ATTN_Q), BF16),
        compiler_params=_cparams(("parallel",)),
        name="attn_context",
    )(sink, q, k, v)


def _attn_lat_kernel(sink_ref, q_ref, kp_ref, kc_ref, kn_ref, vp_ref, vc_ref, vn_ref, kx_ref, vx_ref, o_ref):
    i = pl.program_id(1)
    nb = pl.num_programs(1)
    scale = HEAD_DIM ** -0.5
    R = ATTN_GROUPS * QBLK
    r = lax.broadcasted_iota(I32, (R, QBLK), 0) % QBLK
    c = lax.broadcasted_iota(I32, (R, QBLK), 1)
    ok_prev = c >= r
    ok_next = c <= r
    edge_prev = jnp.where(i > 0, 0.0, NEG_INF)
    edge_next = jnp.where(i < nb - 1, 0.0, NEG_INF)
    for kv in range(ATTN_KV_HEADS):
        q = jnp.concatenate(
            [q_ref[:, (kv * ATTN_GROUPS + g) * HEAD_DIM:(kv * ATTN_GROUPS + g + 1) * HEAD_DIM]
             for g in range(ATTN_GROUPS)], axis=0)
        ksl = slice(kv * HEAD_DIM, (kv + 1) * HEAD_DIM)
        s_p = jnp.where(ok_prev, _dot_nt(q, kp_ref[:, ksl]) * scale + edge_prev, NEG_INF)
        s_c = _dot_nt(q, kc_ref[:, ksl]) * scale
        s_n = jnp.where(ok_next, _dot_nt(q, kn_ref[:, ksl]) * scale + edge_next, NEG_INF)
        s_x = _dot_nt(q, kx_ref[:, ksl]) * scale
        sk = _sink_column(sink_ref, kv, QBLK)
        m = jnp.maximum(
            jnp.maximum(jnp.max(s_p, axis=-1, keepdims=True), jnp.max(s_c, axis=-1, keepdims=True)),
            jnp.maximum(jnp.max(s_n, axis=-1, keepdims=True), jnp.max(s_x, axis=-1, keepdims=True)))
        m = jnp.maximum(m, sk)
        p_p, p_c, p_n, p_x = jnp.exp(s_p - m), jnp.exp(s_c - m), jnp.exp(s_n - m), jnp.exp(s_x - m)
        den = (jnp.sum(p_p, axis=-1, keepdims=True) + jnp.sum(p_c, axis=-1, keepdims=True)
               + jnp.sum(p_n, axis=-1, keepdims=True) + jnp.sum(p_x, axis=-1, keepdims=True) + jnp.exp(sk - m))
        inv = 1.0 / den
        o = (_dot((p_p * inv).astype(BF16), vp_ref[:, ksl]) + _dot((p_c * inv).astype(BF16), vc_ref[:, ksl])
             + _dot((p_n * inv).astype(BF16), vn_ref[:, ksl]) + _dot((p_x * inv).astype(BF16), vx_ref[:, ksl]))
        for g in range(ATTN_GROUPS):
            hq = kv * ATTN_GROUPS + g
            o_ref[:, hq * HEAD_DIM:(hq + 1) * HEAD_DIM] = o[g * QBLK:(g + 1) * QBLK].astype(BF16)


def _attn_latent(sink, q, k, v, ctx_k, ctx_v, B, S):
    nb = S // QBLK
    P = ctx_k.shape[1]
    prev = lambda b, i: (b * nb + jnp.maximum(i - 1, 0), 0)
    cur = lambda b, i: (b * nb + i, 0)
    nxt = lambda b, i: (b * nb + jnp.minimum(i + 1, nb - 1), 0)
    kv_spec = lambda f: pl.BlockSpec((QBLK, ATTN_KV), f)
    return pl.pallas_call(
        _attn_lat_kernel,
        grid_spec=pltpu.PrefetchScalarGridSpec(
            num_scalar_prefetch=0,
            grid=(B, nb),
            in_specs=[
                pl.BlockSpec(memory_space=pltpu.SMEM),
                pl.BlockSpec((QBLK, ATTN_Q), cur),
                kv_spec(prev), kv_spec(cur), kv_spec(nxt),
                kv_spec(prev), kv_spec(cur), kv_spec(nxt),
                pl.BlockSpec((None, P, ATTN_KV), lambda b, i: (b, 0, 0)),
                pl.BlockSpec((None, P, ATTN_KV), lambda b, i: (b, 0, 0)),
            ],
            out_specs=pl.BlockSpec((QBLK, ATTN_Q), cur),
        ),
        out_shape=jax.ShapeDtypeStruct((B * S, ATTN_Q), BF16),
        compiler_params=_cparams(("parallel", "parallel")),
        name="attn_latent",
    )(sink, q, k, k, k, v, v, v, ctx_k, ctx_v)


SLAB = 8
U32 = jnp.uint32


def _pack_rows(x):
    half = x.shape[1] // 2
    bits = lax.bitcast_convert_type(x.astype(BF16).astype(F32), U32)
    return (bits[:, :half] >> 16) | (bits[:, half:] & jnp.uint32(0xFFFF0000))


def _store_slabs(ref, words):
    R = words.shape[0]
    for j in range(SLAB):
        ref[pl.ds(j, R, stride=SLAB), :] = words[:, j * 128:(j + 1) * 128]


def _load_slab_halves(ref, R, j):
    w = ref[pl.ds(j, R, stride=SLAB), :]
    return lax.bitcast_convert_type(w << 16, F32), lax.bitcast_convert_type(w & jnp.uint32(0xFFFF0000), F32)


def _load_rows_bf16(ref, R):
    halves = [_load_slab_halves(ref, R, j) for j in range(SLAB)]
    return jnp.concatenate([lo for lo, _ in halves] + [hi for _, hi in halves], axis=1).astype(BF16)


def _slab_copy(src, dst, sem, src_row8, dst_row):
    return pltpu.make_async_copy(src.at[pl.ds(pl.multiple_of(src_row8, SLAB), SLAB), :],
                                 dst.at[pl.ds(dst_row * SLAB, SLAB), :], sem)


def _router_kernel(x_ref, gain_ref, mod_ref, wr_ref, rb_ref, h_ref, e_ref, g_ref, r_ref, cnt_ref, cnt_scr, *, tm):
    E, GS = N_EXPERTS, GROUP_SIZE
    i = pl.program_id(0)

    @pl.when(i == 0)
    def _():
        cnt_scr[...] = jnp.zeros_like(cnt_scr)

    h = _norm_modulate(x_ref[...], gain_ref[...], mod_ref, 3, 4)
    _store_slabs(h_ref, _pack_rows(h))
    h_hi, h_lo = _split_hi_lo(h)
    w_hi, w_lo = _split_hi_lo(wr_ref[...])
    logits = _dot_nt(w_hi, h_hi) + _dot_nt(w_lo, h_hi) + _dot_nt(w_hi, h_lo)
    scores = jax.nn.sigmoid(logits)
    biased = scores + rb_ref[...]
    sub = lax.broadcasted_iota(I32, (GS, tm), 0)
    gscore = []
    for gi in range(N_EXPERT_GROUPS):
        xg = biased[gi * GS:(gi + 1) * GS, :]
        m1 = jnp.max(xg, axis=0, keepdims=True)
        first = jnp.min(jnp.where(xg == m1, sub, GS), axis=0, keepdims=True)
        m2 = jnp.max(jnp.where(sub == first, NEG_INF, xg), axis=0, keepdims=True)
        gscore.append(m1 + m2)
    cur = jnp.concatenate(gscore, axis=0)
    gid = lax.broadcasted_iota(I32, (N_EXPERT_GROUPS, tm), 0)
    gsel = jnp.zeros((N_EXPERT_GROUPS, tm), F32)
    for _ in range(TOPK_GROUPS):
        mx = jnp.max(cur, axis=0, keepdims=True)
        first = jnp.min(jnp.where(cur == mx, gid, N_EXPERT_GROUPS), axis=0, keepdims=True)
        hit = gid == first
        gsel = jnp.where(hit, 1.0, gsel)
        cur = jnp.where(hit, NEG_INF, cur)
    ok = jnp.concatenate(
        [jnp.broadcast_to(gsel[gi:gi + 1, :], (GS, tm)) for gi in range(N_EXPERT_GROUPS)], axis=0)
    masked = jnp.where(ok > 0.5, biased, NEG_INF)
    eid = lax.broadcasted_iota(I32, (E, tm), 0)
    sel = jnp.zeros((E, tm), F32)
    picks, pick_scores = [], []
    for _ in range(TOP_K):
        mx = jnp.max(masked, axis=0, keepdims=True)
        first = jnp.min(jnp.where(masked == mx, eid, E), axis=0, keepdims=True)
        hit = eid == first
        picks.append(first)
        pick_scores.append(jnp.sum(jnp.where(hit, scores, 0.0), axis=0, keepdims=True))
        sel = jnp.where(hit, 1.0, sel)
        masked = jnp.where(hit, NEG_INF, masked)
    total = pick_scores[0]
    for s in pick_scores[1:]:
        total = total + s
    g_ref[...] = jnp.concatenate(pick_scores, axis=0) / total * ROUTED_SCALE
    e_ref[...] = jnp.concatenate(picks, axis=0)
    rr = lax.broadcasted_iota(I32, (tm, tm), 0)
    cc = lax.broadcasted_iota(I32, (tm, tm), 1)
    before = jnp.where(rr < cc, 1.0, 0.0).astype(BF16)
    rank = _dot(sel.astype(BF16), before) + cnt_scr[...]
    r_ref[...] = jnp.concatenate(
        [jnp.sum(jnp.where(eid == p, rank, 0.0), axis=0, keepdims=True) for p in picks], axis=0).astype(I32)
    cnt_scr[...] = cnt_scr[...] + jnp.sum(sel, axis=-1, keepdims=True)
    cnt_ref[...] = cnt_scr[...].astype(I32)


def _router(x, gain, mod, router_w_t, router_b, *, n_prompt, dec_seq, tm):
    N, D = x.shape
    E = N_EXPERTS
    assert D == 2 * SLAB * 128
    assert N % tm == 0 and n_prompt % tm == 0 and dec_seq % tm == 0
    row = functools.partial(_mod_row, tm=tm, n_prompt=n_prompt, dec_seq=dec_seq)
    return pl.pallas_call(
        functools.partial(_router_kernel, tm=tm),
        grid=(N // tm,),
        in_specs=[
            pl.BlockSpec((tm, D), lambda i: (i, 0)),
            pl.BlockSpec((1, D), lambda i: (0, 0)),
            pl.BlockSpec((None, 1, 6 * D), lambda i: (row(i), 0, 0)),
            pl.BlockSpec((E, D), lambda i: (0, 0)),
            pl.BlockSpec((E, 1), lambda i: (0, 0)),
        ],
        out_specs=[
            pl.BlockSpec((tm * SLAB, 128), lambda i: (i, 0)),
            pl.BlockSpec((TOP_K, tm), lambda i: (0, i)),
            pl.BlockSpec((TOP_K, tm), lambda i: (0, i)),
            pl.BlockSpec((TOP_K, tm), lambda i: (0, i)),
            pl.BlockSpec((E, 1), lambda i: (0, 0)),
        ],
        out_shape=[
            jax.ShapeDtypeStruct((N * SLAB, 128), U32),
            jax.ShapeDtypeStruct((TOP_K, N), I32),
            jax.ShapeDtypeStruct((TOP_K, N), F32),
            jax.ShapeDtypeStruct((TOP_K, N), I32),
            jax.ShapeDtypeStruct((E, 1), I32),
        ],
        scratch_shapes=[pltpu.VMEM((E, 1), F32)],
        compiler_params=_cparams(("arbitrary",)),
        name="moe_router",
    )(x, gain.reshape(1, D), mod, router_w_t, router_b.reshape(E, 1))


def _experts_kernel(be_ref, nused_ref, tok_cur, tok_nxt, h_hbm, w1_ref, w3_ref, w2_ref, y_ref,
                    xbuf, sems, w1c, w3c, w2c, prev_e):
    b = pl.program_id(0)
    e = be_ref[b]
    n_used = nused_ref[0]
    BLK = xbuf.shape[1] // SLAB
    slot = b % 2

    FF = w1c.shape[1]
    D = w2c.shape[1]
    CW = 256
    n_chunks = 2 * (FF // CW) + D // CW
    per_chunk = BLK // n_chunks + 1

    def gather(tok_ref, s, lo=0, hi=BLK):
        for r in range(lo, min(hi, BLK)):
            _slab_copy(h_hbm, xbuf.at[s], sems.at[s], tok_ref[0, r], r).start()

    def drain(s):
        for r in range(BLK):
            _slab_copy(h_hbm, xbuf.at[s], sems.at[s], 0, r).wait()

    @pl.when(b == 0)
    def _():
        prev_e[0] = -1
        gather(tok_cur, 0)

    @pl.when(b < n_used)
    def _():
        @pl.when(e != prev_e[0])
        def _():
            w1c[...] = w1_ref[...].astype(BF16)
            w3c[...] = w3_ref[...].astype(BF16)
            w2c[...] = w2_ref[...].astype(BF16)
            prev_e[0] = e

        drain(slot)
        chunk = [0]

        def gather_some():
            gather(tok_nxt, 1 - slot, chunk[0] * per_chunk, (chunk[0] + 1) * per_chunk)
            chunk[0] += 1

        x = _load_rows_bf16(xbuf.at[slot], BLK)
        mids = []
        for c in range(FF // CW):
            sl = slice(c * CW, (c + 1) * CW)
            gather_some()
            a = _dot(x, w1c[:, sl])
            gather_some()
            mids.append((_silu(a) * _dot(x, w3c[:, sl])).astype(BF16))
        mid = jnp.concatenate(mids, axis=1)
        half = D // 2
        for c in range(half // CW):
            gather_some()
            y_lo = _dot(mid, w2c[:, c * CW:(c + 1) * CW])
            gather_some()
            y_hi = _dot(mid, w2c[:, half + c * CW:half + (c + 1) * CW])
            words = _pack_rows(jnp.concatenate([y_lo, y_hi], axis=1))
            for jj in range(CW // 128):
                j = c * (CW // 128) + jj
                y_ref[pl.ds(j, BLK, stride=SLAB), :] = words[:, jj * 128:(jj + 1) * 128]

    @pl.when(b >= n_used)
    def _():
        @pl.when(b == n_used)
        def _():
            drain(slot)

        y_ref[...] = jnp.zeros_like(y_ref)


def _experts(block_expert, n_used, slot_tok8, h, w1, w3, w2, layer):
    n_blocks = block_expert.shape[0]
    BLK = MOE_BLOCK
    _, _, D, FF = w1.shape
    slot_tok3 = slot_tok8.reshape(n_blocks, 1, BLK)
    return pl.pallas_call(
        _experts_kernel,
        grid_spec=pltpu.PrefetchScalarGridSpec(
            num_scalar_prefetch=2,
            grid=(n_blocks,),
            in_specs=[
                pl.BlockSpec((None, 1, BLK), lambda b, be, nu: (b, 0, 0), memory_space=pltpu.SMEM),
                pl.BlockSpec((None, 1, BLK), lambda b, be, nu: (jnp.minimum(b + 1, n_blocks - 1), 0, 0),
                             memory_space=pltpu.SMEM),
                pl.BlockSpec(memory_space=pl.ANY),
                pl.BlockSpec((None, None, D, FF), lambda b, be, nu: (layer, be[b], 0, 0)),
                pl.BlockSpec((None, None, D, FF), lambda b, be, nu: (layer, be[b], 0, 0)),
                pl.BlockSpec((None, None, FF, D), lambda b, be, nu: (layer, be[b], 0, 0)),
            ],
            out_specs=pl.BlockSpec((BLK * SLAB, 128), lambda b, be, nu: (b, 0)),
            scratch_shapes=[
                pltpu.VMEM((2, BLK * SLAB, 128), U32),
                pltpu.SemaphoreType.DMA((2,)),
                pltpu.VMEM((D, FF), BF16),
                pltpu.VMEM((D, FF), BF16),
                pltpu.VMEM((FF, D), BF16),
                pltpu.SMEM((1,), I32),
            ],
        ),
        out_shape=jax.ShapeDtypeStruct((n_blocks * BLK * SLAB, 128), U32),
        compiler_params=_cparams(("arbitrary",)),
        name="moe_experts",
    )(block_expert, n_used, slot_tok3, slot_tok3, h, w1, w3, w2)


def _combine_kernel(dest_cur, dest_nxt, gates_ref, y_hbm, h_ref, x_ref, gate_ref, ws1_ref, ws3_ref, ws2_ref,
                    *rest, prompt_tiles):
    if prompt_tiles is None:
        o_ref, ybuf0, ybuf1, sems = rest
    else:
        op_ref, os_ref, ybuf0, ybuf1, sems = rest
    tm = x_ref.shape[0]
    i = pl.program_id(0)
    last = pl.num_programs(0) - 1

    def gather(dest_ref, buf, sem, lo=0, hi=tm):
        for k in range(TOP_K):
            for r in range(lo, hi):
                _slab_copy(y_hbm, buf.at[k], sem, dest_ref[k, r], r).start()

    def drain(buf, sem):
        for k in range(TOP_K):
            for r in range(tm):
                _slab_copy(y_hbm, buf.at[k], sem, 0, r).wait()

    @pl.when(i == 0)
    def _():
        gather(dest_cur, ybuf0, sems.at[0])

    def step(cur, cur_sem, nxt, nxt_sem):
        drain(cur, cur_sem)
        per_group = tm // (SLAB + 2)
        gather(dest_nxt, nxt, nxt_sem, 0, per_group)
        hb = _load_rows_bf16(h_ref, tm)
        mid = (_silu(_dot(hb, ws1_ref[...])) * _dot(hb, ws3_ref[...])).astype(BF16)
        gather(dest_nxt, nxt, nxt_sem, per_group, 2 * per_group)
        shared = _dot(mid, ws2_ref[...])
        g = [gates_ref[:, k:k + 1] for k in range(TOP_K)]
        lo_parts, hi_parts = [], []
        for j in range(SLAB):
            gather(dest_nxt, nxt, nxt_sem, (j + 2) * per_group, tm if j == SLAB - 1 else (j + 3) * per_group)
            lo_acc = hi_acc = None
            for k in range(TOP_K):
                lo, hi = _load_slab_halves(cur.at[k], tm, j)
                lo_acc = g[k] * lo if lo_acc is None else lo_acc + g[k] * lo
                hi_acc = g[k] * hi if hi_acc is None else hi_acc + g[k] * hi
            lo_parts.append(lo_acc)
            hi_parts.append(hi_acc)
        routed = jnp.concatenate(lo_parts + hi_parts, axis=1)
        out = x_ref[...] + gate_ref[...] * (routed + shared)
        if prompt_tiles is None:
            o_ref[...] = out
        else:
            @pl.when(i < prompt_tiles)
            def _():
                op_ref[...] = out

            @pl.when(i >= prompt_tiles)
            def _():
                os_ref[...] = out

        @pl.when(i == last)
        def _():
            drain(nxt, nxt_sem)

    @pl.when(i % 2 == 0)
    def _():
        step(ybuf0, sems.at[0], ybuf1, sems.at[1])

    @pl.when(i % 2 == 1)
    def _():
        step(ybuf1, sems.at[1], ybuf0, sems.at[0])


def _combine(dest, gates, y, h, x, mod, ws1, ws3, ws2, *, n_prompt, dec_seq, split_outputs):
    N, D = x.shape
    tm = COMBINE_TM
    FF = ws1.shape[1]
    assert N % tm == 0 and n_prompt % tm == 0 and dec_seq % tm == 0
    nt = N // tm
    row = functools.partial(_mod_row, tm=tm, n_prompt=n_prompt, dec_seq=dec_seq)
    dest3 = dest.reshape(TOP_K, nt, tm).transpose(1, 0, 2)
    if split_outputs:
        pt = n_prompt // tm
        out_specs = [pl.BlockSpec((tm, D), lambda i: (jnp.minimum(i, pt - 1), 0)),
                     pl.BlockSpec((tm, D), lambda i: (jnp.maximum(i - pt, 0), 0))]
        out_shape = [jax.ShapeDtypeStruct((n_prompt, D), F32), jax.ShapeDtypeStruct((N - n_prompt, D), F32)]
    else:
        pt = None
        out_specs = pl.BlockSpec((tm, D), lambda i: (i, 0))
        out_shape = jax.ShapeDtypeStruct((N, D), F32)
    return pl.pallas_call(
        functools.partial(_combine_kernel, prompt_tiles=pt),
        grid=(nt,),
        in_specs=[
            pl.BlockSpec((None, TOP_K, tm), lambda i: (i, 0, 0), memory_space=pltpu.SMEM),
            pl.BlockSpec((None, TOP_K, tm), lambda i: (jnp.minimum(i + 1, nt - 1), 0, 0), memory_space=pltpu.SMEM),
            pl.BlockSpec((tm, TOP_K), lambda i: (i, 0)),
            pl.BlockSpec(memory_space=pl.ANY),
            pl.BlockSpec((tm * SLAB, 128), lambda i: (i, 0)),
            pl.BlockSpec((tm, D), lambda i: (i, 0)),
            pl.BlockSpec((None, 1, D), lambda i: (row(i), 0, 5)),
            pl.BlockSpec((D, FF), lambda i: (0, 0)),
            pl.BlockSpec((D, FF), lambda i: (0, 0)),
            pl.BlockSpec((FF, D), lambda i: (0, 0)),
        ],
        out_specs=out_specs,
        out_shape=out_shape,
        scratch_shapes=[pltpu.VMEM((TOP_K, tm * SLAB, 128), U32), pltpu.VMEM((TOP_K, tm * SLAB, 128), U32),
                        pltpu.SemaphoreType.DMA((2,))],
        compiler_params=_cparams(("arbitrary",)),
        name="moe_combine",
    )(dest3, dest3, gates, y, h, x, mod, ws1, ws3, ws2)


def _moe_layer(x, gain, mod, router_w, router_b, w1, w3, w2, layer, ws1, ws3, ws2, *, n_prompt, dec_seq,
               split_outputs=False):
    N = x.shape[0]
    E, BLK = N_EXPERTS, MOE_BLOCK
    h, top_e, gates, rank, counts = _router(x, gain, mod, router_w.T, router_b,
                                            n_prompt=n_prompt, dec_seq=dec_seq,
                                            tm=_row_tile(512, n_prompt, dec_seq))
    counts = counts[:, 0]
    padded = (counts + BLK - 1) // BLK * BLK
    pad_end = jnp.cumsum(padded)
    pad_start = pad_end - padded
    n_blocks = N * TOP_K // BLK + E
    block_expert = jnp.minimum(
        jnp.sum(pad_end[None, :] <= (jnp.arange(n_blocks, dtype=I32) * BLK)[:, None], axis=1), E - 1).astype(I32)
    n_used = (pad_end[-1:] // BLK).astype(I32)
    onehot = top_e[None] == jnp.arange(E, dtype=I32)[:, None, None]
    dest = jnp.sum(jnp.where(onehot, pad_start[:, None, None], 0), axis=0) + rank
    keys = (top_e * N + jnp.arange(N, dtype=I32)[None, :]).reshape(-1)
    listed = jnp.concatenate([(lax.sort(keys) % N) * SLAB, jnp.zeros((BLK,), I32)])
    start = jnp.cumsum(counts) - counts
    blk0 = jnp.arange(n_blocks, dtype=I32) * BLK
    offset = jnp.clip(start[block_expert] + blk0 - pad_start[block_expert], 0, N * TOP_K)
    slot_tok8 = jax.vmap(lambda o: lax.dynamic_slice(listed, (o,), (BLK,)))(offset)
    y = _experts(block_expert, n_used, slot_tok8, h, w1, w3, w2, layer)
    return _combine(dest * SLAB, gates.T, y, h, x, mod, ws1, ws3, ws2, n_prompt=n_prompt, dec_seq=dec_seq,
                    split_outputs=split_outputs)


def _rope_tables(S):
    quarter = HEAD_DIM // 4
    pos = jnp.arange(S)
    row_id = (pos // GRID_W).astype(F32)
    col_id = (pos % GRID_W).astype(F32)
    inv = ROPE_THETA ** (-jnp.arange(quarter, dtype=F32) / quarter)
    ar, ac = row_id[:, None] * inv, col_id[:, None] * inv
    cos = jnp.concatenate([jnp.cos(ar), jnp.cos(ar), jnp.cos(ac), jnp.cos(ac)], axis=-1)
    sin = jnp.concatenate([-jnp.sin(ar), jnp.sin(ar), -jnp.sin(ac), jnp.sin(ac)], axis=-1)
    return cos, sin


def kernel(x_prompt, x_sample, state_mlstm_C, state_mlstm_n, state_mlstm_m, cache_attn_k, cache_attn_v,
           c, c_ctx, ada_w, ada_b, norm_mix, norm_ffn,
           mlstm_w_in, mlstm_gate_b, mlstm_head_g, mlstm_w_out,
           attn_w_qkv, attn_q_g, attn_k_g, attn_sink, attn_w_o,
           moe_router_w, moe_router_b, moe_w1, moe_w3, moe_w2, shared_w1, shared_w3, shared_w2):
    D = D_MODEL
    Bp, Sp, _ = x_prompt.shape
    Bs, Ss, _ = x_sample.shape
    n_prompt = Bp * Sp
    N = n_prompt + Bs * Ss
    dims = dict(n_prompt=n_prompt, dec_seq=Ss)
    tm512 = _row_tile(512, n_prompt, Ss)

    x = jnp.concatenate([x_prompt.reshape(n_prompt, D), x_sample.reshape(Bs * Ss, D)], axis=0)
    rows = 16
    cvec = jnp.zeros((rows, D), F32).at[0].set(c_ctx).at[1:1 + Bs].set(c)
    mod_all = _ada_table(cvec, ada_w, ada_b)[:, :1 + Bs].reshape(DEPTH, 1 + Bs, 1, 6 * D)

    H = MLSTM_HEADS
    mod = mod_all[0]
    w_in = mlstm_w_in[0]
    w_main = w_in[:, :MLSTM_MAIN].astype(BF16)
    w_gate = jnp.pad(w_in[:, MLSTM_MAIN:], ((0, 0), (0, GATE_PAD - 4 * H))).astype(BF16)
    gate_b = jnp.pad(mlstm_gate_b[0], (0, GATE_PAD - 4 * H))
    col_scale = jnp.concatenate([jnp.full((MLSTM_QK,), MLSTM_DK ** -0.5, F32),
                                 jnp.ones((MLSTM_MAIN - MLSTM_QK,), F32)])
    qkvo, gates = _nm_matmul(x, norm_mix[0], mod, w_main, col_scale, jnp.zeros((MLSTM_MAIN,), F32),
                             shift_idx=0, scale_idx=1, out_dtype=BF16, tm=tm512, tn=1024,
                             aux_w=w_gate, aux_b=gate_b, name="mlstm_proj", **dims)
    L = 256
    state = (state_mlstm_C[:, 0], state_mlstm_n[:, 0], state_mlstm_m[:, 0])
    hf, hb, C_p, n_p, m_p = _mlstm_scan(qkvo, gates, Bp, Sp, Bs, Ss, L, state)
    x = _mm_residual((hf, hb, qkvo, mlstm_head_g[0]), mlstm_w_out[0].astype(BF16), x, mod, gate_idx=2,
                     tm=tm512, tn=1024, mlstm_prologue=True, name="mlstm_out", **dims)
    x = _moe_layer(x, norm_ffn[0], mod, moe_router_w[0], moe_router_b[0], moe_w1, moe_w3, moe_w2, 0,
                   shared_w1[0].astype(BF16), shared_w3[0].astype(BF16), shared_w2[0].astype(BF16), **dims)

    mod = mod_all[1]
    qkv = _nm_matmul(x, norm_mix[1], mod, attn_w_qkv[0].astype(BF16), jnp.ones((ATTN_PROJ,), F32),
                     jnp.zeros((ATTN_PROJ,), F32), shift_idx=0, scale_idx=1, out_dtype=F32, tm=tm512, tn=1024,
                     name="attn_qkv", **dims)
    ident_cos = jnp.ones((Sp, HEAD_DIM), F32)
    ident_sin = jnp.zeros((Sp, HEAD_DIM), F32)
    q_p, k_p, v_p, kf_p, vf_p = _qk_prep(qkv, attn_q_g[0], attn_k_g[0], ident_cos, ident_sin, 0, n_prompt, Sp,
                                         True, "qk_prep_prompt")
    cos, sin = _rope_tables(Ss)
    q_s, k_s, v_s = _qk_prep(qkv, attn_q_g[0], attn_k_g[0], cos, sin, n_prompt, Bs * Ss, 256, False,
                             "qk_prep_latent")
    o_p = _attn_context(attn_sink[0], q_p, k_p, v_p, Bp, Sp)
    P = cache_attn_k.shape[2]
    ctx_k = cache_attn_k[:, 0].reshape(Bs, P, ATTN_KV).astype(BF16)
    ctx_v = cache_attn_v[:, 0].reshape(Bs, P, ATTN_KV).astype(BF16)
    o_s = _attn_latent(attn_sink[0], q_s, k_s, v_s, ctx_k, ctx_v, Bs, Ss)
    x = _mm_residual((o_p, o_s), attn_w_o[0].astype(BF16), x, mod, gate_idx=2,
                     tm=_row_tile(1024, n_prompt, Ss), tn=1024,
                     mlstm_prologue=False, name="attn_out", **dims)
    y_prompt, y_sample = _moe_layer(
        x, norm_ffn[1], mod, moe_router_w[1], moe_router_b[1], moe_w1, moe_w3, moe_w2, 1,
        shared_w1[1].astype(BF16), shared_w3[1].astype(BF16), shared_w2[1].astype(BF16), split_outputs=True, **dims)
    return (y_prompt.reshape(Bp, Sp, D), y_sample.reshape(Bs, Ss, D), C_p[:, None], n_p[:, None], m_p[:, None],
            kf_p.reshape(Bp, 1, Sp, ATTN_KV_HEADS, HEAD_DIM), vf_p.reshape(Bp, 1, Sp, ATTN_KV_HEADS, HEAD_DIM))
```

```python
import functools

import jax
import jax.numpy as jnp
from jax import lax
from jax.experimental import pallas as pl
from jax.experimental.pallas import tpu as pltpu

F32 = jnp.float32
BF16 = jnp.bfloat16
I32 = jnp.int32

D_MODEL = 2048
DEPTH = 2
EPS = 1e-6
GRID_W = 64
MLSTM_HEADS = 8
MLSTM_DK = 128
MLSTM_DV = 256
MLSTM_QK = MLSTM_HEADS * MLSTM_DK
MLSTM_V = MLSTM_HEADS * MLSTM_DV
MLSTM_MAIN = 2 * MLSTM_QK + 2 * MLSTM_V
GATE_PAD = 128
HEAD_DIM = 128
ATTN_Q_HEADS = 16
ATTN_KV_HEADS = 4
ATTN_GROUPS = 4
WINDOW = 128
QBLK = 128
ROPE_THETA = 10000.0
ATTN_Q = ATTN_Q_HEADS * HEAD_DIM
ATTN_KV = ATTN_KV_HEADS * HEAD_DIM
ATTN_PROJ = ATTN_Q + 2 * ATTN_KV
N_EXPERTS = 64
TOP_K = 8
N_EXPERT_GROUPS = 8
TOPK_GROUPS = 4
GROUP_SIZE = N_EXPERTS // N_EXPERT_GROUPS
EXPERT_FF = 512
ROUTED_SCALE = 2.5
MOE_BLOCK = 512
COMBINE_TM = 128

V7X_VMEM_LIMIT = 56 * 1024 * 1024
NEG_INF = float("-inf")


def _cparams(sem):
    return pltpu.CompilerParams(dimension_semantics=("arbitrary",) * len(sem), vmem_limit_bytes=V7X_VMEM_LIMIT)


def _split_hi_lo(a):
    hi = a.astype(BF16)
    lo = (a - hi.astype(F32)).astype(BF16)
    return hi, lo


def _dot(a, b):
    return jnp.dot(a, b, preferred_element_type=F32)


def _dot_nt(a, b):
    return lax.dot_general(a, b, (((1,), (1,)), ((), ())), preferred_element_type=F32)


def _dot_tn(a, b):
    return lax.dot_general(a, b, (((0,), (0,)), ((), ())), preferred_element_type=F32)


def _silu(x):
    return x * jax.nn.sigmoid(x)


def _row_tile(preferred, n_prompt, dec_seq):
    tm = preferred
    while n_prompt % tm or dec_seq % tm:
        tm //= 2
    return tm


def _mod_row(i, tm, n_prompt, dec_seq):
    r0 = i * tm
    return jnp.where(r0 < n_prompt, 0, 1 + (r0 - n_prompt) // dec_seq)


def _norm_modulate(x, gain, mod_ref, shift_idx, scale_idx):
    D = D_MODEL
    y = x * lax.rsqrt(jnp.mean(x * x, axis=-1, keepdims=True) + EPS) * gain
    shift = mod_ref[:, shift_idx * D:(shift_idx + 1) * D]
    scale = mod_ref[:, scale_idx * D:(scale_idx + 1) * D]
    return y * (1.0 + scale) + shift


def _ada_kernel(c_ref, w_ref, b_ref, o_ref):
    s = _silu(c_ref[...])
    s_hi, s_lo = _split_hi_lo(s)
    w_hi, w_lo = _split_hi_lo(w_ref[...])
    o_ref[...] = _dot(s_hi, w_hi) + _dot(s_hi, w_lo) + _dot(s_lo, w_hi) + b_ref[...]


def _ada_table(cvec, ada_w, ada_b):
    D = D_MODEL
    tn = 1024
    rows = cvec.shape[0]
    return pl.pallas_call(
        _ada_kernel,
        grid=(DEPTH, 6 * D // tn),
        in_specs=[
            pl.BlockSpec((rows, D), lambda l, j: (0, 0)),
            pl.BlockSpec((None, D, tn), lambda l, j: (l, 0, j)),
            pl.BlockSpec((None, 1, tn), lambda l, j: (l, 0, j)),
        ],
        out_specs=pl.BlockSpec((None, rows, tn), lambda l, j: (l, 0, j)),
        out_shape=jax.ShapeDtypeStruct((DEPTH, rows, 6 * D), F32),
        compiler_params=_cparams(("parallel", "parallel")),
        name="ada_table",
    )(cvec, ada_w, ada_b.reshape(DEPTH, 1, 6 * D))


def _nm_matmul_kernel(x_ref, gain_ref, mod_ref, w_ref, cs_ref, cb_ref, *rest, shift_idx, scale_idx, has_aux):
    if has_aux:
        wa_ref, ab_ref, o_ref, aux_ref, h_scr = rest
    else:
        o_ref, h_scr = rest

    @pl.when(pl.program_id(1) == 0)
    def _():
        h = _norm_modulate(x_ref[...], gain_ref[...], mod_ref, shift_idx, scale_idx).astype(BF16)
        h_scr[...] = h
        if has_aux:
            aux_ref[...] = _dot(h, wa_ref[...]) + ab_ref[...]

    acc = _dot(h_scr[...], w_ref[...])
    o_ref[...] = (acc * cs_ref[...] + cb_ref[...]).astype(o_ref.dtype)


def _nm_matmul(x, gain, mod, w, col_scale, col_bias, *, shift_idx, scale_idx, n_prompt, dec_seq,
               out_dtype, tm, tn, aux_w=None, aux_b=None, name):
    N, D = x.shape
    P = w.shape[1]
    assert N % tm == 0 and P % tn == 0 and n_prompt % tm == 0 and dec_seq % tm == 0
    has_aux = aux_w is not None
    row = functools.partial(_mod_row, tm=tm, n_prompt=n_prompt, dec_seq=dec_seq)
    in_specs = [
        pl.BlockSpec((tm, D), lambda i, j: (i, 0)),
        pl.BlockSpec((1, D), lambda i, j: (0, 0)),
        pl.BlockSpec((None, 1, 6 * D), lambda i, j: (row(i), 0, 0)),
        pl.BlockSpec((D, tn), lambda i, j: (0, j)),
        pl.BlockSpec((1, tn), lambda i, j: (0, j)),
        pl.BlockSpec((1, tn), lambda i, j: (0, j)),
    ]
    args = [x, gain.reshape(1, D), mod, w, col_scale.reshape(1, P), col_bias.reshape(1, P)]
    out_specs = pl.BlockSpec((tm, tn), lambda i, j: (i, j))
    out_shape = jax.ShapeDtypeStruct((N, P), out_dtype)
    if has_aux:
        PA = aux_w.shape[1]
        in_specs += [pl.BlockSpec((D, PA), lambda i, j: (0, 0)), pl.BlockSpec((1, PA), lambda i, j: (0, 0))]
        args += [aux_w, aux_b.reshape(1, PA)]
        out_specs = [out_specs, pl.BlockSpec((tm, PA), lambda i, j: (i, 0))]
        out_shape = [out_shape, jax.ShapeDtypeStruct((N, PA), F32)]
    return pl.pallas_call(
        functools.partial(_nm_matmul_kernel, shift_idx=shift_idx, scale_idx=scale_idx, has_aux=has_aux),
        grid=(N // tm, P // tn),
        in_specs=in_specs,
        out_specs=out_specs,
        out_shape=out_shape,
        scratch_shapes=[pltpu.VMEM((tm, D), BF16)],
        compiler_params=_cparams(("parallel", "arbitrary")),
        name=name,
    )(*args)


def _mm_residual_kernel(*refs, mlstm_prologue, prompt_tiles):
    if mlstm_prologue:
        hf_ref, hb_ref, og_ref, hg_ref, w_ref, x_ref, gate_ref, o_ref, l_scr = refs

        @pl.when(pl.program_id(1) == 0)
        def _():
            hs = hf_ref[...] + hb_ref[...]
            og = jax.nn.sigmoid(og_ref[...].astype(F32))
            for h in range(MLSTM_HEADS):
                sl = slice(h * MLSTM_DV, (h + 1) * MLSTM_DV)
                hh = hs[:, sl]
                hn = hh * lax.rsqrt(jnp.mean(hh * hh, axis=-1, keepdims=True) + EPS) * hg_ref[:, sl]
                l_scr[:, sl] = (og[:, sl] * hn).astype(BF16)

        o_ref[...] = x_ref[...] + gate_ref[...] * _dot(l_scr[...], w_ref[...])
    else:
        lp_ref, ls_ref, w_ref, x_ref, gate_ref, o_ref = refs
        context = pl.program_id(0) < prompt_tiles

        @pl.when(context)
        def _():
            o_ref[...] = x_ref[...] + gate_ref[...] * _dot(lp_ref[...], w_ref[...])

        @pl.when(jnp.logical_not(context))
        def _():
            o_ref[...] = x_ref[...] + gate_ref[...] * _dot(ls_ref[...], w_ref[...])


def _mm_residual(lhs_args, w, x, mod, *, gate_idx, n_prompt, dec_seq, tm, tn, mlstm_prologue, name):
    N, D = x.shape
    K = w.shape[0]
    assert N % tm == 0 and D % tn == 0 and n_prompt % tm == 0 and dec_seq % tm == 0
    row = functools.partial(_mod_row, tm=tm, n_prompt=n_prompt, dec_seq=dec_seq)
    pt = n_prompt // tm
    if mlstm_prologue:
        hf, hb, qkvo, head_g = lhs_args
        o_blk = (2 * MLSTM_QK + MLSTM_V) // MLSTM_V
        in_specs = [
            pl.BlockSpec((tm, K), lambda i, j: (i, 0)),
            pl.BlockSpec((tm, K), lambda i, j: (i, 0)),
            pl.BlockSpec((tm, MLSTM_V), lambda i, j: (i, o_blk)),
            pl.BlockSpec((1, K), lambda i, j: (0, 0)),
        ]
        args = [hf, hb, qkvo, head_g.reshape(1, K)]
        scratch = [pltpu.VMEM((tm, K), BF16)]
    else:
        lhs_p, lhs_s = lhs_args
        in_specs = [pl.BlockSpec((tm, K), lambda i, j: (jnp.minimum(i, pt - 1), 0)),
                    pl.BlockSpec((tm, K), lambda i, j: (jnp.maximum(i - pt, 0), 0))]
        args = [lhs_p, lhs_s]
        scratch = []
    in_specs += [
        pl.BlockSpec((K, tn), lambda i, j: (0, j)),
        pl.BlockSpec((tm, tn), lambda i, j: (i, j)),
        pl.BlockSpec((None, 1, tn), lambda i, j: (row(i), 0, gate_idx * (D // tn) + j)),
    ]
    args += [w, x, mod]
    return pl.pallas_call(
        functools.partial(_mm_residual_kernel, mlstm_prologue=mlstm_prologue, prompt_tiles=pt),
        grid=(N // tm, D // tn),
        in_specs=in_specs,
        out_specs=pl.BlockSpec((tm, tn), lambda i, j: (i, j)),
        out_shape=jax.ShapeDtypeStruct((N, D), F32),
        scratch_shapes=scratch,
        compiler_params=_cparams(("parallel", "arbitrary")),
        name=name,
    )(*args)


def _log_sigmoid(x):
    return jnp.minimum(x, 0.0) - jnp.log(1.0 + jnp.exp(-jnp.abs(x)))


def _mlstm_direction(d, q_ref, k_ref, v_ref, g_ref, h_ref, C_scr, n_scr, m_scr, L):
    H, DK, DV = MLSTM_HEADS, MLSTM_DK, MLSTM_DV
    g = g_ref[...]
    lf = _log_sigmoid(g)
    r = lax.broadcasted_iota(I32, (L, L), 0)
    c = lax.broadcasted_iota(I32, (L, L), 1)
    causal = (c <= r) if d == 0 else (c >= r)
    tri = jnp.where(causal, 1.0, 0.0).astype(BF16)
    lf1 = lf.astype(BF16)
    rem = lf - lf1.astype(F32)
    lf2 = rem.astype(BF16)
    lf3 = (rem - lf2.astype(F32)).astype(BF16)
    bsum = _dot(tri, lf1) + _dot(tri, lf2) + _dot(tri, lf3)
    g_t = g.T
    b_t = bsum.T
    end = L - 1 if d == 0 else 0
    for h in range(H):
        ci, cf = d * 2 * H + h, d * 2 * H + H + h
        b_col, i_col = bsum[:, cf:cf + 1], g[:, ci:ci + 1]
        b_row, i_row = b_t[cf:cf + 1, :], g_t[ci:ci + 1, :]
        b_end = bsum[end:end + 1, cf:cf + 1]
        m = m_scr[d, h]
        C = C_scr[d, h]
        n = n_scr[d, h]
        qh = q_ref[:, h * DK:(h + 1) * DK]
        kh = k_ref[:, h * DK:(h + 1) * DK]
        vh = v_ref[:, h * DV:(h + 1) * DV]
        dmat = jnp.where(causal, b_col - b_row + i_row, NEG_INF)
        inter = b_col + m
        m_out = jnp.maximum(inter, jnp.max(dmat, axis=-1, keepdims=True))
        w = jnp.exp(dmat - m_out) * _dot_nt(qh, kh)
        dec = jnp.exp(inter - m_out)
        num = _dot(w.astype(BF16), vh) + dec * _dot(qh, C.astype(BF16))
        den = jnp.sum(w, axis=-1, keepdims=True) + dec * jnp.sum(qh.astype(F32) * n, axis=-1, keepdims=True)
        h_ref[:, h * DV:(h + 1) * DV] = num / jnp.maximum(jnp.abs(den), jnp.exp(-m_out))
        to_end = b_end - b_col + i_col
        m_new = jnp.maximum(b_end + m, jnp.max(to_end, axis=0, keepdims=True))
        wk = jnp.exp(to_end - m_new)
        dec_end = jnp.exp(b_end + m - m_new)
        kw = kh.astype(F32) * wk
        C_scr[d, h] = dec_end * C + _dot_tn(kw.astype(BF16), vh)
        n_scr[d, h] = dec_end * n + jnp.sum(kw, axis=0, keepdims=True)
        m_scr[d, h] = m_new


def _mlstm_kernel(fwd_ref, bwd_ref, bidx_ref, flag_ref, qf, kf, vf, gf, qb, kb, vb, gb, C0, n0, m0,
                  hf, hb, Co, no, mo, C_scr, n_scr, m_scr, *, L):
    H = MLSTM_HEADS
    flags = flag_ref[pl.program_id(0)]
    first, last, context = (flags & 1) != 0, (flags & 2) != 0, (flags & 4) != 0

    @pl.when(first & context)
    def _():
        C_scr[...] = jnp.zeros_like(C_scr)
        n_scr[...] = jnp.zeros_like(n_scr)
        m_scr[...] = jnp.zeros_like(m_scr)

    @pl.when(first & jnp.logical_not(context))
    def _():
        C_scr[...] = C0[...]
        for d in range(2):
            for h in range(H):
                n_scr[d, h] = n0[d, h:h + 1, :]
                m_scr[d, h] = m0[d:d + 1, h:h + 1]

    _mlstm_direction(0, qf, kf, vf, gf, hf, C_scr, n_scr, m_scr, L)
    _mlstm_direction(1, qb, kb, vb, gb, hb, C_scr, n_scr, m_scr, L)

    @pl.when(last & context)
    def _():
        Co[...] = C_scr[...]
        for d in range(2):
            for h in range(H):
                no[d, h:h + 1, :] = n_scr[d, h]
                mo[d:d + 1, h:h + 1] = m_scr[d, h]


def _mlstm_scan(qkvo, gates, Bp, Sp, Bs, Ss, L, state):
    H, DK, DV = MLSTM_HEADS, MLSTM_DK, MLSTM_DV
    assert Sp % L == 0 and Ss % L == 0
    fwd, bwd, bidx, flags = [], [], [], []
    for context, B, S, base in ((1, Bp, Sp, 0), (0, Bs, Ss, Bp * Sp // L)):
        nc = S // L
        for b in range(B):
            for c in range(nc):
                fwd.append(base + b * nc + c)
                bwd.append(base + b * nc + nc - 1 - c)
                bidx.append(b)
                flags.append((c == 0) * 1 + (c == nc - 1) * 2 + context * 4)
    tables = [jnp.asarray(t, I32) for t in (fwd, bwd, bidx, flags)]
    lat_b = lambda s, fl, bi: jnp.where((fl[s] & 4) != 0, 0, bi[s])
    ctx_b = lambda s, fl, bi: jnp.where((fl[s] & 4) != 0, bi[s], Bp - 1)

    def specs(tbl):
        return [
            pl.BlockSpec((L, MLSTM_QK), lambda s, f, w, bi, fl: ((f, w)[tbl][s], 0)),
            pl.BlockSpec((L, MLSTM_QK), lambda s, f, w, bi, fl: ((f, w)[tbl][s], 1)),
            pl.BlockSpec((L, MLSTM_V), lambda s, f, w, bi, fl: ((f, w)[tbl][s], 1)),
            pl.BlockSpec((L, GATE_PAD), lambda s, f, w, bi, fl: ((f, w)[tbl][s], 0)),
        ]

    def state_specs(which):
        return [
            pl.BlockSpec((None, 2, H, DK, DV), lambda s, f, w, bi, fl: (which(s, fl, bi), 0, 0, 0, 0)),
            pl.BlockSpec((None, 2, H, DK), lambda s, f, w, bi, fl: (which(s, fl, bi), 0, 0, 0)),
            pl.BlockSpec((None, 2, H), lambda s, f, w, bi, fl: (which(s, fl, bi), 0, 0)),
        ]

    N = qkvo.shape[0]
    return pl.pallas_call(
        functools.partial(_mlstm_kernel, L=L),
        grid_spec=pltpu.PrefetchScalarGridSpec(
            num_scalar_prefetch=4,
            grid=(len(fwd),),
            in_specs=specs(0) + specs(1) + state_specs(lat_b),
            out_specs=[
                pl.BlockSpec((L, MLSTM_V), lambda s, f, w, bi, fl: (f[s], 0)),
                pl.BlockSpec((L, MLSTM_V), lambda s, f, w, bi, fl: (w[s], 0)),
            ] + state_specs(ctx_b),
            scratch_shapes=[
                pltpu.VMEM((2, H, DK, DV), F32),
                pltpu.VMEM((2, H, 1, DK), F32),
                pltpu.VMEM((2, H, 1, 1), F32),
            ],
        ),
        out_shape=[
            jax.ShapeDtypeStruct((N, MLSTM_V), F32),
            jax.ShapeDtypeStruct((N, MLSTM_V), F32),
            jax.ShapeDtypeStruct((Bp, 2, H, DK, DV), F32),
            jax.ShapeDtypeStruct((Bp, 2, H, DK), F32),
            jax.ShapeDtypeStruct((Bp, 2, H), F32),
        ],
        compiler_params=_cparams(("arbitrary",)),
        name="mlstm_scan",
    )(*tables, qkvo, qkvo, qkvo, gates, qkvo, qkvo, qkvo, gates, *state)


def _qk_prep_kernel(qkv_ref, qg_ref, kg_ref, cos_ref, sin_ref, *outs, emit_f32):
    if emit_f32:
        q_ref, k_ref, v_ref, kf_ref, vf_ref = outs
    else:
        q_ref, k_ref, v_ref = outs
    cos = cos_ref[...]
    sin = sin_ref[...]
    lane = lax.broadcasted_iota(I32, cos.shape, 1)
    first = (lane % (HEAD_DIM // 2)) < (HEAD_DIM // 4)

    def norm_rope(x, gain):
        xn = x * lax.rsqrt(jnp.mean(x * x, axis=-1, keepdims=True) + EPS) * gain
        partner = jnp.where(first, pltpu.roll(xn, HEAD_DIM - HEAD_DIM // 4, 1), pltpu.roll(xn, HEAD_DIM // 4, 1))
        return xn * cos + partner * sin

    for h in range(ATTN_Q_HEADS):
        sl = slice(h * HEAD_DIM, (h + 1) * HEAD_DIM)
        q_ref[:, sl] = (norm_rope(qkv_ref[:, sl], qg_ref[...]) * HEAD_DIM ** -0.5).astype(BF16)
    for h in range(ATTN_KV_HEADS):
        sl = slice(h * HEAD_DIM, (h + 1) * HEAD_DIM)
        kx = norm_rope(qkv_ref[:, ATTN_Q + h * HEAD_DIM:ATTN_Q + (h + 1) * HEAD_DIM], kg_ref[...])
        k_ref[:, sl] = kx.astype(BF16)
        if emit_f32:
            kf_ref[:, sl] = kx
    vx = qkv_ref[:, ATTN_Q + ATTN_KV:]
    v_ref[...] = vx.astype(BF16)
    if emit_f32:
        vf_ref[...] = vx


def _qk_prep(qkv, q_g, k_g, cos, sin, row0, rows, tm, emit_f32, name):
    assert rows % tm == 0 and row0 % tm == 0 and cos.shape[0] % tm == 0
    base = row0 // tm
    nt = cos.shape[0] // tm
    out_specs = [
        pl.BlockSpec((tm, ATTN_Q), lambda i: (i, 0)),
        pl.BlockSpec((tm, ATTN_KV), lambda i: (i, 0)),
        pl.BlockSpec((tm, ATTN_KV), lambda i: (i, 0)),
    ]
    out_shape = [
        jax.ShapeDtypeStruct((rows, ATTN_Q), BF16),
        jax.ShapeDtypeStruct((rows, ATTN_KV), BF16),
        jax.ShapeDtypeStruct((rows, ATTN_KV), BF16),
    ]
    if emit_f32:
        out_specs += [pl.BlockSpec((tm, ATTN_KV), lambda i: (i, 0))] * 2
        out_shape += [jax.ShapeDtypeStruct((rows, ATTN_KV), F32)] * 2
    return pl.pallas_call(
        functools.partial(_qk_prep_kernel, emit_f32=emit_f32),
        grid=(rows // tm,),
        in_specs=[
            pl.BlockSpec((tm, ATTN_PROJ), lambda i: (base + i, 0)),
            pl.BlockSpec((1, HEAD_DIM), lambda i: (0, 0)),
            pl.BlockSpec((1, HEAD_DIM), lambda i: (0, 0)),
            pl.BlockSpec((tm, HEAD_DIM), lambda i: (i % nt, 0)),
            pl.BlockSpec((tm, HEAD_DIM), lambda i: (i % nt, 0)),
        ],
        out_specs=out_specs,
        out_shape=out_shape,
        compiler_params=_cparams(("parallel",)),
        name=name,
    )(qkv, q_g.reshape(1, HEAD_DIM), k_g.reshape(1, HEAD_DIM), cos, sin)


def _sink_column(sink_ref, kv, rows_per_head):
    parts = [jnp.full((rows_per_head, 1), sink_ref[kv * ATTN_GROUPS + g], F32) for g in range(ATTN_GROUPS)]
    return jnp.concatenate(parts, axis=0)


def _attn_ctx_kernel(sink_ref, q_ref, k_ref, v_ref, o_ref):
    S = q_ref.shape[0]
    for kv in range(ATTN_KV_HEADS):
        q = jnp.concatenate(
            [q_ref[:, (kv * ATTN_GROUPS + g) * HEAD_DIM:(kv * ATTN_GROUPS + g + 1) * HEAD_DIM]
             for g in range(ATTN_GROUPS)], axis=0)
        ksl = slice(kv * HEAD_DIM, (kv + 1) * HEAD_DIM)
        s = _dot_nt(q, k_ref[:, ksl])
        sk = _sink_column(sink_ref, kv, S)
        m = jnp.maximum(jnp.max(s, axis=-1, keepdims=True), sk)
        p = jnp.exp(s - m)
        den = jnp.sum(p, axis=-1, keepdims=True) + jnp.exp(sk - m)
        o = _dot(p.astype(BF16), v_ref[:, ksl]) * (1.0 / den)
        for g in range(ATTN_GROUPS):
            hq = kv * ATTN_GROUPS + g
            o_ref[:, hq * HEAD_DIM:(hq + 1) * HEAD_DIM] = o[g * S:(g + 1) * S].astype(BF16)


def _attn_context(sink, q, k, v, B, S):
    return pl.pallas_call(
        _attn_ctx_kernel,
        grid_spec=pltpu.PrefetchScalarGridSpec(
            num_scalar_prefetch=0,
            grid=(B,),
            in_specs=[
                pl.BlockSpec(memory_space=pltpu.SMEM),
                pl.BlockSpec((S, ATTN_Q), lambda b: (b, 0)),
                pl.BlockSpec((S, ATTN_KV), lambda b: (b, 0)),
                pl.BlockSpec((S, ATTN_KV), lambda b: (b, 0)),
            ],
            out_specs=pl.BlockSpec((S, ATTN_Q), lambda b: (b, 0)),
        ),
        out_shape=jax.ShapeDtypeStruct((B * S, ATTN_Q), BF16),
        compiler_params=_cparams(("parallel",)),
        name="attn_context",
    )(sink, q, k, v)


def _attn_lat_kernel(sink_ref, q_ref, kp_ref, kc_ref, kn_ref, vp_ref, vc_ref, vn_ref, kx_ref, vx_ref, o_ref):
    i = pl.program_id(1)
    nb = pl.num_programs(1)
    R = ATTN_GROUPS * QBLK
    r = lax.broadcasted_iota(I32, (R, QBLK), 0) % QBLK
    c = lax.broadcasted_iota(I32, (R, QBLK), 1)
    ok_prev = c >= r
    ok_next = c <= r
    edge_prev = jnp.where(i > 0, 0.0, NEG_INF)
    edge_next = jnp.where(i < nb - 1, 0.0, NEG_INF)
    for kv in range(ATTN_KV_HEADS):
        q = jnp.concatenate(
            [q_ref[:, (kv * ATTN_GROUPS + g) * HEAD_DIM:(kv * ATTN_GROUPS + g + 1) * HEAD_DIM]
             for g in range(ATTN_GROUPS)], axis=0)
        ksl = slice(kv * HEAD_DIM, (kv + 1) * HEAD_DIM)
        s_p = jnp.where(ok_prev, _dot_nt(q, kp_ref[:, ksl]) + edge_prev, NEG_INF)
        s_c = _dot_nt(q, kc_ref[:, ksl])
        s_n = jnp.where(ok_next, _dot_nt(q, kn_ref[:, ksl]) + edge_next, NEG_INF)
        s_x = _dot_nt(q, kx_ref[:, ksl])
        sk = _sink_column(sink_ref, kv, QBLK)
        m = jnp.maximum(
            jnp.maximum(jnp.max(s_p, axis=-1, keepdims=True), jnp.max(s_c, axis=-1, keepdims=True)),
            jnp.maximum(jnp.max(s_n, axis=-1, keepdims=True), jnp.max(s_x, axis=-1, keepdims=True)))
        m = jnp.maximum(m, sk)
        p_p, p_c, p_n, p_x = jnp.exp(s_p - m), jnp.exp(s_c - m), jnp.exp(s_n - m), jnp.exp(s_x - m)
        den = (jnp.sum(p_p, axis=-1, keepdims=True) + jnp.sum(p_c, axis=-1, keepdims=True)
               + jnp.sum(p_n, axis=-1, keepdims=True) + jnp.sum(p_x, axis=-1, keepdims=True) + jnp.exp(sk - m))
        o = (_dot(p_p.astype(BF16), vp_ref[:, ksl]) + _dot(p_c.astype(BF16), vc_ref[:, ksl])
             + _dot(p_n.astype(BF16), vn_ref[:, ksl]) + _dot(p_x.astype(BF16), vx_ref[:, ksl])) * (1.0 / den)
        for g in range(ATTN_GROUPS):
            hq = kv * ATTN_GROUPS + g
            o_ref[:, hq * HEAD_DIM:(hq + 1) * HEAD_DIM] = o[g * QBLK:(g + 1) * QBLK].astype(BF16)


def _attn_latent(sink, q, k, v, ctx_k, ctx_v, B, S):
    nb = S // QBLK
    P = ctx_k.shape[1]
    prev = lambda b, i: (b * nb + jnp.maximum(i - 1, 0), 0)
    cur = lambda b, i: (b * nb + i, 0)
    nxt = lambda b, i: (b * nb + jnp.minimum(i + 1, nb - 1), 0)
    kv_spec = lambda f: pl.BlockSpec((QBLK, ATTN_KV), f)
    return pl.pallas_call(
        _attn_lat_kernel,
        grid_spec=pltpu.PrefetchScalarGridSpec(
            num_scalar_prefetch=0,
            grid=(B, nb),
            in_specs=[
                pl.BlockSpec(memory_space=pltpu.SMEM),
                pl.BlockSpec((QBLK, ATTN_Q), cur),
                kv_spec(prev), kv_spec(cur), kv_spec(nxt),
                kv_spec(prev), kv_spec(cur), kv_spec(nxt),
                pl.BlockSpec((None, P, ATTN_KV), lambda b, i: (b, 0, 0)),
                pl.BlockSpec((None, P, ATTN_KV), lambda b, i: (b, 0, 0)),
            ],
            out_specs=pl.BlockSpec((QBLK, ATTN_Q), cur),
        ),
        out_shape=jax.ShapeDtypeStruct((B * S, ATTN_Q), BF16),
        compiler_params=_cparams(("parallel", "parallel")),
        name="attn_latent",
    )(sink, q, k, k, k, v, v, v, ctx_k, ctx_v)


SLAB = 8
U32 = jnp.uint32


def _pack_rows(x):
    half = x.shape[1] // 2
    bits = lax.bitcast_convert_type(x.astype(BF16).astype(F32), U32)
    return (bits[:, :half] >> 16) | (bits[:, half:] & jnp.uint32(0xFFFF0000))


def _store_slabs(ref, words):
    R = words.shape[0]
    for j in range(SLAB):
        ref[pl.ds(j, R, stride=SLAB), :] = words[:, j * 128:(j + 1) * 128]


def _load_slab_halves(ref, R, j):
    w = ref[pl.ds(j, R, stride=SLAB), :]
    return lax.bitcast_convert_type(w << 16, F32), lax.bitcast_convert_type(w & jnp.uint32(0xFFFF0000), F32)


def _load_rows_bf16(ref, R):
    halves = [_load_slab_halves(ref, R, j) for j in range(SLAB)]
    return jnp.concatenate([lo for lo, _ in halves] + [hi for _, hi in halves], axis=1).astype(BF16)


def _slab_copy(src, dst, sem, src_row8, dst_row):
    return pltpu.make_async_copy(src.at[pl.ds(pl.multiple_of(src_row8, SLAB), SLAB), :],
                                 dst.at[pl.ds(dst_row * SLAB, SLAB), :], sem)


def _router_kernel(x_ref, gain_ref, mod_ref, wr_ref, rb_ref, h_ref, e_ref, g_ref, r_ref, cnt_ref, cnt_scr, *, tm):
    E, GS = N_EXPERTS, GROUP_SIZE
    i = pl.program_id(0)

    @pl.when(i == 0)
    def _():
        cnt_scr[...] = jnp.zeros_like(cnt_scr)

    h = _norm_modulate(x_ref[...], gain_ref[...], mod_ref, 3, 4)
    _store_slabs(h_ref, _pack_rows(h))
    h_hi, h_lo = _split_hi_lo(h)
    w_hi, w_lo = _split_hi_lo(wr_ref[...])
    logits = _dot_nt(w_hi, h_hi) + _dot_nt(w_lo, h_hi) + _dot_nt(w_hi, h_lo)
    scores = jax.nn.sigmoid(logits)
    biased = scores + rb_ref[...]
    sub = lax.broadcasted_iota(I32, (GS, tm), 0)
    gscore = []
    for gi in range(N_EXPERT_GROUPS):
        xg = biased[gi * GS:(gi + 1) * GS, :]
        m1 = jnp.max(xg, axis=0, keepdims=True)
        first = jnp.min(jnp.where(xg == m1, sub, GS), axis=0, keepdims=True)
        m2 = jnp.max(jnp.where(sub == first, NEG_INF, xg), axis=0, keepdims=True)
        gscore.append(m1 + m2)
    cur = jnp.concatenate(gscore, axis=0)
    gid = lax.broadcasted_iota(I32, (N_EXPERT_GROUPS, tm), 0)
    gsel = jnp.zeros((N_EXPERT_GROUPS, tm), F32)
    for _ in range(TOPK_GROUPS):
        mx = jnp.max(cur, axis=0, keepdims=True)
        first = jnp.min(jnp.where(cur == mx, gid, N_EXPERT_GROUPS), axis=0, keepdims=True)
        hit = gid == first
        gsel = jnp.where(hit, 1.0, gsel)
        cur = jnp.where(hit, NEG_INF, cur)
    ok = jnp.concatenate(
        [jnp.broadcast_to(gsel[gi:gi + 1, :], (GS, tm)) for gi in range(N_EXPERT_GROUPS)], axis=0)
    masked = jnp.where(ok > 0.5, biased, NEG_INF)
    eid = lax.broadcasted_iota(I32, (E, tm), 0)
    sel = jnp.zeros((E, tm), F32)
    picks, pick_scores = [], []
    for _ in range(TOP_K):
        mx = jnp.max(masked, axis=0, keepdims=True)
        first = jnp.min(jnp.where(masked == mx, eid, E), axis=0, keepdims=True)
        hit = eid == first
        picks.append(first)
        pick_scores.append(jnp.sum(jnp.where(hit, scores, 0.0), axis=0, keepdims=True))
        sel = jnp.where(hit, 1.0, sel)
        masked = jnp.where(hit, NEG_INF, masked)
    total = pick_scores[0]
    for s in pick_scores[1:]:
        total = total + s
    g_ref[...] = jnp.concatenate(pick_scores, axis=0) / total * ROUTED_SCALE
    e_ref[...] = jnp.concatenate(picks, axis=0)
    rr = lax.broadcasted_iota(I32, (tm, tm), 0)
    cc = lax.broadcasted_iota(I32, (tm, tm), 1)
    before = jnp.where(rr < cc, 1.0, 0.0).astype(BF16)
    rank = _dot(sel.astype(BF16), before) + cnt_scr[...]
    r_ref[...] = jnp.concatenate(
        [jnp.sum(jnp.where(eid == p, rank, 0.0), axis=0, keepdims=True) for p in picks], axis=0).astype(I32)
    cnt_scr[...] = cnt_scr[...] + jnp.sum(sel, axis=-1, keepdims=True)
    cnt_ref[...] = cnt_scr[...].astype(I32)


def _router(x, gain, mod, router_w_t, router_b, *, n_prompt, dec_seq, tm):
    N, D = x.shape
    E = N_EXPERTS
    assert D == 2 * SLAB * 128
    assert N % tm == 0 and n_prompt % tm == 0 and dec_seq % tm == 0
    row = functools.partial(_mod_row, tm=tm, n_prompt=n_prompt, dec_seq=dec_seq)
    return pl.pallas_call(
        functools.partial(_router_kernel, tm=tm),
        grid=(N // tm,),
        in_specs=[
            pl.BlockSpec((tm, D), lambda i: (i, 0)),
            pl.BlockSpec((1, D), lambda i: (0, 0)),
            pl.BlockSpec((None, 1, 6 * D), lambda i: (row(i), 0, 0)),
            pl.BlockSpec((E, D), lambda i: (0, 0)),
            pl.BlockSpec((E, 1), lambda i: (0, 0)),
        ],
        out_specs=[
            pl.BlockSpec((tm * SLAB, 128), lambda i: (i, 0)),
            pl.BlockSpec((TOP_K, tm), lambda i: (0, i)),
            pl.BlockSpec((TOP_K, tm), lambda i: (0, i)),
            pl.BlockSpec((TOP_K, tm), lambda i: (0, i)),
            pl.BlockSpec((E, 1), lambda i: (0, 0)),
        ],
        out_shape=[
            jax.ShapeDtypeStruct((N * SLAB, 128), U32),
            jax.ShapeDtypeStruct((TOP_K, N), I32),
            jax.ShapeDtypeStruct((TOP_K, N), F32),
            jax.ShapeDtypeStruct((TOP_K, N), I32),
            jax.ShapeDtypeStruct((E, 1), I32),
        ],
        scratch_shapes=[pltpu.VMEM((E, 1), F32)],
        compiler_params=_cparams(("arbitrary",)),
        name="moe_router",
    )(x, gain.reshape(1, D), mod, router_w_t, router_b.reshape(E, 1))


def _experts_kernel(be_ref, nused_ref, tok_cur, tok_nxt, h_hbm, w1_ref, w3_ref, w2_ref, y_ref,
                    xbuf, sems, w1c, w3c, w2c, prev_e):
    b = pl.program_id(0)
    e = be_ref[b]
    n_used = nused_ref[0]
    BLK = xbuf.shape[1] // SLAB
    slot = b % 2

    FF = w1c.shape[1]
    D = w2c.shape[1]
    CW = 256
    n_chunks = 2 * (FF // CW) + D // CW
    per_chunk = BLK // n_chunks + 1

    def gather(tok_ref, s, lo=0, hi=BLK):
        for r in range(lo, min(hi, BLK)):
            _slab_copy(h_hbm, xbuf.at[s], sems.at[s], tok_ref[0, r], r).start()

    def drain(s):
        for r in range(BLK):
            _slab_copy(h_hbm, xbuf.at[s], sems.at[s], 0, r).wait()

    @pl.when(b == 0)
    def _():
        prev_e[0] = -1
        gather(tok_cur, 0)

    @pl.when(b < n_used)
    def _():
        @pl.when(e != prev_e[0])
        def _():
            w1c[...] = w1_ref[...].astype(BF16)
            w3c[...] = w3_ref[...].astype(BF16)
            w2c[...] = w2_ref[...].astype(BF16)
            prev_e[0] = e

        drain(slot)
        chunk = [0]

        def gather_some():
            gather(tok_nxt, 1 - slot, chunk[0] * per_chunk, (chunk[0] + 1) * per_chunk)
            chunk[0] += 1

        x = _load_rows_bf16(xbuf.at[slot], BLK)
        mids = []
        for c in range(FF // CW):
            sl = slice(c * CW, (c + 1) * CW)
            gather_some()
            a = _dot(x, w1c[:, sl])
            gather_some()
            mids.append((_silu(a) * _dot(x, w3c[:, sl])).astype(BF16))
        mid = jnp.concatenate(mids, axis=1)
        half = D // 2
        for c in range(half // CW):
            gather_some()
            y_lo = _dot(mid, w2c[:, c * CW:(c + 1) * CW])
            gather_some()
            y_hi = _dot(mid, w2c[:, half + c * CW:half + (c + 1) * CW])
            words = _pack_rows(jnp.concatenate([y_lo, y_hi], axis=1))
            for jj in range(CW // 128):
                j = c * (CW // 128) + jj
                y_ref[pl.ds(j, BLK, stride=SLAB), :] = words[:, jj * 128:(jj + 1) * 128]

    @pl.when(b >= n_used)
    def _():
        @pl.when(b == n_used)
        def _():
            drain(slot)

        y_ref[...] = jnp.zeros_like(y_ref)


def _experts(block_expert, n_used, slot_tok8, h, w1, w3, w2, layer):
    n_blocks = block_expert.shape[0]
    BLK = MOE_BLOCK
    _, _, D, FF = w1.shape
    slot_tok3 = slot_tok8.reshape(n_blocks, 1, BLK)
    return pl.pallas_call(
        _experts_kernel,
        grid_spec=pltpu.PrefetchScalarGridSpec(
            num_scalar_prefetch=2,
            grid=(n_blocks,),
            in_specs=[
                pl.BlockSpec((None, 1, BLK), lambda b, be, nu: (b, 0, 0), memory_space=pltpu.SMEM),
                pl.BlockSpec((None, 1, BLK), lambda b, be, nu: (jnp.minimum(b + 1, n_blocks - 1), 0, 0),
                             memory_space=pltpu.SMEM),
                pl.BlockSpec(memory_space=pl.ANY),
                pl.BlockSpec((None, None, D, FF), lambda b, be, nu: (layer, be[b], 0, 0)),
                pl.BlockSpec((None, None, D, FF), lambda b, be, nu: (layer, be[b], 0, 0)),
                pl.BlockSpec((None, None, FF, D), lambda b, be, nu: (layer, be[b], 0, 0)),
            ],
            out_specs=pl.BlockSpec((BLK * SLAB, 128), lambda b, be, nu: (b, 0)),
            scratch_shapes=[
                pltpu.VMEM((2, BLK * SLAB, 128), U32),
                pltpu.SemaphoreType.DMA((2,)),
                pltpu.VMEM((D, FF), BF16),
                pltpu.VMEM((D, FF), BF16),
                pltpu.VMEM((FF, D), BF16),
                pltpu.SMEM((1,), I32),
            ],
        ),
        out_shape=jax.ShapeDtypeStruct((n_blocks * BLK * SLAB, 128), U32),
        compiler_params=_cparams(("arbitrary",)),
        name="moe_experts",
    )(block_expert, n_used, slot_tok3, slot_tok3, h, w1, w3, w2)


def _combine_kernel(dest_cur, dest_nxt, gates_ref, y_hbm, h_ref, x_ref, gate_ref, ws1_ref, ws3_ref, ws2_ref,
                    *rest, prompt_tiles):
    if prompt_tiles is None:
        o_ref, ybuf0, ybuf1, sems = rest
    else:
        op_ref, os_ref, ybuf0, ybuf1, sems = rest
    tm = x_ref.shape[0]
    i = pl.program_id(0)
    last = pl.num_programs(0) - 1

    def gather(dest_ref, buf, sem, lo=0, hi=tm):
        for k in range(TOP_K):
            for r in range(lo, hi):
                _slab_copy(y_hbm, buf.at[k], sem, dest_ref[k, r], r).start()

    def drain(buf, sem):
        for k in range(TOP_K):
            for r in range(tm):
                _slab_copy(y_hbm, buf.at[k], sem, 0, r).wait()

    @pl.when(i == 0)
    def _():
        gather(dest_cur, ybuf0, sems.at[0])

    def step(cur, cur_sem, nxt, nxt_sem):
        drain(cur, cur_sem)
        per_group = tm // (SLAB + 2)
        gather(dest_nxt, nxt, nxt_sem, 0, per_group)
        hb = _load_rows_bf16(h_ref, tm)
        mid = (_silu(_dot(hb, ws1_ref[...])) * _dot(hb, ws3_ref[...])).astype(BF16)
        gather(dest_nxt, nxt, nxt_sem, per_group, 2 * per_group)
        shared = _dot(mid, ws2_ref[...])
        g = [gates_ref[:, k:k + 1] for k in range(TOP_K)]
        lo_parts, hi_parts = [], []
        for j in range(SLAB):
            gather(dest_nxt, nxt, nxt_sem, (j + 2) * per_group, tm if j == SLAB - 1 else (j + 3) * per_group)
            lo_acc = hi_acc = None
            for k in range(TOP_K):
                lo, hi = _load_slab_halves(cur.at[k], tm, j)
                lo_acc = g[k] * lo if lo_acc is None else lo_acc + g[k] * lo
                hi_acc = g[k] * hi if hi_acc is None else hi_acc + g[k] * hi
            lo_parts.append(lo_acc)
            hi_parts.append(hi_acc)
        routed = jnp.concatenate(lo_parts + hi_parts, axis=1)
        out = x_ref[...] + gate_ref[...] * (routed + shared)
        if prompt_tiles is None:
            o_ref[...] = out
        else:
            @pl.when(i < prompt_tiles)
            def _():
                op_ref[...] = out

            @pl.when(i >= prompt_tiles)
            def _():
                os_ref[...] = out

        @pl.when(i == last)
        def _():
            drain(nxt, nxt_sem)

    @pl.when(i % 2 == 0)
    def _():
        step(ybuf0, sems.at[0], ybuf1, sems.at[1])

    @pl.when(i % 2 == 1)
    def _():
        step(ybuf1, sems.at[1], ybuf0, sems.at[0])


def _combine(dest, gates, y, h, x, mod, ws1, ws3, ws2, *, n_prompt, dec_seq, split_outputs):
    N, D = x.shape
    tm = COMBINE_TM
    FF = ws1.shape[1]
    assert N % tm == 0 and n_prompt % tm == 0 and dec_seq % tm == 0
    nt = N // tm
    row = functools.partial(_mod_row, tm=tm, n_prompt=n_prompt, dec_seq=dec_seq)
    dest3 = dest.reshape(TOP_K, nt, tm).transpose(1, 0, 2)
    if split_outputs:
        pt = n_prompt // tm
        out_specs = [pl.BlockSpec((tm, D), lambda i: (jnp.minimum(i, pt - 1), 0)),
                     pl.BlockSpec((tm, D), lambda i: (jnp.maximum(i - pt, 0), 0))]
        out_shape = [jax.ShapeDtypeStruct((n_prompt, D), F32), jax.ShapeDtypeStruct((N - n_prompt, D), F32)]
    else:
        pt = None
        out_specs = pl.BlockSpec((tm, D), lambda i: (i, 0))
        out_shape = jax.ShapeDtypeStruct((N, D), F32)
    return pl.pallas_call(
        functools.partial(_combine_kernel, prompt_tiles=pt),
        grid=(nt,),
        in_specs=[
            pl.BlockSpec((None, TOP_K, tm), lambda i: (i, 0, 0), memory_space=pltpu.SMEM),
            pl.BlockSpec((None, TOP_K, tm), lambda i: (jnp.minimum(i + 1, nt - 1), 0, 0), memory_space=pltpu.SMEM),
            pl.BlockSpec((tm, TOP_K), lambda i: (i, 0)),
            pl.BlockSpec(memory_space=pl.ANY),
            pl.BlockSpec((tm * SLAB, 128), lambda i: (i, 0)),
            pl.BlockSpec((tm, D), lambda i: (i, 0)),
            pl.BlockSpec((None, 1, D), lambda i: (row(i), 0, 5)),
            pl.BlockSpec((D, FF), lambda i: (0, 0)),
            pl.BlockSpec((D, FF), lambda i: (0, 0)),
            pl.BlockSpec((FF, D), lambda i: (0, 0)),
        ],
        out_specs=out_specs,
        out_shape=out_shape,
        scratch_shapes=[pltpu.VMEM((TOP_K, tm * SLAB, 128), U32), pltpu.VMEM((TOP_K, tm * SLAB, 128), U32),
                        pltpu.SemaphoreType.DMA((2,))],
        compiler_params=_cparams(("arbitrary",)),
        name="moe_combine",
    )(dest3, dest3, gates, y, h, x, mod, ws1, ws3, ws2)


def _moe_layer(x, gain, mod, router_w, router_b, w1, w3, w2, layer, ws1, ws3, ws2, *, n_prompt, dec_seq,
               split_outputs=False):
    N = x.shape[0]
    E, BLK = N_EXPERTS, MOE_BLOCK
    h, top_e, gates, rank, counts = _router(x, gain, mod, router_w.T, router_b,
                                            n_prompt=n_prompt, dec_seq=dec_seq,
                                            tm=_row_tile(512, n_prompt, dec_seq))
    counts = counts[:, 0]
    padded = (counts + BLK - 1) // BLK * BLK
    pad_end = jnp.cumsum(padded)
    pad_start = pad_end - padded
    n_blocks = N * TOP_K // BLK + E
    block_expert = jnp.minimum(
        jnp.sum(pad_end[None, :] <= (jnp.arange(n_blocks, dtype=I32) * BLK)[:, None], axis=1), E - 1).astype(I32)
    n_used = (pad_end[-1:] // BLK).astype(I32)
    onehot = top_e[None] == jnp.arange(E, dtype=I32)[:, None, None]
    dest = jnp.sum(jnp.where(onehot, pad_start[:, None, None], 0), axis=0) + rank
    NP = N + BLK
    keys = (top_e * NP + jnp.arange(N, dtype=I32)[None, :]).reshape(-1)
    fill_i = jnp.arange(BLK, dtype=I32)[None, :]
    fill_e = jnp.arange(E, dtype=I32)[:, None]
    fillers = jnp.where(fill_i < (padded - counts)[:, None], fill_e * NP + N + fill_i, E * NP + fill_e * BLK + fill_i)
    slot_t = lax.sort(jnp.concatenate([keys, fillers.reshape(-1)])) % NP
    slot_tok8 = jnp.where(slot_t < N, slot_t, 0) * SLAB
    y = _experts(block_expert, n_used, slot_tok8, h, w1, w3, w2, layer)
    return _combine(dest * SLAB, gates.T, y, h, x, mod, ws1, ws3, ws2, n_prompt=n_prompt, dec_seq=dec_seq,
                    split_outputs=split_outputs)


def _rope_tables(S):
    quarter = HEAD_DIM // 4
    pos = jnp.arange(S)
    row_id = (pos // GRID_W).astype(F32)
    col_id = (pos % GRID_W).astype(F32)
    inv = ROPE_THETA ** (-jnp.arange(quarter, dtype=F32) / quarter)
    ar, ac = row_id[:, None] * inv, col_id[:, None] * inv
    cos = jnp.concatenate([jnp.cos(ar), jnp.cos(ar), jnp.cos(ac), jnp.cos(ac)], axis=-1)
    sin = jnp.concatenate([-jnp.sin(ar), jnp.sin(ar), -jnp.sin(ac), jnp.sin(ac)], axis=-1)
    return cos, sin


def kernel(x_prompt, x_sample, state_mlstm_C, state_mlstm_n, state_mlstm_m, cache_attn_k, cache_attn_v,
           c, c_ctx, ada_w, ada_b, norm_mix, norm_ffn,
           mlstm_w_in, mlstm_gate_b, mlstm_head_g, mlstm_w_out,
           attn_w_qkv, attn_q_g, attn_k_g, attn_sink, attn_w_o,
           moe_router_w, moe_router_b, moe_w1, moe_w3, moe_w2, shared_w1, shared_w3, shared_w2):
    D = D_MODEL
    Bp, Sp, _ = x_prompt.shape
    Bs, Ss, _ = x_sample.shape
    n_prompt = Bp * Sp
    N = n_prompt + Bs * Ss
    dims = dict(n_prompt=n_prompt, dec_seq=Ss)
    tm512 = _row_tile(512, n_prompt, Ss)

    x = jnp.concatenate([x_prompt.reshape(n_prompt, D), x_sample.reshape(Bs * Ss, D)], axis=0)
    rows = 16
    cvec = jnp.zeros((rows, D), F32).at[0].set(c_ctx).at[1:1 + Bs].set(c)
    mod_all = _ada_table(cvec, ada_w, ada_b)[:, :1 + Bs].reshape(DEPTH, 1 + Bs, 1, 6 * D)

    H = MLSTM_HEADS
    mod = mod_all[0]
    w_in = mlstm_w_in[0]
    w_main = w_in[:, :MLSTM_MAIN].astype(BF16)
    w_gate = jnp.pad(w_in[:, MLSTM_MAIN:], ((0, 0), (0, GATE_PAD - 4 * H))).astype(BF16)
    gate_b = jnp.pad(mlstm_gate_b[0], (0, GATE_PAD - 4 * H))
    col_scale = jnp.concatenate([jnp.full((MLSTM_QK,), MLSTM_DK ** -0.5, F32),
                                 jnp.ones((MLSTM_MAIN - MLSTM_QK,), F32)])
    qkvo, gates = _nm_matmul(x, norm_mix[0], mod, w_main, col_scale, jnp.zeros((MLSTM_MAIN,), F32),
                             shift_idx=0, scale_idx=1, out_dtype=BF16, tm=tm512, tn=1024,
                             aux_w=w_gate, aux_b=gate_b, name="mlstm_proj", **dims)
    L = 256
    state = (state_mlstm_C[:, 0], state_mlstm_n[:, 0], state_mlstm_m[:, 0])
    hf, hb, C_p, n_p, m_p = _mlstm_scan(qkvo, gates, Bp, Sp, Bs, Ss, L, state)
    x = _mm_residual((hf, hb, qkvo, mlstm_head_g[0]), mlstm_w_out[0].astype(BF16), x, mod, gate_idx=2,
                     tm=tm512, tn=1024, mlstm_prologue=True, name="mlstm_out", **dims)
    x = _moe_layer(x, norm_ffn[0], mod, moe_router_w[0], moe_router_b[0], moe_w1, moe_w3, moe_w2, 0,
                   shared_w1[0].astype(BF16), shared_w3[0].astype(BF16), shared_w2[0].astype(BF16), **dims)

    mod = mod_all[1]
    qkv = _nm_matmul(x, norm_mix[1], mod, attn_w_qkv[0].astype(BF16), jnp.ones((ATTN_PROJ,), F32),
                     jnp.zeros((ATTN_PROJ,), F32), shift_idx=0, scale_idx=1, out_dtype=F32, tm=tm512, tn=1024,
                     name="attn_qkv", **dims)
    ident_cos = jnp.ones((Sp, HEAD_DIM), F32)
    ident_sin = jnp.zeros((Sp, HEAD_DIM), F32)
    q_p, k_p, v_p, kf_p, vf_p = _qk_prep(qkv, attn_q_g[0], attn_k_g[0], ident_cos, ident_sin, 0, n_prompt, Sp,
                                         True, "qk_prep_prompt")
    cos, sin = _rope_tables(Ss)
    q_s, k_s, v_s = _qk_prep(qkv, attn_q_g[0], attn_k_g[0], cos, sin, n_prompt, Bs * Ss, 256, False,
                             "qk_prep_latent")
    o_p = _attn_context(attn_sink[0], q_p, k_p, v_p, Bp, Sp)
    P = cache_attn_k.shape[2]
    ctx_k = cache_attn_k[:, 0].reshape(Bs, P, ATTN_KV).astype(BF16)
    ctx_v = cache_attn_v[:, 0].reshape(Bs, P, ATTN_KV).astype(BF16)
    o_s = _attn_latent(attn_sink[0], q_s, k_s, v_s, ctx_k, ctx_v, Bs, Ss)
    x = _mm_residual((o_p, o_s), attn_w_o[0].astype(BF16), x, mod, gate_idx=2,
                     tm=_row_tile(1024, n_prompt, Ss), tn=1024,
                     mlstm_prologue=False, name="attn_out", **dims)
    y_prompt, y_sample = _moe_layer(
        x, norm_ffn[1], mod, moe_router_w[1], moe_router_b[1], moe_w1, moe_w3, moe_w2, 1,
        shared_w1[1].astype(BF16), shared_w3[1].astype(BF16), shared_w2[1].astype(BF16), split_outputs=True, **dims)
    return (y_prompt.reshape(Bp, Sp, D), y_sample.reshape(Bs, Ss, D), C_p[:, None], n_p[:, None], m_p[:, None],
            kf_p.reshape(Bp, 1, Sp, ATTN_KV_HEADS, HEAD_DIM), vf_p.reshape(Bp, 1, Sp, ATTN_KV_HEADS, HEAD_DIM))
```

```python
import functools

import jax
import jax.numpy as jnp
from jax import lax
from jax.experimental import pallas as pl
from jax.experimental.pallas import tpu as pltpu

F32 = jnp.float32
BF16 = jnp.bfloat16
I32 = jnp.int32

D_MODEL = 2048
DEPTH = 2
EPS = 1e-6
GRID_W = 64
MLSTM_HEADS = 8
MLSTM_DK = 128
MLSTM_DV = 256
MLSTM_QK = MLSTM_HEADS * MLSTM_DK
MLSTM_V = MLSTM_HEADS * MLSTM_DV
MLSTM_MAIN = 2 * MLSTM_QK + 2 * MLSTM_V
GATE_PAD = 128
HEAD_DIM = 128
ATTN_Q_HEADS = 16
ATTN_KV_HEADS = 4
ATTN_GROUPS = 4
WINDOW = 128
QBLK = 128
ROPE_THETA = 10000.0
ATTN_Q = ATTN_Q_HEADS * HEAD_DIM
ATTN_KV = ATTN_KV_HEADS * HEAD_DIM
ATTN_PROJ = ATTN_Q + 2 * ATTN_KV
N_EXPERTS = 64
TOP_K = 8
N_EXPERT_GROUPS = 8
TOPK_GROUPS = 4
GROUP_SIZE = N_EXPERTS // N_EXPERT_GROUPS
EXPERT_FF = 512
ROUTED_SCALE = 2.5
MOE_BLOCK = 512
COMBINE_TM = 128

V7X_VMEM_LIMIT = 56 * 1024 * 1024
NEG_INF = float("-inf")


def _cparams(sem):
    return pltpu.CompilerParams(dimension_semantics=("arbitrary",) * len(sem), vmem_limit_bytes=V7X_VMEM_LIMIT)


def _split_hi_lo(a):
    hi = a.astype(BF16)
    lo = (a - hi.astype(F32)).astype(BF16)
    return hi, lo


def _dot(a, b):
    return jnp.dot(a, b, preferred_element_type=F32)


def _dot_nt(a, b):
    return lax.dot_general(a, b, (((1,), (1,)), ((), ())), preferred_element_type=F32)


def _dot_tn(a, b):
    return lax.dot_general(a, b, (((0,), (0,)), ((), ())), preferred_element_type=F32)


def _silu(x):
    return x * jax.nn.sigmoid(x)


def _row_tile(preferred, n_prompt, dec_seq):
    tm = preferred
    while n_prompt % tm or dec_seq % tm:
        tm //= 2
    return tm


def _mod_row(i, tm, n_prompt, dec_seq):
    r0 = i * tm
    return jnp.where(r0 < n_prompt, 0, 1 + (r0 - n_prompt) // dec_seq)


def _norm_modulate(x, gain, mod_ref, shift_idx, scale_idx):
    D = D_MODEL
    y = x * lax.rsqrt(jnp.mean(x * x, axis=-1, keepdims=True) + EPS) * gain
    shift = mod_ref[:, shift_idx * D:(shift_idx + 1) * D]
    scale = mod_ref[:, scale_idx * D:(scale_idx + 1) * D]
    return y * (1.0 + scale) + shift


def _ada_kernel(c_ref, w_ref, b_ref, o_ref):
    s = _silu(c_ref[...])
    s_hi, s_lo = _split_hi_lo(s)
    w_hi, w_lo = _split_hi_lo(w_ref[...])
    o_ref[...] = _dot(s_hi, w_hi) + _dot(s_hi, w_lo) + _dot(s_lo, w_hi) + b_ref[...]


def _ada_table(cvec, ada_w, ada_b):
    D = D_MODEL
    tn = 1024
    rows = cvec.shape[0]
    return pl.pallas_call(
        _ada_kernel,
        grid=(DEPTH, 6 * D // tn),
        in_specs=[
            pl.BlockSpec((rows, D), lambda l, j: (0, 0)),
            pl.BlockSpec((None, D, tn), lambda l, j: (l, 0, j)),
            pl.BlockSpec((None, 1, tn), lambda l, j: (l, 0, j)),
        ],
        out_specs=pl.BlockSpec((None, rows, tn), lambda l, j: (l, 0, j)),
        out_shape=jax.ShapeDtypeStruct((DEPTH, rows, 6 * D), F32),
        compiler_params=_cparams(("parallel", "parallel")),
        name="ada_table",
    )(cvec, ada_w, ada_b.reshape(DEPTH, 1, 6 * D))


def _nm_matmul_kernel(x_ref, gain_ref, mod_ref, w_ref, cs_ref, cb_ref, *rest, shift_idx, scale_idx, has_aux):
    if has_aux:
        wa_ref, ab_ref, o_ref, aux_ref, h_scr = rest
    else:
        o_ref, h_scr = rest

    @pl.when(pl.program_id(1) == 0)
    def _():
        h = _norm_modulate(x_ref[...], gain_ref[...], mod_ref, shift_idx, scale_idx).astype(BF16)
        h_scr[...] = h
        if has_aux:
            aux_ref[...] = _dot(h, wa_ref[...]) + ab_ref[...]

    acc = _dot(h_scr[...], w_ref[...])
    o_ref[...] = (acc * cs_ref[...] + cb_ref[...]).astype(o_ref.dtype)


def _nm_matmul(x, gain, mod, w, col_scale, col_bias, *, shift_idx, scale_idx, n_prompt, dec_seq,
               out_dtype, tm, tn, aux_w=None, aux_b=None, name):
    N, D = x.shape
    P = w.shape[1]
    assert N % tm == 0 and P % tn == 0 and n_prompt % tm == 0 and dec_seq % tm == 0
    has_aux = aux_w is not None
    row = functools.partial(_mod_row, tm=tm, n_prompt=n_prompt, dec_seq=dec_seq)
    in_specs = [
        pl.BlockSpec((tm, D), lambda i, j: (i, 0)),
        pl.BlockSpec((1, D), lambda i, j: (0, 0)),
        pl.BlockSpec((None, 1, 6 * D), lambda i, j: (row(i), 0, 0)),
        pl.BlockSpec((D, tn), lambda i, j: (0, j)),
        pl.BlockSpec((1, tn), lambda i, j: (0, j)),
        pl.BlockSpec((1, tn), lambda i, j: (0, j)),
    ]
    args = [x, gain.reshape(1, D), mod, w, col_scale.reshape(1, P), col_bias.reshape(1, P)]
    out_specs = pl.BlockSpec((tm, tn), lambda i, j: (i, j))
    out_shape = jax.ShapeDtypeStruct((N, P), out_dtype)
    if has_aux:
        PA = aux_w.shape[1]
        in_specs += [pl.BlockSpec((D, PA), lambda i, j: (0, 0)), pl.BlockSpec((1, PA), lambda i, j: (0, 0))]
        args += [aux_w, aux_b.reshape(1, PA)]
        out_specs = [out_specs, pl.BlockSpec((tm, PA), lambda i, j: (i, 0))]
        out_shape = [out_shape, jax.ShapeDtypeStruct((N, PA), F32)]
    return pl.pallas_call(
        functools.partial(_nm_matmul_kernel, shift_idx=shift_idx, scale_idx=scale_idx, has_aux=has_aux),
        grid=(N // tm, P // tn),
        in_specs=in_specs,
        out_specs=out_specs,
        out_shape=out_shape,
        scratch_shapes=[pltpu.VMEM((tm, D), BF16)],
        compiler_params=_cparams(("parallel", "arbitrary")),
        name=name,
    )(*args)


def _mm_residual_kernel(*refs, mlstm_prologue, prompt_tiles):
    if mlstm_prologue:
        hf_ref, hb_ref, og_ref, hg_ref, w_ref, x_ref, gate_ref, o_ref, l_scr = refs

        @pl.when(pl.program_id(1) == 0)
        def _():
            hs = hf_ref[...] + hb_ref[...]
            og = jax.nn.sigmoid(og_ref[...].astype(F32))
            for h in range(MLSTM_HEADS):
                sl = slice(h * MLSTM_DV, (h + 1) * MLSTM_DV)
                hh = hs[:, sl]
                hn = hh * lax.rsqrt(jnp.mean(hh * hh, axis=-1, keepdims=True) + EPS) * hg_ref[:, sl]
                l_scr[:, sl] = (og[:, sl] * hn).astype(BF16)

        o_ref[...] = x_ref[...] + gate_ref[...] * _dot(l_scr[...], w_ref[...])
    else:
        lp_ref, ls_ref, w_ref, x_ref, gate_ref, o_ref = refs
        context = pl.program_id(0) < prompt_tiles

        @pl.when(context)
        def _():
            o_ref[...] = x_ref[...] + gate_ref[...] * _dot(lp_ref[...], w_ref[...])

        @pl.when(jnp.logical_not(context))
        def _():
            o_ref[...] = x_ref[...] + gate_ref[...] * _dot(ls_ref[...], w_ref[...])


def _mm_residual(lhs_args, w, x, mod, *, gate_idx, n_prompt, dec_seq, tm, tn, mlstm_prologue, name):
    N, D = x.shape
    K = w.shape[0]
    assert N % tm == 0 and D % tn == 0 and n_prompt % tm == 0 and dec_seq % tm == 0
    row = functools.partial(_mod_row, tm=tm, n_prompt=n_prompt, dec_seq=dec_seq)
    pt = n_prompt // tm
    if mlstm_prologue:
        hf, hb, qkvo, head_g = lhs_args
        o_blk = (2 * MLSTM_QK + MLSTM_V) // MLSTM_V
        in_specs = [
            pl.BlockSpec((tm, K), lambda i, j: (i, 0)),
            pl.BlockSpec((tm, K), lambda i, j: (i, 0)),
            pl.BlockSpec((tm, MLSTM_V), lambda i, j: (i, o_blk)),
            pl.BlockSpec((1, K), lambda i, j: (0, 0)),
        ]
        args = [hf, hb, qkvo, head_g.reshape(1, K)]
        scratch = [pltpu.VMEM((tm, K), BF16)]
    else:
        lhs_p, lhs_s = lhs_args
        in_specs = [pl.BlockSpec((tm, K), lambda i, j: (jnp.minimum(i, pt - 1), 0)),
                    pl.BlockSpec((tm, K), lambda i, j: (jnp.maximum(i - pt, 0), 0))]
        args = [lhs_p, lhs_s]
        scratch = []
    in_specs += [
        pl.BlockSpec((K, tn), lambda i, j: (0, j)),
        pl.BlockSpec((tm, tn), lambda i, j: (i, j)),
        pl.BlockSpec((None, 1, tn), lambda i, j: (row(i), 0, gate_idx * (D // tn) + j)),
    ]
    args += [w, x, mod]
    return pl.pallas_call(
        functools.partial(_mm_residual_kernel, mlstm_prologue=mlstm_prologue, prompt_tiles=pt),
        grid=(N // tm, D // tn),
        in_specs=in_specs,
        out_specs=pl.BlockSpec((tm, tn), lambda i, j: (i, j)),
        out_shape=jax.ShapeDtypeStruct((N, D), F32),
        scratch_shapes=scratch,
        compiler_params=_cparams(("parallel", "arbitrary")),
        name=name,
    )(*args)


def _log_sigmoid(x):
    return jnp.minimum(x, 0.0) - jnp.log(1.0 + jnp.exp(-jnp.abs(x)))


def _mlstm_direction(d, q_ref, k_ref, v_ref, g_ref, h_ref, C_scr, n_scr, m_scr, L):
    H, DK, DV = MLSTM_HEADS, MLSTM_DK, MLSTM_DV
    g = g_ref[...]
    lf = _log_sigmoid(g)
    r = lax.broadcasted_iota(I32, (L, L), 0)
    c = lax.broadcasted_iota(I32, (L, L), 1)
    causal = (c <= r) if d == 0 else (c >= r)
    tri = jnp.where(causal, 1.0, 0.0).astype(BF16)
    lf1 = lf.astype(BF16)
    rem = lf - lf1.astype(F32)
    lf2 = rem.astype(BF16)
    lf3 = (rem - lf2.astype(F32)).astype(BF16)
    bsum = _dot(tri, lf1) + _dot(tri, lf2) + _dot(tri, lf3)
    g_t = g.T
    b_t = bsum.T
    end = L - 1 if d == 0 else 0
    m_old = [m_scr[d, h] for h in range(H)]
    n_old = [n_scr[d, h] for h in range(H)]
    m_upd, n_upd = [], []
    for h in range(H):
        ci, cf = d * 2 * H + h, d * 2 * H + H + h
        b_col, i_col = bsum[:, cf:cf + 1], g[:, ci:ci + 1]
        b_row, i_row = b_t[cf:cf + 1, :], g_t[ci:ci + 1, :]
        b_end = bsum[end:end + 1, cf:cf + 1]
        m, n = m_old[h], n_old[h]
        C = C_scr[d * H + h][...]
        qh = q_ref[:, h * DK:(h + 1) * DK]
        kh = k_ref[:, h * DK:(h + 1) * DK]
        vh = v_ref[:, h * DV:(h + 1) * DV]
        dmat = jnp.where(causal, b_col - b_row + i_row, NEG_INF)
        inter = b_col + m
        m_out = jnp.maximum(inter, jnp.max(dmat, axis=-1, keepdims=True))
        w = jnp.exp(dmat - m_out) * _dot_nt(qh, kh)
        dec = jnp.exp(inter - m_out)
        num = _dot(w.astype(BF16), vh) + dec * _dot(qh, C.astype(BF16))
        den = jnp.sum(w, axis=-1, keepdims=True) + dec * jnp.sum(qh.astype(F32) * n, axis=-1, keepdims=True)
        h_ref[:, h * DV:(h + 1) * DV] = num / jnp.maximum(jnp.abs(den), jnp.exp(-m_out))
        to_end = b_end - b_col + i_col
        m_new = jnp.maximum(b_end + m, jnp.max(to_end, axis=0, keepdims=True))
        wk = jnp.exp(to_end - m_new)
        dec_end = jnp.exp(b_end + m - m_new)
        kw = kh.astype(F32) * wk
        C_scr[d * H + h][...] = dec_end * C + _dot_tn(kw.astype(BF16), vh)
        n_upd.append(dec_end * n + jnp.sum(kw, axis=0, keepdims=True))
        m_upd.append(m_new)
    for h in range(H):
        n_scr[d, h] = n_upd[h]
        m_scr[d, h] = m_upd[h]


def _mlstm_kernel(fwd_ref, bwd_ref, bidx_ref, flag_ref, qf, kf, vf, gf, qb, kb, vb, gb, C0, n0, m0,
                  hf, hb, Co, no, mo, *scratch, L):
    H = MLSTM_HEADS
    C_scr, (n_scr, m_scr) = scratch[:2 * H], scratch[2 * H:]
    flags = flag_ref[pl.program_id(0)]
    first, last, context = (flags & 1) != 0, (flags & 2) != 0, (flags & 4) != 0

    @pl.when(first & context)
    def _():
        for C in C_scr:
            C[...] = jnp.zeros_like(C)
        n_scr[...] = jnp.zeros_like(n_scr)
        m_scr[...] = jnp.zeros_like(m_scr)

    @pl.when(first & jnp.logical_not(context))
    def _():
        for d in range(2):
            for h in range(H):
                C_scr[d * H + h][...] = C0[d, h]
                n_scr[d, h] = n0[d, h:h + 1, :]
                m_scr[d, h] = m0[d:d + 1, h:h + 1]

    _mlstm_direction(0, qf, kf, vf, gf, hf, C_scr, n_scr, m_scr, L)
    _mlstm_direction(1, qb, kb, vb, gb, hb, C_scr, n_scr, m_scr, L)

    @pl.when(last & context)
    def _():
        for d in range(2):
            for h in range(H):
                Co[d, h] = C_scr[d * H + h][...]
                no[d, h:h + 1, :] = n_scr[d, h]
                mo[d:d + 1, h:h + 1] = m_scr[d, h]


def _mlstm_scan(qkvo, gates, Bp, Sp, Bs, Ss, L, state):
    H, DK, DV = MLSTM_HEADS, MLSTM_DK, MLSTM_DV
    assert Sp % L == 0 and Ss % L == 0
    fwd, bwd, bidx, flags = [], [], [], []
    for context, B, S, base in ((1, Bp, Sp, 0), (0, Bs, Ss, Bp * Sp // L)):
        nc = S // L
        for b in range(B):
            for c in range(nc):
                fwd.append(base + b * nc + c)
                bwd.append(base + b * nc + nc - 1 - c)
                bidx.append(b)
                flags.append((c == 0) * 1 + (c == nc - 1) * 2 + context * 4)
    tables = [jnp.asarray(t, I32) for t in (fwd, bwd, bidx, flags)]
    lat_b = lambda s, fl, bi: jnp.where((fl[s] & 4) != 0, 0, bi[s])
    ctx_b = lambda s, fl, bi: jnp.where((fl[s] & 4) != 0, bi[s], Bp - 1)

    def specs(tbl):
        return [
            pl.BlockSpec((L, MLSTM_QK), lambda s, f, w, bi, fl: ((f, w)[tbl][s], 0)),
            pl.BlockSpec((L, MLSTM_QK), lambda s, f, w, bi, fl: ((f, w)[tbl][s], 1)),
            pl.BlockSpec((L, MLSTM_V), lambda s, f, w, bi, fl: ((f, w)[tbl][s], 1)),
            pl.BlockSpec((L, GATE_PAD), lambda s, f, w, bi, fl: ((f, w)[tbl][s], 0)),
        ]

    def state_specs(which):
        return [
            pl.BlockSpec((None, 2, H, DK, DV), lambda s, f, w, bi, fl: (which(s, fl, bi), 0, 0, 0, 0)),
            pl.BlockSpec((None, 2, H, DK), lambda s, f, w, bi, fl: (which(s, fl, bi), 0, 0, 0)),
            pl.BlockSpec((None, 2, H), lambda s, f, w, bi, fl: (which(s, fl, bi), 0, 0)),
        ]

    N = qkvo.shape[0]
    return pl.pallas_call(
        functools.partial(_mlstm_kernel, L=L),
        grid_spec=pltpu.PrefetchScalarGridSpec(
            num_scalar_prefetch=4,
            grid=(len(fwd),),
            in_specs=specs(0) + specs(1) + state_specs(lat_b),
            out_specs=[
                pl.BlockSpec((L, MLSTM_V), lambda s, f, w, bi, fl: (f[s], 0)),
                pl.BlockSpec((L, MLSTM_V), lambda s, f, w, bi, fl: (w[s], 0)),
            ] + state_specs(ctx_b),
            scratch_shapes=[pltpu.VMEM((DK, DV), F32)] * (2 * H) + [
                pltpu.VMEM((2, H, 1, DK), F32),
                pltpu.VMEM((2, H, 1, 1), F32),
            ],
        ),
        out_shape=[
            jax.ShapeDtypeStruct((N, MLSTM_V), F32),
            jax.ShapeDtypeStruct((N, MLSTM_V), F32),
            jax.ShapeDtypeStruct((Bp, 2, H, DK, DV), F32),
            jax.ShapeDtypeStruct((Bp, 2, H, DK), F32),
            jax.ShapeDtypeStruct((Bp, 2, H), F32),
        ],
        compiler_params=_cparams(("arbitrary",)),
        name="mlstm_scan",
    )(*tables, qkvo, qkvo, qkvo, gates, qkvo, qkvo, qkvo, gates, *state)


def _qk_prep_kernel(qkv_ref, qg_ref, kg_ref, cos_ref, sin_ref, *outs, emit_f32):
    if emit_f32:
        q_ref, k_ref, v_ref, kf_ref, vf_ref = outs
    else:
        q_ref, k_ref, v_ref = outs
    cos = cos_ref[...]
    sin = sin_ref[...]
    lane = lax.broadcasted_iota(I32, cos.shape, 1)
    first = (lane % (HEAD_DIM // 2)) < (HEAD_DIM // 4)

    def norm_rope(x, gain):
        xn = x * lax.rsqrt(jnp.mean(x * x, axis=-1, keepdims=True) + EPS) * gain
        partner = jnp.where(first, pltpu.roll(xn, HEAD_DIM - HEAD_DIM // 4, 1), pltpu.roll(xn, HEAD_DIM // 4, 1))
        return xn * cos + partner * sin

    for h in range(ATTN_Q_HEADS):
        sl = slice(h * HEAD_DIM, (h + 1) * HEAD_DIM)
        q_ref[:, sl] = (norm_rope(qkv_ref[:, sl], qg_ref[...]) * HEAD_DIM ** -0.5).astype(BF16)
    for h in range(ATTN_KV_HEADS):
        sl = slice(h * HEAD_DIM, (h + 1) * HEAD_DIM)
        kx = norm_rope(qkv_ref[:, ATTN_Q + h * HEAD_DIM:ATTN_Q + (h + 1) * HEAD_DIM], kg_ref[...])
        k_ref[:, sl] = kx.astype(BF16)
        if emit_f32:
            kf_ref[:, sl] = kx
    vx = qkv_ref[:, ATTN_Q + ATTN_KV:]
    v_ref[...] = vx.astype(BF16)
    if emit_f32:
        vf_ref[...] = vx


def _qk_prep(qkv, q_g, k_g, cos, sin, row0, rows, tm, emit_f32, name):
    assert rows % tm == 0 and row0 % tm == 0 and cos.shape[0] % tm == 0
    base = row0 // tm
    nt = cos.shape[0] // tm
    out_specs = [
        pl.BlockSpec((tm, ATTN_Q), lambda i: (i, 0)),
        pl.BlockSpec((tm, ATTN_KV), lambda i: (i, 0)),
        pl.BlockSpec((tm, ATTN_KV), lambda i: (i, 0)),
    ]
    out_shape = [
        jax.ShapeDtypeStruct((rows, ATTN_Q), BF16),
        jax.ShapeDtypeStruct((rows, ATTN_KV), BF16),
        jax.ShapeDtypeStruct((rows, ATTN_KV), BF16),
    ]
    if emit_f32:
        out_specs += [pl.BlockSpec((tm, ATTN_KV), lambda i: (i, 0))] * 2
        out_shape += [jax.ShapeDtypeStruct((rows, ATTN_KV), F32)] * 2
    return pl.pallas_call(
        functools.partial(_qk_prep_kernel, emit_f32=emit_f32),
        grid=(rows // tm,),
        in_specs=[
            pl.BlockSpec((tm, ATTN_PROJ), lambda i: (base + i, 0)),
            pl.BlockSpec((1, HEAD_DIM), lambda i: (0, 0)),
            pl.BlockSpec((1, HEAD_DIM), lambda i: (0, 0)),
            pl.BlockSpec((tm, HEAD_DIM), lambda i: (i % nt, 0)),
            pl.BlockSpec((tm, HEAD_DIM), lambda i: (i % nt, 0)),
        ],
        out_specs=out_specs,
        out_shape=out_shape,
        compiler_params=_cparams(("parallel",)),
        name=name,
    )(qkv, q_g.reshape(1, HEAD_DIM), k_g.reshape(1, HEAD_DIM), cos, sin)


def _sink_column(sink_ref, kv, rows_per_head):
    parts = [jnp.full((rows_per_head, 1), sink_ref[kv * ATTN_GROUPS + g], F32) for g in range(ATTN_GROUPS)]
    return jnp.concatenate(parts, axis=0)


def _attn_ctx_kernel(sink_ref, q_ref, k_ref, v_ref, o_ref):
    S = q_ref.shape[0]
    for kv in range(ATTN_KV_HEADS):
        q = jnp.concatenate(
            [q_ref[:, (kv * ATTN_GROUPS + g) * HEAD_DIM:(kv * ATTN_GROUPS + g + 1) * HEAD_DIM]
             for g in range(ATTN_GROUPS)], axis=0)
        ksl = slice(kv * HEAD_DIM, (kv + 1) * HEAD_DIM)
        s = _dot_nt(q, k_ref[:, ksl])
        sk = _sink_column(sink_ref, kv, S)
        m = jnp.maximum(jnp.max(s, axis=-1, keepdims=True), sk)
        p = jnp.exp(s - m)
        den = jnp.sum(p, axis=-1, keepdims=True) + jnp.exp(sk - m)
        o = _dot(p.astype(BF16), v_ref[:, ksl]) * (1.0 / den)
        for g in range(ATTN_GROUPS):
            hq = kv * ATTN_GROUPS + g
            o_ref[:, hq * HEAD_DIM:(hq + 1) * HEAD_DIM] = o[g * S:(g + 1) * S].astype(BF16)


def _attn_context(sink, q, k, v, B, S):
    return pl.pallas_call(
        _attn_ctx_kernel,
        grid_spec=pltpu.PrefetchScalarGridSpec(
            num_scalar_prefetch=0,
            grid=(B,),
            in_specs=[
                pl.BlockSpec(memory_space=pltpu.SMEM),
                pl.BlockSpec((S, ATTN_Q), lambda b: (b, 0)),
                pl.BlockSpec((S, ATTN_KV), lambda b: (b, 0)),
                pl.BlockSpec((S, ATTN_KV), lambda b: (b, 0)),
            ],
            out_specs=pl.BlockSpec((S, ATTN_Q), lambda b: (b, 0)),
        ),
        out_shape=jax.ShapeDtypeStruct((B * S, ATTN_Q), BF16),
        compiler_params=_cparams(("parallel",)),
        name="attn_context",
    )(sink, q, k, v)


def _attn_lat_kernel(sink_ref, q_ref, kp_ref, kc_ref, kn_ref, vp_ref, vc_ref, vn_ref, kx_ref, vx_ref, o_ref):
    i = pl.program_id(1)
    nb = pl.num_programs(1)
    R = ATTN_GROUPS * QBLK
    r = lax.broadcasted_iota(I32, (R, QBLK), 0) % QBLK
    c = lax.broadcasted_iota(I32, (R, QBLK), 1)
    ok_prev = c >= r
    ok_next = c <= r
    edge_prev = jnp.where(i > 0, 0.0, NEG_INF)
    edge_next = jnp.where(i < nb - 1, 0.0, NEG_INF)
    for kv in range(ATTN_KV_HEADS):
        q = jnp.concatenate(
            [q_ref[:, (kv * ATTN_GROUPS + g) * HEAD_DIM:(kv * ATTN_GROUPS + g + 1) * HEAD_DIM]
             for g in range(ATTN_GROUPS)], axis=0)
        ksl = slice(kv * HEAD_DIM, (kv + 1) * HEAD_DIM)
        s_p = jnp.where(ok_prev, _dot_nt(q, kp_ref[:, ksl]) + edge_prev, NEG_INF)
        s_c = _dot_nt(q, kc_ref[:, ksl])
        s_n = jnp.where(ok_next, _dot_nt(q, kn_ref[:, ksl]) + edge_next, NEG_INF)
        s_x = _dot_nt(q, kx_ref[:, ksl])
        sk = _sink_column(sink_ref, kv, QBLK)
        m = jnp.maximum(
            jnp.maximum(jnp.max(s_p, axis=-1, keepdims=True), jnp.max(s_c, axis=-1, keepdims=True)),
            jnp.maximum(jnp.max(s_n, axis=-1, keepdims=True), jnp.max(s_x, axis=-1, keepdims=True)))
        m = jnp.maximum(m, sk)
        p_p, p_c, p_n, p_x = jnp.exp(s_p - m), jnp.exp(s_c - m), jnp.exp(s_n - m), jnp.exp(s_x - m)
        den = (jnp.sum(p_p, axis=-1, keepdims=True) + jnp.sum(p_c, axis=-1, keepdims=True)
               + jnp.sum(p_n, axis=-1, keepdims=True) + jnp.sum(p_x, axis=-1, keepdims=True) + jnp.exp(sk - m))
        o = (_dot(p_p.astype(BF16), vp_ref[:, ksl]) + _dot(p_c.astype(BF16), vc_ref[:, ksl])
             + _dot(p_n.astype(BF16), vn_ref[:, ksl]) + _dot(p_x.astype(BF16), vx_ref[:, ksl])) * (1.0 / den)
        for g in range(ATTN_GROUPS):
            hq = kv * ATTN_GROUPS + g
            o_ref[:, hq * HEAD_DIM:(hq + 1) * HEAD_DIM] = o[g * QBLK:(g + 1) * QBLK].astype(BF16)


def _attn_latent(sink, q, k, v, ctx_k, ctx_v, B, S):
    nb = S // QBLK
    P = ctx_k.shape[1]
    prev = lambda b, i: (b * nb + jnp.maximum(i - 1, 0), 0)
    cur = lambda b, i: (b * nb + i, 0)
    nxt = lambda b, i: (b * nb + jnp.minimum(i + 1, nb - 1), 0)
    kv_spec = lambda f: pl.BlockSpec((QBLK, ATTN_KV), f)
    return pl.pallas_call(
        _attn_lat_kernel,
        grid_spec=pltpu.PrefetchScalarGridSpec(
            num_scalar_prefetch=0,
            grid=(B, nb),
            in_specs=[
                pl.BlockSpec(memory_space=pltpu.SMEM),
                pl.BlockSpec((QBLK, ATTN_Q), cur),
                kv_spec(prev), kv_spec(cur), kv_spec(nxt),
                kv_spec(prev), kv_spec(cur), kv_spec(nxt),
                pl.BlockSpec((None, P, ATTN_KV), lambda b, i: (b, 0, 0)),
                pl.BlockSpec((None, P, ATTN_KV), lambda b, i: (b, 0, 0)),
            ],
            out_specs=pl.BlockSpec((QBLK, ATTN_Q), cur),
        ),
        out_shape=jax.ShapeDtypeStruct((B * S, ATTN_Q), BF16),
        compiler_params=_cparams(("parallel", "parallel")),
        name="attn_latent",
    )(sink, q, k, k, k, v, v, v, ctx_k, ctx_v)


SLAB = 8
U32 = jnp.uint32


def _pack_rows(x):
    half = x.shape[1] // 2
    bits = lax.bitcast_convert_type(x.astype(BF16).astype(F32), U32)
    return (bits[:, :half] >> 16) | (bits[:, half:] & jnp.uint32(0xFFFF0000))


def _store_slabs(ref, words):
    R = words.shape[0]
    for j in range(SLAB):
        ref[pl.ds(j, R, stride=SLAB), :] = words[:, j * 128:(j + 1) * 128]


def _load_slab_halves(ref, R, j):
    w = ref[pl.ds(j, R, stride=SLAB), :]
    return lax.bitcast_convert_type(w << 16, F32), lax.bitcast_convert_type(w & jnp.uint32(0xFFFF0000), F32)


def _load_rows_bf16(ref, R):
    halves = [_load_slab_halves(ref, R, j) for j in range(SLAB)]
    return jnp.concatenate([lo for lo, _ in halves] + [hi for _, hi in halves], axis=1).astype(BF16)


def _slab_copy(src, dst, sem, src_row8, dst_row):
    return pltpu.make_async_copy(src.at[pl.ds(pl.multiple_of(src_row8, SLAB), SLAB), :],
                                 dst.at[pl.ds(dst_row * SLAB, SLAB), :], sem)


def _router_kernel(x_ref, gain_ref, mod_ref, wr_ref, rb_ref, h_ref, e_ref, g_ref, r_ref, cnt_ref, cnt_scr, *, tm):
    E, GS = N_EXPERTS, GROUP_SIZE
    i = pl.program_id(0)

    @pl.when(i == 0)
    def _():
        cnt_scr[...] = jnp.zeros_like(cnt_scr)

    h = _norm_modulate(x_ref[...], gain_ref[...], mod_ref, 3, 4)
    _store_slabs(h_ref, _pack_rows(h))
    h_hi, h_lo = _split_hi_lo(h)
    w_hi, w_lo = _split_hi_lo(wr_ref[...])
    logits = _dot_nt(w_hi, h_hi) + _dot_nt(w_lo, h_hi) + _dot_nt(w_hi, h_lo)
    scores = jax.nn.sigmoid(logits)
    biased = scores + rb_ref[...]
    sub = lax.broadcasted_iota(I32, (GS, tm), 0)
    gscore = []
    for gi in range(N_EXPERT_GROUPS):
        xg = biased[gi * GS:(gi + 1) * GS, :]
        m1 = jnp.max(xg, axis=0, keepdims=True)
        first = jnp.min(jnp.where(xg == m1, sub, GS), axis=0, keepdims=True)
        m2 = jnp.max(jnp.where(sub == first, NEG_INF, xg), axis=0, keepdims=True)
        gscore.append(m1 + m2)
    cur = jnp.concatenate(gscore, axis=0)
    gid = lax.broadcasted_iota(I32, (N_EXPERT_GROUPS, tm), 0)
    gsel = jnp.zeros((N_EXPERT_GROUPS, tm), F32)
    for _ in range(TOPK_GROUPS):
        mx = jnp.max(cur, axis=0, keepdims=True)
        first = jnp.min(jnp.where(cur == mx, gid, N_EXPERT_GROUPS), axis=0, keepdims=True)
        hit = gid == first
        gsel = jnp.where(hit, 1.0, gsel)
        cur = jnp.where(hit, NEG_INF, cur)
    ok = jnp.concatenate(
        [jnp.broadcast_to(gsel[gi:gi + 1, :], (GS, tm)) for gi in range(N_EXPERT_GROUPS)], axis=0)
    masked = jnp.where(ok > 0.5, biased, NEG_INF)
    eid = lax.broadcasted_iota(I32, (E, tm), 0)
    sel = jnp.zeros((E, tm), F32)
    picks, pick_scores = [], []
    for _ in range(TOP_K):
        mx = jnp.max(masked, axis=0, keepdims=True)
        first = jnp.min(jnp.where(masked == mx, eid, E), axis=0, keepdims=True)
        hit = eid == first
        picks.append(first)
        pick_scores.append(jnp.sum(jnp.where(hit, scores, 0.0), axis=0, keepdims=True))
        sel = jnp.where(hit, 1.0, sel)
        masked = jnp.where(hit, NEG_INF, masked)
    total = pick_scores[0]
    for s in pick_scores[1:]:
        total = total + s
    g_ref[...] = jnp.concatenate(pick_scores, axis=0) / total * ROUTED_SCALE
    e_ref[...] = jnp.concatenate(picks, axis=0)
    rr = lax.broadcasted_iota(I32, (tm, tm), 0)
    cc = lax.broadcasted_iota(I32, (tm, tm), 1)
    before = jnp.where(rr < cc, 1.0, 0.0).astype(BF16)
    rank = _dot(sel.astype(BF16), before) + cnt_scr[...]
    r_ref[...] = jnp.concatenate(
        [jnp.sum(jnp.where(eid == p, rank, 0.0), axis=0, keepdims=True) for p in picks], axis=0).astype(I32)
    cnt_scr[...] = cnt_scr[...] + jnp.sum(sel, axis=-1, keepdims=True)
    cnt_ref[...] = cnt_scr[...].astype(I32)


def _router(x, gain, mod, router_w_t, router_b, *, n_prompt, dec_seq, tm):
    N, D = x.shape
    E = N_EXPERTS
    assert D == 2 * SLAB * 128
    assert N % tm == 0 and n_prompt % tm == 0 and dec_seq % tm == 0
    row = functools.partial(_mod_row, tm=tm, n_prompt=n_prompt, dec_seq=dec_seq)
    return pl.pallas_call(
        functools.partial(_router_kernel, tm=tm),
        grid=(N // tm,),
        in_specs=[
            pl.BlockSpec((tm, D), lambda i: (i, 0)),
            pl.BlockSpec((1, D), lambda i: (0, 0)),
            pl.BlockSpec((None, 1, 6 * D), lambda i: (row(i), 0, 0)),
            pl.BlockSpec((E, D), lambda i: (0, 0)),
            pl.BlockSpec((E, 1), lambda i: (0, 0)),
        ],
        out_specs=[
            pl.BlockSpec((tm * SLAB, 128), lambda i: (i, 0)),
            pl.BlockSpec((TOP_K, tm), lambda i: (0, i)),
            pl.BlockSpec((TOP_K, tm), lambda i: (0, i)),
            pl.BlockSpec((TOP_K, tm), lambda i: (0, i)),
            pl.BlockSpec((E, 1), lambda i: (0, 0)),
        ],
        out_shape=[
            jax.ShapeDtypeStruct((N * SLAB, 128), U32),
            jax.ShapeDtypeStruct((TOP_K, N), I32),
            jax.ShapeDtypeStruct((TOP_K, N), F32),
            jax.ShapeDtypeStruct((TOP_K, N), I32),
            jax.ShapeDtypeStruct((E, 1), I32),
        ],
        scratch_shapes=[pltpu.VMEM((E, 1), F32)],
        compiler_params=_cparams(("arbitrary",)),
        name="moe_router",
    )(x, gain.reshape(1, D), mod, router_w_t, router_b.reshape(E, 1))


def _experts_kernel(be_ref, nused_ref, tok_cur, tok_nxt, h_hbm, w1_ref, w3_ref, w2_ref, y_ref,
                    xbuf, sems, w1c, w3c, w2c, prev_e):
    b = pl.program_id(0)
    e = be_ref[b]
    n_used = nused_ref[0]
    BLK = xbuf.shape[1] // SLAB
    slot = b % 2

    FF = w1c.shape[1]
    D = w2c.shape[1]
    CW = 256
    n_chunks = 2 * (FF // CW) + D // CW
    per_chunk = BLK // n_chunks + 1

    def gather(tok_ref, s, lo=0, hi=BLK):
        for r in range(lo, min(hi, BLK)):
            _slab_copy(h_hbm, xbuf.at[s], sems.at[s], tok_ref[0, r], r).start()

    def drain(s):
        for r in range(BLK):
            _slab_copy(h_hbm, xbuf.at[s], sems.at[s], 0, r).wait()

    @pl.when(b == 0)
    def _():
        prev_e[0] = -1
        gather(tok_cur, 0)

    @pl.when(b < n_used)
    def _():
        @pl.when(e != prev_e[0])
        def _():
            w1c[...] = w1_ref[...].astype(BF16)
            w3c[...] = w3_ref[...].astype(BF16)
            w2c[...] = w2_ref[...].astype(BF16)
            prev_e[0] = e

        drain(slot)
        chunk = [0]

        def gather_some():
            gather(tok_nxt, 1 - slot, chunk[0] * per_chunk, (chunk[0] + 1) * per_chunk)
            chunk[0] += 1

        x = _load_rows_bf16(xbuf.at[slot], BLK)
        mids = []
        for c in range(FF // CW):
            sl = slice(c * CW, (c + 1) * CW)
            gather_some()
            a = _dot(x, w1c[:, sl])
            gather_some()
            mids.append((_silu(a) * _dot(x, w3c[:, sl])).astype(BF16))
        mid = jnp.concatenate(mids, axis=1)
        half = D // 2
        for c in range(half // CW):
            gather_some()
            y_lo = _dot(mid, w2c[:, c * CW:(c + 1) * CW])
            gather_some()
            y_hi = _dot(mid, w2c[:, half + c * CW:half + (c + 1) * CW])
            words = _pack_rows(jnp.concatenate([y_lo, y_hi], axis=1))
            for jj in range(CW // 128):
                j = c * (CW // 128) + jj
                y_ref[pl.ds(j, BLK, stride=SLAB), :] = words[:, jj * 128:(jj + 1) * 128]

    @pl.when(b >= n_used)
    def _():
        @pl.when(b == n_used)
        def _():
            drain(slot)

        y_ref[...] = jnp.zeros_like(y_ref)


def _experts(block_expert, n_used, slot_tok8, h, w1, w3, w2, layer):
    n_blocks = block_expert.shape[0]
    BLK = MOE_BLOCK
    _, _, D, FF = w1.shape
    slot_tok3 = slot_tok8.reshape(n_blocks, 1, BLK)
    return pl.pallas_call(
        _experts_kernel,
        grid_spec=pltpu.PrefetchScalarGridSpec(
            num_scalar_prefetch=2,
            grid=(n_blocks,),
            in_specs=[
                pl.BlockSpec((None, 1, BLK), lambda b, be, nu: (b, 0, 0), memory_space=pltpu.SMEM),
                pl.BlockSpec((None, 1, BLK), lambda b, be, nu: (jnp.minimum(b + 1, n_blocks - 1), 0, 0),
                             memory_space=pltpu.SMEM),
                pl.BlockSpec(memory_space=pl.ANY),
                pl.BlockSpec((None, None, D, FF), lambda b, be, nu: (layer, be[b], 0, 0)),
                pl.BlockSpec((None, None, D, FF), lambda b, be, nu: (layer, be[b], 0, 0)),
                pl.BlockSpec((None, None, FF, D), lambda b, be, nu: (layer, be[b], 0, 0)),
            ],
            out_specs=pl.BlockSpec((BLK * SLAB, 128), lambda b, be, nu: (b, 0)),
            scratch_shapes=[
                pltpu.VMEM((2, BLK * SLAB, 128), U32),
                pltpu.SemaphoreType.DMA((2,)),
                pltpu.VMEM((D, FF), BF16),
                pltpu.VMEM((D, FF), BF16),
                pltpu.VMEM((FF, D), BF16),
                pltpu.SMEM((1,), I32),
            ],
        ),
        out_shape=jax.ShapeDtypeStruct((n_blocks * BLK * SLAB, 128), U32),
        compiler_params=_cparams(("arbitrary",)),
        name="moe_experts",
    )(block_expert, n_used, slot_tok3, slot_tok3, h, w1, w3, w2)


def _combine_kernel(dest_cur, dest_nxt, gates_ref, y_hbm, h_ref, x_ref, gate_ref, ws1_ref, ws3_ref, ws2_ref,
                    *rest, prompt_tiles):
    if prompt_tiles is None:
        o_ref, ybuf0, ybuf1, sems = rest
    else:
        op_ref, os_ref, ybuf0, ybuf1, sems = rest
    tm = x_ref.shape[0]
    i = pl.program_id(0)
    last = pl.num_programs(0) - 1

    def gather(dest_ref, buf, sem, lo=0, hi=tm):
        for k in range(TOP_K):
            for r in range(lo, hi):
                _slab_copy(y_hbm, buf.at[k], sem, dest_ref[k, r], r).start()

    def drain(buf, sem):
        for k in range(TOP_K):
            for r in range(tm):
                _slab_copy(y_hbm, buf.at[k], sem, 0, r).wait()

    @pl.when(i == 0)
    def _():
        gather(dest_cur, ybuf0, sems.at[0])

    def step(cur, cur_sem, nxt, nxt_sem):
        drain(cur, cur_sem)
        per_group = tm // (SLAB + 2)
        gather(dest_nxt, nxt, nxt_sem, 0, per_group)
        hb = _load_rows_bf16(h_ref, tm)
        mid = (_silu(_dot(hb, ws1_ref[...])) * _dot(hb, ws3_ref[...])).astype(BF16)
        gather(dest_nxt, nxt, nxt_sem, per_group, 2 * per_group)
        shared = _dot(mid, ws2_ref[...])
        g = [gates_ref[:, k:k + 1] for k in range(TOP_K)]
        lo_parts, hi_parts = [], []
        for j in range(SLAB):
            gather(dest_nxt, nxt, nxt_sem, (j + 2) * per_group, tm if j == SLAB - 1 else (j + 3) * per_group)
            lo_acc = hi_acc = None
            for k in range(TOP_K):
                lo, hi = _load_slab_halves(cur.at[k], tm, j)
                lo_acc = g[k] * lo if lo_acc is None else lo_acc + g[k] * lo
                hi_acc = g[k] * hi if hi_acc is None else hi_acc + g[k] * hi
            lo_parts.append(lo_acc)
            hi_parts.append(hi_acc)
        routed = jnp.concatenate(lo_parts + hi_parts, axis=1)
        out = x_ref[...] + gate_ref[...] * (routed + shared)
        if prompt_tiles is None:
            o_ref[...] = out
        else:
            @pl.when(i < prompt_tiles)
            def _():
                op_ref[...] = out

            @pl.when(i >= prompt_tiles)
            def _():
                os_ref[...] = out

        @pl.when(i == last)
        def _():
            drain(nxt, nxt_sem)

    @pl.when(i % 2 == 0)
    def _():
        step(ybuf0, sems.at[0], ybuf1, sems.at[1])

    @pl.when(i % 2 == 1)
    def _():
        step(ybuf1, sems.at[1], ybuf0, sems.at[0])


def _combine(dest, gates, y, h, x, mod, ws1, ws3, ws2, *, n_prompt, dec_seq, split_outputs):
    N, D = x.shape
    tm = COMBINE_TM
    FF = ws1.shape[1]
    assert N % tm == 0 and n_prompt % tm == 0 and dec_seq % tm == 0
    nt = N // tm
    row = functools.partial(_mod_row, tm=tm, n_prompt=n_prompt, dec_seq=dec_seq)
    dest3 = dest.reshape(TOP_K, nt, tm).transpose(1, 0, 2)
    if split_outputs:
        pt = n_prompt // tm
        out_specs = [pl.BlockSpec((tm, D), lambda i: (jnp.minimum(i, pt - 1), 0)),
                     pl.BlockSpec((tm, D), lambda i: (jnp.maximum(i - pt, 0), 0))]
        out_shape = [jax.ShapeDtypeStruct((n_prompt, D), F32), jax.ShapeDtypeStruct((N - n_prompt, D), F32)]
    else:
        pt = None
        out_specs = pl.BlockSpec((tm, D), lambda i: (i, 0))
        out_shape = jax.ShapeDtypeStruct((N, D), F32)
    return pl.pallas_call(
        functools.partial(_combine_kernel, prompt_tiles=pt),
        grid=(nt,),
        in_specs=[
            pl.BlockSpec((None, TOP_K, tm), lambda i: (i, 0, 0), memory_space=pltpu.SMEM),
            pl.BlockSpec((None, TOP_K, tm), lambda i: (jnp.minimum(i + 1, nt - 1), 0, 0), memory_space=pltpu.SMEM),
            pl.BlockSpec((tm, TOP_K), lambda i: (i, 0)),
            pl.BlockSpec(memory_space=pl.ANY),
            pl.BlockSpec((tm * SLAB, 128), lambda i: (i, 0)),
            pl.BlockSpec((tm, D), lambda i: (i, 0)),
            pl.BlockSpec((None, 1, D), lambda i: (row(i), 0, 5)),
            pl.BlockSpec((D, FF), lambda i: (0, 0)),
            pl.BlockSpec((D, FF), lambda i: (0, 0)),
            pl.BlockSpec((FF, D), lambda i: (0, 0)),
        ],
        out_specs=out_specs,
        out_shape=out_shape,
        scratch_shapes=[pltpu.VMEM((TOP_K, tm * SLAB, 128), U32), pltpu.VMEM((TOP_K, tm * SLAB, 128), U32),
                        pltpu.SemaphoreType.DMA((2,))],
        compiler_params=_cparams(("arbitrary",)),
        name="moe_combine",
    )(dest3, dest3, gates, y, h, x, mod, ws1, ws3, ws2)


def _moe_layer(x, gain, mod, router_w, router_b, w1, w3, w2, layer, ws1, ws3, ws2, *, n_prompt, dec_seq,
               split_outputs=False):
    N = x.shape[0]
    E, BLK = N_EXPERTS, MOE_BLOCK
    h, top_e, gates, rank, counts = _router(x, gain, mod, router_w.T, router_b,
                                            n_prompt=n_prompt, dec_seq=dec_seq,
                                            tm=_row_tile(512, n_prompt, dec_seq))
    counts = counts[:, 0]
    padded = (counts + BLK - 1) // BLK * BLK
    pad_end = jnp.cumsum(padded)
    pad_start = pad_end - padded
    n_blocks = N * TOP_K // BLK + E
    block_expert = jnp.minimum(
        jnp.sum(pad_end[None, :] <= (jnp.arange(n_blocks, dtype=I32) * BLK)[:, None], axis=1), E - 1).astype(I32)
    n_used = (pad_end[-1:] // BLK).astype(I32)
    onehot = top_e[None] == jnp.arange(E, dtype=I32)[:, None, None]
    dest = jnp.sum(jnp.where(onehot, pad_start[:, None, None], 0), axis=0) + rank
    NP = N + BLK
    keys = (top_e * NP + jnp.arange(N, dtype=I32)[None, :]).reshape(-1)
    fill_i = jnp.arange(BLK, dtype=I32)[None, :]
    fill_e = jnp.arange(E, dtype=I32)[:, None]
    fillers = jnp.where(fill_i < (padded - counts)[:, None], fill_e * NP + N + fill_i, E * NP + fill_e * BLK + fill_i)
    slot_t = lax.sort(jnp.concatenate([keys, fillers.reshape(-1)])) % NP
    spread = jnp.arange(n_blocks * BLK, dtype=I32) % N
    slot_tok8 = jnp.where(slot_t < N, slot_t, spread) * SLAB
    y = _experts(block_expert, n_used, slot_tok8, h, w1, w3, w2, layer)
    return _combine(dest * SLAB, gates.T, y, h, x, mod, ws1, ws3, ws2, n_prompt=n_prompt, dec_seq=dec_seq,
                    split_outputs=split_outputs)


def _rope_tables(S):
    quarter = HEAD_DIM // 4
    pos = jnp.arange(S)
    row_id = (pos // GRID_W).astype(F32)
    col_id = (pos % GRID_W).astype(F32)
    inv = ROPE_THETA ** (-jnp.arange(quarter, dtype=F32) / quarter)
    ar, ac = row_id[:, None] * inv, col_id[:, None] * inv
    cos = jnp.concatenate([jnp.cos(ar), jnp.cos(ar), jnp.cos(ac), jnp.cos(ac)], axis=-1)
    sin = jnp.concatenate([-jnp.sin(ar), jnp.sin(ar), -jnp.sin(ac), jnp.sin(ac)], axis=-1)
    return cos, sin


def kernel(x_prompt, x_sample, state_mlstm_C, state_mlstm_n, state_mlstm_m, cache_attn_k, cache_attn_v,
           c, c_ctx, ada_w, ada_b, norm_mix, norm_ffn,
           mlstm_w_in, mlstm_gate_b, mlstm_head_g, mlstm_w_out,
           attn_w_qkv, attn_q_g, attn_k_g, attn_sink, attn_w_o,
           moe_router_w, moe_router_b, moe_w1, moe_w3, moe_w2, shared_w1, shared_w3, shared_w2):
    D = D_MODEL
    Bp, Sp, _ = x_prompt.shape
    Bs, Ss, _ = x_sample.shape
    n_prompt = Bp * Sp
    N = n_prompt + Bs * Ss
    dims = dict(n_prompt=n_prompt, dec_seq=Ss)
    tm512 = _row_tile(512, n_prompt, Ss)

    x = jnp.concatenate([x_prompt.reshape(n_prompt, D), x_sample.reshape(Bs * Ss, D)], axis=0)
    rows = 16
    cvec = jnp.zeros((rows, D), F32).at[0].set(c_ctx).at[1:1 + Bs].set(c)
    mod_all = _ada_table(cvec, ada_w, ada_b)[:, :1 + Bs].reshape(DEPTH, 1 + Bs, 1, 6 * D)

    H = MLSTM_HEADS
    mod = mod_all[0]
    w_in = mlstm_w_in[0]
    w_main = w_in[:, :MLSTM_MAIN].astype(BF16)
    w_gate = jnp.pad(w_in[:, MLSTM_MAIN:], ((0, 0), (0, GATE_PAD - 4 * H))).astype(BF16)
    gate_b = jnp.pad(mlstm_gate_b[0], (0, GATE_PAD - 4 * H))
    col_scale = jnp.concatenate([jnp.full((MLSTM_QK,), MLSTM_DK ** -0.5, F32),
                                 jnp.ones((MLSTM_MAIN - MLSTM_QK,), F32)])
    qkvo, gates = _nm_matmul(x, norm_mix[0], mod, w_main, col_scale, jnp.zeros((MLSTM_MAIN,), F32),
                             shift_idx=0, scale_idx=1, out_dtype=BF16, tm=tm512, tn=1024,
                             aux_w=w_gate, aux_b=gate_b, name="mlstm_proj", **dims)
    L = 256
    state = (state_mlstm_C[:, 0], state_mlstm_n[:, 0], state_mlstm_m[:, 0])
    hf, hb, C_p, n_p, m_p = _mlstm_scan(qkvo, gates, Bp, Sp, Bs, Ss, L, state)
    x = _mm_residual((hf, hb, qkvo, mlstm_head_g[0]), mlstm_w_out[0].astype(BF16), x, mod, gate_idx=2,
                     tm=tm512, tn=1024, mlstm_prologue=True, name="mlstm_out", **dims)
    x = _moe_layer(x, norm_ffn[0], mod, moe_router_w[0], moe_router_b[0], moe_w1, moe_w3, moe_w2, 0,
                   shared_w1[0].astype(BF16), shared_w3[0].astype(BF16), shared_w2[0].astype(BF16), **dims)

    mod = mod_all[1]
    qkv = _nm_matmul(x, norm_mix[1], mod, attn_w_qkv[0].astype(BF16), jnp.ones((ATTN_PROJ,), F32),
                     jnp.zeros((ATTN_PROJ,), F32), shift_idx=0, scale_idx=1, out_dtype=F32, tm=tm512, tn=1024,
                     name="attn_qkv", **dims)
    ident_cos = jnp.ones((Sp, HEAD_DIM), F32)
    ident_sin = jnp.zeros((Sp, HEAD_DIM), F32)
    q_p, k_p, v_p, kf_p, vf_p = _qk_prep(qkv, attn_q_g[0], attn_k_g[0], ident_cos, ident_sin, 0, n_prompt, Sp,
                                         True, "qk_prep_prompt")
    cos, sin = _rope_tables(Ss)
    q_s, k_s, v_s = _qk_prep(qkv, attn_q_g[0], attn_k_g[0], cos, sin, n_prompt, Bs * Ss, 256, False,
                             "qk_prep_latent")
    o_p = _attn_context(attn_sink[0], q_p, k_p, v_p, Bp, Sp)
    P = cache_attn_k.shape[2]
    ctx_k = cache_attn_k[:, 0].reshape(Bs, P, ATTN_KV).astype(BF16)
    ctx_v = cache_attn_v[:, 0].reshape(Bs, P, ATTN_KV).astype(BF16)
    o_s = _attn_latent(attn_sink[0], q_s, k_s, v_s, ctx_k, ctx_v, Bs, Ss)
    x = _mm_residual((o_p, o_s), attn_w_o[0].astype(BF16), x, mod, gate_idx=2,
                     tm=_row_tile(1024, n_prompt, Ss), tn=1024,
                     mlstm_prologue=False, name="attn_out", **dims)
    y_prompt, y_sample = _moe_layer(
        x, norm_ffn[1], mod, moe_router_w[1], moe_router_b[1], moe_w1, moe_w3, moe_w2, 1,
        shared_w1[1].astype(BF16), shared_w3[1].astype(BF16), shared_w2[1].astype(BF16), split_outputs=True, **dims)
    return (y_prompt.reshape(Bp, Sp, D), y_sample.reshape(Bs, Ss, D), C_p[:, None], n_p[:, None], m_p[:, None],
            kf_p.reshape(Bp, 1, Sp, ATTN_KV_HEADS, HEAD_DIM), vf_p.reshape(Bp, 1, Sp, ATTN_KV_HEADS, HEAD_DIM))
```

```python
import functools

import jax
import jax.numpy as jnp
from jax import lax
from jax.experimental import pallas as pl
from jax.experimental.pallas import tpu as pltpu

F32 = jnp.float32
BF16 = jnp.bfloat16
I32 = jnp.int32

D_MODEL = 2048
DEPTH = 2
EPS = 1e-6
GRID_W = 64
MLSTM_HEADS = 8
MLSTM_DK = 128
MLSTM_DV = 256
MLSTM_QK = MLSTM_HEADS * MLSTM_DK
MLSTM_V = MLSTM_HEADS * MLSTM_DV
MLSTM_MAIN = 2 * MLSTM_QK + 2 * MLSTM_V
GATE_PAD = 128
HEAD_DIM = 128
ATTN_Q_HEADS = 16
ATTN_KV_HEADS = 4
ATTN_GROUPS = 4
WINDOW = 128
QBLK = 128
ROPE_THETA = 10000.0
ATTN_Q = ATTN_Q_HEADS * HEAD_DIM
ATTN_KV = ATTN_KV_HEADS * HEAD_DIM
ATTN_PROJ = ATTN_Q + 2 * ATTN_KV
N_EXPERTS = 64
TOP_K = 8
N_EXPERT_GROUPS = 8
TOPK_GROUPS = 4
GROUP_SIZE = N_EXPERTS // N_EXPERT_GROUPS
EXPERT_FF = 512
ROUTED_SCALE = 2.5
MOE_BLOCK = 512
COMBINE_TM = 128

V7X_VMEM_LIMIT = 56 * 1024 * 1024
NEG_INF = float("-inf")


def _cparams(sem):
    return pltpu.CompilerParams(dimension_semantics=("arbitrary",) * len(sem), vmem_limit_bytes=V7X_VMEM_LIMIT)


def _split_hi_lo(a):
    hi = a.astype(BF16)
    lo = (a - hi.astype(F32)).astype(BF16)
    return hi, lo


def _dot(a, b):
    return jnp.dot(a, b, preferred_element_type=F32)


def _dot_nt(a, b):
    return lax.dot_general(a, b, (((1,), (1,)), ((), ())), preferred_element_type=F32)


def _dot_tn(a, b):
    return lax.dot_general(a, b, (((0,), (0,)), ((), ())), preferred_element_type=F32)


def _silu(x):
    return x * jax.nn.sigmoid(x)


def _row_tile(preferred, n_prompt, dec_seq):
    tm = preferred
    while n_prompt % tm or dec_seq % tm:
        tm //= 2
    return tm


def _mod_row(i, tm, n_prompt, dec_seq):
    r0 = i * tm
    return jnp.where(r0 < n_prompt, 0, 1 + (r0 - n_prompt) // dec_seq)


def _norm_modulate(x, gain, mod_ref, shift_idx, scale_idx):
    D = D_MODEL
    y = x * lax.rsqrt(jnp.mean(x * x, axis=-1, keepdims=True) + EPS) * gain
    shift = mod_ref[:, shift_idx * D:(shift_idx + 1) * D]
    scale = mod_ref[:, scale_idx * D:(scale_idx + 1) * D]
    return y * (1.0 + scale) + shift


def _ada_kernel(c_ref, w_ref, b_ref, o_ref):
    s = _silu(c_ref[...])
    s_hi, s_lo = _split_hi_lo(s)
    w_hi, w_lo = _split_hi_lo(w_ref[...])
    o_ref[...] = _dot(s_hi, w_hi) + _dot(s_hi, w_lo) + _dot(s_lo, w_hi) + b_ref[...]


def _ada_table(cvec, ada_w, ada_b):
    D = D_MODEL
    tn = 1024
    rows = cvec.shape[0]
    return pl.pallas_call(
        _ada_kernel,
        grid=(DEPTH, 6 * D // tn),
        in_specs=[
            pl.BlockSpec((rows, D), lambda l, j: (0, 0)),
            pl.BlockSpec((None, D, tn), lambda l, j: (l, 0, j)),
            pl.BlockSpec((None, 1, tn), lambda l, j: (l, 0, j)),
        ],
        out_specs=pl.BlockSpec((None, rows, tn), lambda l, j: (l, 0, j)),
        out_shape=jax.ShapeDtypeStruct((DEPTH, rows, 6 * D), F32),
        compiler_params=_cparams(("parallel", "parallel")),
        name="ada_table",
    )(cvec, ada_w, ada_b.reshape(DEPTH, 1, 6 * D))


def _nm_matmul_kernel(*refs, shift_idx, scale_idx, has_aux, prompt_tiles):
    if prompt_tiles is None:
        x_ref, gain_ref, mod_ref, w_ref, cs_ref, cb_ref, *rest = refs
    else:
        xp_ref, xs_ref, gain_ref, mod_ref, w_ref, cs_ref, cb_ref, *rest = refs
    if has_aux:
        wa_ref, ab_ref, o_ref, aux_ref, *tail = rest
    else:
        o_ref, *tail = rest
    if prompt_tiles is None:
        (h_scr,) = tail
    else:
        xcat_ref, h_scr = tail

    def prologue(x):
        h = _norm_modulate(x, gain_ref[...], mod_ref, shift_idx, scale_idx).astype(BF16)
        h_scr[...] = h
        if has_aux:
            aux_ref[...] = _dot(h, wa_ref[...]) + ab_ref[...]
        if prompt_tiles is not None:
            xcat_ref[...] = x

    first_col = pl.program_id(1) == 0
    if prompt_tiles is None:
        @pl.when(first_col)
        def _():
            prologue(x_ref[...])
    else:
        context = pl.program_id(0) < prompt_tiles

        @pl.when(first_col & context)
        def _():
            prologue(xp_ref[...])

        @pl.when(first_col & jnp.logical_not(context))
        def _():
            prologue(xs_ref[...])

    acc = _dot(h_scr[...], w_ref[...])
    o_ref[...] = (acc * cs_ref[...] + cb_ref[...]).astype(o_ref.dtype)


def _nm_matmul(x, gain, mod, w, col_scale, col_bias, *, shift_idx, scale_idx, n_prompt, dec_seq,
               out_dtype, tm, tn, aux_w=None, aux_b=None, name):
    split = isinstance(x, tuple)
    D = w.shape[0]
    N = sum(p.shape[0] for p in x) if split else x.shape[0]
    P = w.shape[1]
    assert N % tm == 0 and P % tn == 0 and n_prompt % tm == 0 and dec_seq % tm == 0
    has_aux = aux_w is not None
    row = functools.partial(_mod_row, tm=tm, n_prompt=n_prompt, dec_seq=dec_seq)
    pt = n_prompt // tm
    if split:
        x_specs = [pl.BlockSpec((tm, D), lambda i, j: (jnp.minimum(i, pt - 1), 0)),
                   pl.BlockSpec((tm, D), lambda i, j: (jnp.maximum(i - pt, 0), 0))]
        x_args = list(x)
    else:
        x_specs = [pl.BlockSpec((tm, D), lambda i, j: (i, 0))]
        x_args = [x]
    in_specs = x_specs + [
        pl.BlockSpec((1, D), lambda i, j: (0, 0)),
        pl.BlockSpec((None, 1, 6 * D), lambda i, j: (row(i), 0, 0)),
        pl.BlockSpec((D, tn), lambda i, j: (0, j)),
        pl.BlockSpec((1, tn), lambda i, j: (0, j)),
        pl.BlockSpec((1, tn), lambda i, j: (0, j)),
    ]
    args = x_args + [gain.reshape(1, D), mod, w, col_scale.reshape(1, P), col_bias.reshape(1, P)]
    out_specs = [pl.BlockSpec((tm, tn), lambda i, j: (i, j))]
    out_shape = [jax.ShapeDtypeStruct((N, P), out_dtype)]
    if has_aux:
        PA = aux_w.shape[1]
        in_specs += [pl.BlockSpec((D, PA), lambda i, j: (0, 0)), pl.BlockSpec((1, PA), lambda i, j: (0, 0))]
        args += [aux_w, aux_b.reshape(1, PA)]
        out_specs.append(pl.BlockSpec((tm, PA), lambda i, j: (i, 0)))
        out_shape.append(jax.ShapeDtypeStruct((N, PA), F32))
    if split:
        out_specs.append(pl.BlockSpec((tm, D), lambda i, j: (i, 0)))
        out_shape.append(jax.ShapeDtypeStruct((N, D), F32))
    return pl.pallas_call(
        functools.partial(_nm_matmul_kernel, shift_idx=shift_idx, scale_idx=scale_idx, has_aux=has_aux,
                          prompt_tiles=pt if split else None),
        grid=(N // tm, P // tn),
        in_specs=in_specs,
        out_specs=out_specs,
        out_shape=out_shape,
        scratch_shapes=[pltpu.VMEM((tm, D), BF16)],
        compiler_params=_cparams(("parallel", "arbitrary")),
        name=name,
    )(*args)


def _mm_residual_kernel(*refs, mlstm_prologue, prompt_tiles):
    if mlstm_prologue:
        hf_ref, hb_ref, og_ref, hg_ref, w_ref, x_ref, gate_ref, o_ref, l_scr = refs

        @pl.when(pl.program_id(1) == 0)
        def _():
            hs = hf_ref[...].astype(F32) + hb_ref[...].astype(F32)
            og = jax.nn.sigmoid(og_ref[...].astype(F32))
            for h in range(MLSTM_HEADS):
                sl = slice(h * MLSTM_DV, (h + 1) * MLSTM_DV)
                hh = hs[:, sl]
                hn = hh * lax.rsqrt(jnp.mean(hh * hh, axis=-1, keepdims=True) + EPS) * hg_ref[:, sl]
                l_scr[:, sl] = (og[:, sl] * hn).astype(BF16)

        o_ref[...] = x_ref[...] + gate_ref[...] * _dot(l_scr[...], w_ref[...])
    else:
        lp_ref, ls_ref, w_ref, x_ref, gate_ref, o_ref = refs
        context = pl.program_id(0) < prompt_tiles

        @pl.when(context)
        def _():
            o_ref[...] = x_ref[...] + gate_ref[...] * _dot(lp_ref[...], w_ref[...])

        @pl.when(jnp.logical_not(context))
        def _():
            o_ref[...] = x_ref[...] + gate_ref[...] * _dot(ls_ref[...], w_ref[...])


def _mm_residual(lhs_args, w, x, mod, *, gate_idx, n_prompt, dec_seq, tm, tn, mlstm_prologue, name):
    N, D = x.shape
    K = w.shape[0]
    assert N % tm == 0 and D % tn == 0 and n_prompt % tm == 0 and dec_seq % tm == 0
    row = functools.partial(_mod_row, tm=tm, n_prompt=n_prompt, dec_seq=dec_seq)
    pt = n_prompt // tm
    if mlstm_prologue:
        hf, hb, qkvo, head_g = lhs_args
        o_blk = (2 * MLSTM_QK + MLSTM_V) // MLSTM_V
        in_specs = [
            pl.BlockSpec((tm, K), lambda i, j: (i, 0)),
            pl.BlockSpec((tm, K), lambda i, j: (i, 0)),
            pl.BlockSpec((tm, MLSTM_V), lambda i, j: (i, o_blk)),
            pl.BlockSpec((1, K), lambda i, j: (0, 0)),
        ]
        args = [hf, hb, qkvo, head_g.reshape(1, K)]
        scratch = [pltpu.VMEM((tm, K), BF16)]
    else:
        lhs_p, lhs_s = lhs_args
        in_specs = [pl.BlockSpec((tm, K), lambda i, j: (jnp.minimum(i, pt - 1), 0)),
                    pl.BlockSpec((tm, K), lambda i, j: (jnp.maximum(i - pt, 0), 0))]
        args = [lhs_p, lhs_s]
        scratch = []
    in_specs += [
        pl.BlockSpec((K, tn), lambda i, j: (0, j)),
        pl.BlockSpec((tm, tn), lambda i, j: (i, j)),
        pl.BlockSpec((None, 1, tn), lambda i, j: (row(i), 0, gate_idx * (D // tn) + j)),
    ]
    args += [w, x, mod]
    return pl.pallas_call(
        functools.partial(_mm_residual_kernel, mlstm_prologue=mlstm_prologue, prompt_tiles=pt),
        grid=(N // tm, D // tn),
        in_specs=in_specs,
        out_specs=pl.BlockSpec((tm, tn), lambda i, j: (i, j)),
        out_shape=jax.ShapeDtypeStruct((N, D), F32),
        scratch_shapes=scratch,
        compiler_params=_cparams(("parallel", "arbitrary")),
        name=name,
    )(*args)


def _log_sigmoid(x):
    return jnp.minimum(x, 0.0) - jnp.log(1.0 + jnp.exp(-jnp.abs(x)))


def _mlstm_direction(d, q_ref, k_ref, v_ref, g_ref, h_ref, C_scr, n_scr, m_scr, L):
    H, DK, DV = MLSTM_HEADS, MLSTM_DK, MLSTM_DV
    g = g_ref[...]
    lf = _log_sigmoid(g)
    r = lax.broadcasted_iota(I32, (L, L), 0)
    c = lax.broadcasted_iota(I32, (L, L), 1)
    causal = (c <= r) if d == 0 else (c >= r)
    tri = jnp.where(causal, 1.0, 0.0).astype(BF16)
    lf1 = lf.astype(BF16)
    rem = lf - lf1.astype(F32)
    lf2 = rem.astype(BF16)
    lf3 = (rem - lf2.astype(F32)).astype(BF16)
    bsum = _dot(tri, lf1) + _dot(tri, lf2) + _dot(tri, lf3)
    g_t = g.T
    b_t = bsum.T
    end = L - 1 if d == 0 else 0
    m_old = [m_scr[d, h] for h in range(H)]
    n_old = [n_scr[d, h] for h in range(H)]
    m_upd, n_upd = [], []
    for h in range(H):
        ci, cf = d * 2 * H + h, d * 2 * H + H + h
        b_col, i_col = bsum[:, cf:cf + 1], g[:, ci:ci + 1]
        b_row, i_row = b_t[cf:cf + 1, :], g_t[ci:ci + 1, :]
        b_end = bsum[end:end + 1, cf:cf + 1]
        m, n = m_old[h], n_old[h]
        C = C_scr[d * H + h][...]
        qh = q_ref[:, h * DK:(h + 1) * DK]
        kh = k_ref[:, h * DK:(h + 1) * DK]
        vh = v_ref[:, h * DV:(h + 1) * DV]
        dmat = jnp.where(causal, b_col - b_row + i_row, NEG_INF)
        inter = b_col + m
        m_out = jnp.maximum(inter, jnp.max(dmat, axis=-1, keepdims=True))
        w = jnp.exp(dmat - m_out) * _dot_nt(qh, kh)
        dec = jnp.exp(inter - m_out)
        num = _dot(w.astype(BF16), vh) + dec * _dot(qh, C.astype(BF16))
        den = jnp.sum(w, axis=-1, keepdims=True) + dec * jnp.sum(qh.astype(F32) * n, axis=-1, keepdims=True)
        h_ref[:, h * DV:(h + 1) * DV] = (num / jnp.maximum(jnp.abs(den), jnp.exp(-m_out))).astype(h_ref.dtype)
        to_end = b_end - b_col + i_col
        m_new = jnp.maximum(b_end + m, jnp.max(to_end, axis=0, keepdims=True))
        wk = jnp.exp(to_end - m_new)
        dec_end = jnp.exp(b_end + m - m_new)
        kw = kh.astype(F32) * wk
        C_scr[d * H + h][...] = dec_end * C + _dot_tn(kw.astype(BF16), vh)
        n_upd.append(dec_end * n + jnp.sum(kw, axis=0, keepdims=True))
        m_upd.append(m_new)
    for h in range(H):
        n_scr[d, h] = n_upd[h]
        m_scr[d, h] = m_upd[h]


def _mlstm_kernel(fwd_ref, bwd_ref, bidx_ref, flag_ref, qf, kf, vf, gf, qb, kb, vb, gb, C0, n0, m0,
                  hf, hb, Co, no, mo, *scratch, L):
    H = MLSTM_HEADS
    C_scr, (n_scr, m_scr) = scratch[:2 * H], scratch[2 * H:]
    flags = flag_ref[pl.program_id(0)]
    first, last, context = (flags & 1) != 0, (flags & 2) != 0, (flags & 4) != 0

    @pl.when(first & context)
    def _():
        for C in C_scr:
            C[...] = jnp.zeros_like(C)
        n_scr[...] = jnp.zeros_like(n_scr)
        m_scr[...] = jnp.zeros_like(m_scr)

    @pl.when(first & jnp.logical_not(context))
    def _():
        for d in range(2):
            for h in range(H):
                C_scr[d * H + h][...] = C0[d, h]
                n_scr[d, h] = n0[d, h:h + 1, :]
                m_scr[d, h] = m0[d:d + 1, h:h + 1]

    _mlstm_direction(0, qf, kf, vf, gf, hf, C_scr, n_scr, m_scr, L)
    _mlstm_direction(1, qb, kb, vb, gb, hb, C_scr, n_scr, m_scr, L)

    @pl.when(last & context)
    def _():
        for d in range(2):
            for h in range(H):
                Co[d, h] = C_scr[d * H + h][...]
                no[d, h:h + 1, :] = n_scr[d, h]
                mo[d:d + 1, h:h + 1] = m_scr[d, h]


def _mlstm_scan(qkvo, gates, Bp, Sp, Bs, Ss, L, state):
    H, DK, DV = MLSTM_HEADS, MLSTM_DK, MLSTM_DV
    assert Sp % L == 0 and Ss % L == 0
    fwd, bwd, bidx, flags = [], [], [], []
    for context, B, S, base in ((1, Bp, Sp, 0), (0, Bs, Ss, Bp * Sp // L)):
        nc = S // L
        for b in range(B):
            for c in range(nc):
                fwd.append(base + b * nc + c)
                bwd.append(base + b * nc + nc - 1 - c)
                bidx.append(b)
                flags.append((c == 0) * 1 + (c == nc - 1) * 2 + context * 4)
    tables = [jnp.asarray(t, I32) for t in (fwd, bwd, bidx, flags)]
    lat_b = lambda s, fl, bi: jnp.where((fl[s] & 4) != 0, 0, bi[s])
    ctx_b = lambda s, fl, bi: jnp.where((fl[s] & 4) != 0, bi[s], Bp - 1)

    def specs(tbl):
        return [
            pl.BlockSpec((L, MLSTM_QK), lambda s, f, w, bi, fl: ((f, w)[tbl][s], 0)),
            pl.BlockSpec((L, MLSTM_QK), lambda s, f, w, bi, fl: ((f, w)[tbl][s], 1)),
            pl.BlockSpec((L, MLSTM_V), lambda s, f, w, bi, fl: ((f, w)[tbl][s], 1)),
            pl.BlockSpec((L, GATE_PAD), lambda s, f, w, bi, fl: ((f, w)[tbl][s], 0)),
        ]

    def state_specs(which):
        return [
            pl.BlockSpec((None, 2, H, DK, DV), lambda s, f, w, bi, fl: (which(s, fl, bi), 0, 0, 0, 0)),
            pl.BlockSpec((None, 2, H, DK), lambda s, f, w, bi, fl: (which(s, fl, bi), 0, 0, 0)),
            pl.BlockSpec((None, 2, H), lambda s, f, w, bi, fl: (which(s, fl, bi), 0, 0)),
        ]

    N = qkvo.shape[0]
    return pl.pallas_call(
        functools.partial(_mlstm_kernel, L=L),
        grid_spec=pltpu.PrefetchScalarGridSpec(
            num_scalar_prefetch=4,
            grid=(len(fwd),),
            in_specs=specs(0) + specs(1) + state_specs(lat_b),
            out_specs=[
                pl.BlockSpec((L, MLSTM_V), lambda s, f, w, bi, fl: (f[s], 0)),
                pl.BlockSpec((L, MLSTM_V), lambda s, f, w, bi, fl: (w[s], 0)),
            ] + state_specs(ctx_b),
            scratch_shapes=[pltpu.VMEM((DK, DV), F32)] * (2 * H) + [
                pltpu.VMEM((2, H, 1, DK), F32),
                pltpu.VMEM((2, H, 1, 1), F32),
            ],
        ),
        out_shape=[
            jax.ShapeDtypeStruct((N, MLSTM_V), BF16),
            jax.ShapeDtypeStruct((N, MLSTM_V), BF16),
            jax.ShapeDtypeStruct((Bp, 2, H, DK, DV), F32),
            jax.ShapeDtypeStruct((Bp, 2, H, DK), F32),
            jax.ShapeDtypeStruct((Bp, 2, H), F32),
        ],
        compiler_params=_cparams(("arbitrary",)),
        name="mlstm_scan",
    )(*tables, qkvo, qkvo, qkvo, gates, qkvo, qkvo, qkvo, gates, *state)


def _qk_prep_kernel(qkv_ref, qg_ref, kg_ref, cos_ref, sin_ref, *outs, emit_f32):
    if emit_f32:
        q_ref, k_ref, v_ref, kf_ref, vf_ref = outs
    else:
        q_ref, k_ref, v_ref = outs
    cos = cos_ref[...]
    sin = sin_ref[...]
    lane = lax.broadcasted_iota(I32, cos.shape, 1)
    first = (lane % (HEAD_DIM // 2)) < (HEAD_DIM // 4)

    def norm_rope(x, gain):
        xn = x * lax.rsqrt(jnp.mean(x * x, axis=-1, keepdims=True) + EPS) * gain
        partner = jnp.where(first, pltpu.roll(xn, HEAD_DIM - HEAD_DIM // 4, 1), pltpu.roll(xn, HEAD_DIM // 4, 1))
        return xn * cos + partner * sin

    for h in range(ATTN_Q_HEADS):
        sl = slice(h * HEAD_DIM, (h + 1) * HEAD_DIM)
        q_ref[:, sl] = (norm_rope(qkv_ref[:, sl], qg_ref[...]) * HEAD_DIM ** -0.5).astype(BF16)
    for h in range(ATTN_KV_HEADS):
        sl = slice(h * HEAD_DIM, (h + 1) * HEAD_DIM)
        kx = norm_rope(qkv_ref[:, ATTN_Q + h * HEAD_DIM:ATTN_Q + (h + 1) * HEAD_DIM], kg_ref[...])
        k_ref[:, sl] = kx.astype(BF16)
        if emit_f32:
            kf_ref[:, sl] = kx
    vx = qkv_ref[:, ATTN_Q + ATTN_KV:]
    v_ref[...] = vx.astype(BF16)
    if emit_f32:
        vf_ref[...] = vx


def _qk_prep(qkv, q_g, k_g, cos, sin, row0, rows, tm, emit_f32, name):
    assert rows % tm == 0 and row0 % tm == 0 and cos.shape[0] % tm == 0
    base = row0 // tm
    nt = cos.shape[0] // tm
    out_specs = [
        pl.BlockSpec((tm, ATTN_Q), lambda i: (i, 0)),
        pl.BlockSpec((tm, ATTN_KV), lambda i: (i, 0)),
        pl.BlockSpec((tm, ATTN_KV), lambda i: (i, 0)),
    ]
    out_shape = [
        jax.ShapeDtypeStruct((rows, ATTN_Q), BF16),
        jax.ShapeDtypeStruct((rows, ATTN_KV), BF16),
        jax.ShapeDtypeStruct((rows, ATTN_KV), BF16),
    ]
    if emit_f32:
        out_specs += [pl.BlockSpec((tm, ATTN_KV), lambda i: (i, 0))] * 2
        out_shape += [jax.ShapeDtypeStruct((rows, ATTN_KV), F32)] * 2
    return pl.pallas_call(
        functools.partial(_qk_prep_kernel, emit_f32=emit_f32),
        grid=(rows // tm,),
        in_specs=[
            pl.BlockSpec((tm, ATTN_PROJ), lambda i: (base + i, 0)),
            pl.BlockSpec((1, HEAD_DIM), lambda i: (0, 0)),
            pl.BlockSpec((1, HEAD_DIM), lambda i: (0, 0)),
            pl.BlockSpec((tm, HEAD_DIM), lambda i: (i % nt, 0)),
            pl.BlockSpec((tm, HEAD_DIM), lambda i: (i % nt, 0)),
        ],
        out_specs=out_specs,
        out_shape=out_shape,
        compiler_params=_cparams(("parallel",)),
        name=name,
    )(qkv, q_g.reshape(1, HEAD_DIM), k_g.reshape(1, HEAD_DIM), cos, sin)


def _sink_column(sink_ref, kv, rows_per_head):
    parts = [jnp.full((rows_per_head, 1), sink_ref[kv * ATTN_GROUPS + g], F32) for g in range(ATTN_GROUPS)]
    return jnp.concatenate(parts, axis=0)


def _attn_ctx_kernel(sink_ref, q_ref, k_ref, v_ref, o_ref):
    S = q_ref.shape[0]
    for kv in range(ATTN_KV_HEADS):
        q = jnp.concatenate(
            [q_ref[:, (kv * ATTN_GROUPS + g) * HEAD_DIM:(kv * ATTN_GROUPS + g + 1) * HEAD_DIM]
             for g in range(ATTN_GROUPS)], axis=0)
        ksl = slice(kv * HEAD_DIM, (kv + 1) * HEAD_DIM)
        s = _dot_nt(q, k_ref[:, ksl])
        sk = _sink_column(sink_ref, kv, S)
        m = jnp.maximum(jnp.max(s, axis=-1, keepdims=True), sk)
        p = jnp.exp(s - m)
        den = jnp.sum(p, axis=-1, keepdims=True) + jnp.exp(sk - m)
        o = _dot(p.astype(BF16), v_ref[:, ksl]) * (1.0 / den)
        for g in range(ATTN_GROUPS):
            hq = kv * ATTN_GROUPS + g
            o_ref[:, hq * HEAD_DIM:(hq + 1) * HEAD_DIM] = o[g * S:(g + 1) * S].astype(BF16)


def _attn_context(sink, q, k, v, B, S):
    return pl.pallas_call(
        _attn_ctx_kernel,
        grid_spec=pltpu.PrefetchScalarGridSpec(
            num_scalar_prefetch=0,
            grid=(B,),
            in_specs=[
                pl.BlockSpec(memory_space=pltpu.SMEM),
                pl.BlockSpec((S, ATTN_Q), lambda b: (b, 0)),
                pl.BlockSpec((S, ATTN_KV), lambda b: (b, 0)),
                pl.BlockSpec((S, ATTN_KV), lambda b: (b, 0)),
            ],
            out_specs=pl.BlockSpec((S, ATTN_Q), lambda b: (b, 0)),
        ),
        out_shape=jax.ShapeDtypeStruct((B * S, ATTN_Q), BF16),
        compiler_params=_cparams(("parallel",)),
        name="attn_context",
    )(sink, q, k, v)


def _attn_lat_kernel(sink_ref, q_ref, kp_ref, kc_ref, kn_ref, vp_ref, vc_ref, vn_ref, kx_ref, vx_ref, o_ref):
    i = pl.program_id(1)
    nb = pl.num_programs(1)
    R = ATTN_GROUPS * QBLK
    r = lax.broadcasted_iota(I32, (R, QBLK), 0) % QBLK
    c = lax.broadcasted_iota(I32, (R, QBLK), 1)
    ok_prev = c >= r
    ok_next = c <= r
    edge_prev = jnp.where(i > 0, 0.0, NEG_INF)
    edge_next = jnp.where(i < nb - 1, 0.0, NEG_INF)
    for kv in range(ATTN_KV_HEADS):
        q = jnp.concatenate(
            [q_ref[:, (kv * ATTN_GROUPS + g) * HEAD_DIM:(kv * ATTN_GROUPS + g + 1) * HEAD_DIM]
             for g in range(ATTN_GROUPS)], axis=0)
        ksl = slice(kv * HEAD_DIM, (kv + 1) * HEAD_DIM)
        s_p = jnp.where(ok_prev, _dot_nt(q, kp_ref[:, ksl]) + edge_prev, NEG_INF)
        s_c = _dot_nt(q, kc_ref[:, ksl])
        s_n = jnp.where(ok_next, _dot_nt(q, kn_ref[:, ksl]) + edge_next, NEG_INF)
        s_x = _dot_nt(q, kx_ref[:, ksl])
        sk = _sink_column(sink_ref, kv, QBLK)
        m = jnp.maximum(
            jnp.maximum(jnp.max(s_p, axis=-1, keepdims=True), jnp.max(s_c, axis=-1, keepdims=True)),
            jnp.maximum(jnp.max(s_n, axis=-1, keepdims=True), jnp.max(s_x, axis=-1, keepdims=True)))
        m = jnp.maximum(m, sk)
        p_p, p_c, p_n, p_x = jnp.exp(s_p - m), jnp.exp(s_c - m), jnp.exp(s_n - m), jnp.exp(s_x - m)
        den = (jnp.sum(p_p, axis=-1, keepdims=True) + jnp.sum(p_c, axis=-1, keepdims=True)
               + jnp.sum(p_n, axis=-1, keepdims=True) + jnp.sum(p_x, axis=-1, keepdims=True) + jnp.exp(sk - m))
        o = (_dot(p_p.astype(BF16), vp_ref[:, ksl]) + _dot(p_c.astype(BF16), vc_ref[:, ksl])
             + _dot(p_n.astype(BF16), vn_ref[:, ksl]) + _dot(p_x.astype(BF16), vx_ref[:, ksl])) * (1.0 / den)
        for g in range(ATTN_GROUPS):
            hq = kv * ATTN_GROUPS + g
            o_ref[:, hq * HEAD_DIM:(hq + 1) * HEAD_DIM] = o[g * QBLK:(g + 1) * QBLK].astype(BF16)


def _attn_latent(sink, q, k, v, ctx_k, ctx_v, B, S):
    nb = S // QBLK
    P = ctx_k.shape[1]
    prev = lambda b, i: (b * nb + jnp.maximum(i - 1, 0), 0)
    cur = lambda b, i: (b * nb + i, 0)
    nxt = lambda b, i: (b * nb + jnp.minimum(i + 1, nb - 1), 0)
    kv_spec = lambda f: pl.BlockSpec((QBLK, ATTN_KV), f)
    return pl.pallas_call(
        _attn_lat_kernel,
        grid_spec=pltpu.PrefetchScalarGridSpec(
            num_scalar_prefetch=0,
            grid=(B, nb),
            in_specs=[
                pl.BlockSpec(memory_space=pltpu.SMEM),
                pl.BlockSpec((QBLK, ATTN_Q), cur),
                kv_spec(prev), kv_spec(cur), kv_spec(nxt),
                kv_spec(prev), kv_spec(cur), kv_spec(nxt),
                pl.BlockSpec((None, P, ATTN_KV), lambda b, i: (b, 0, 0)),
                pl.BlockSpec((None, P, ATTN_KV), lambda b, i: (b, 0, 0)),
            ],
            out_specs=pl.BlockSpec((QBLK, ATTN_Q), cur),
        ),
        out_shape=jax.ShapeDtypeStruct((B * S, ATTN_Q), BF16),
        compiler_params=_cparams(("parallel", "parallel")),
        name="attn_latent",
    )(sink, q, k, k, k, v, v, v, ctx_k, ctx_v)


SLAB = 8
U32 = jnp.uint32


def _pack_rows(x):
    half = x.shape[1] // 2
    bits = lax.bitcast_convert_type(x.astype(BF16).astype(F32), U32)
    return (bits[:, :half] >> 16) | (bits[:, half:] & jnp.uint32(0xFFFF0000))


def _store_slabs(ref, words):
    R = words.shape[0]
    for j in range(SLAB):
        ref[pl.ds(j, R, stride=SLAB), :] = words[:, j * 128:(j + 1) * 128]


def _load_slab_halves(ref, R, j):
    w = ref[pl.ds(j, R, stride=SLAB), :]
    return lax.bitcast_convert_type(w << 16, F32), lax.bitcast_convert_type(w & jnp.uint32(0xFFFF0000), F32)


def _load_rows_bf16(ref, R):
    halves = [_load_slab_halves(ref, R, j) for j in range(SLAB)]
    return jnp.concatenate([lo for lo, _ in halves] + [hi for _, hi in halves], axis=1).astype(BF16)


def _slab_copy(src, dst, sem, src_row8, dst_row):
    return pltpu.make_async_copy(src.at[pl.ds(pl.multiple_of(src_row8, SLAB), SLAB), :],
                                 dst.at[pl.ds(dst_row * SLAB, SLAB), :], sem)


def _router_kernel(x_ref, gain_ref, mod_ref, wr_ref, rb_ref, h_ref, e_ref, g_ref, r_ref, cnt_ref, cnt_scr, *, tm):
    E, GS = N_EXPERTS, GROUP_SIZE
    i = pl.program_id(0)

    @pl.when(i == 0)
    def _():
        cnt_scr[...] = jnp.zeros_like(cnt_scr)

    h = _norm_modulate(x_ref[...], gain_ref[...], mod_ref, 3, 4)
    _store_slabs(h_ref, _pack_rows(h))
    h_hi, h_lo = _split_hi_lo(h)
    w_hi, w_lo = _split_hi_lo(wr_ref[...])
    logits = _dot_nt(w_hi, h_hi) + _dot_nt(w_lo, h_hi) + _dot_nt(w_hi, h_lo)
    scores = jax.nn.sigmoid(logits)
    biased = scores + rb_ref[...]
    sub = lax.broadcasted_iota(I32, (GS, tm), 0)
    gscore = []
    for gi in range(N_EXPERT_GROUPS):
        xg = biased[gi * GS:(gi + 1) * GS, :]
        m1 = jnp.max(xg, axis=0, keepdims=True)
        first = jnp.min(jnp.where(xg == m1, sub, GS), axis=0, keepdims=True)
        m2 = jnp.max(jnp.where(sub == first, NEG_INF, xg), axis=0, keepdims=True)
        gscore.append(m1 + m2)
    cur = jnp.concatenate(gscore, axis=0)
    gid = lax.broadcasted_iota(I32, (N_EXPERT_GROUPS, tm), 0)
    gsel = jnp.zeros((N_EXPERT_GROUPS, tm), F32)
    for _ in range(TOPK_GROUPS):
        mx = jnp.max(cur, axis=0, keepdims=True)
        first = jnp.min(jnp.where(cur == mx, gid, N_EXPERT_GROUPS), axis=0, keepdims=True)
        hit = gid == first
        gsel = jnp.where(hit, 1.0, gsel)
        cur = jnp.where(hit, NEG_INF, cur)
    ok = jnp.concatenate(
        [jnp.broadcast_to(gsel[gi:gi + 1, :], (GS, tm)) for gi in range(N_EXPERT_GROUPS)], axis=0)
    masked = jnp.where(ok > 0.5, biased, NEG_INF)
    eid = lax.broadcasted_iota(I32, (E, tm), 0)
    sel = jnp.zeros((E, tm), F32)
    picks, pick_scores = [], []
    for _ in range(TOP_K):
        mx = jnp.max(masked, axis=0, keepdims=True)
        first = jnp.min(jnp.where(masked == mx, eid, E), axis=0, keepdims=True)
        hit = eid == first
        picks.append(first)
        pick_scores.append(jnp.sum(jnp.where(hit, scores, 0.0), axis=0, keepdims=True))
        sel = jnp.where(hit, 1.0, sel)
        masked = jnp.where(hit, NEG_INF, masked)
    total = pick_scores[0]
    for s in pick_scores[1:]:
        total = total + s
    g_ref[...] = jnp.concatenate(pick_scores, axis=0) / total * ROUTED_SCALE
    e_ref[...] = jnp.concatenate(picks, axis=0)
    rr = lax.broadcasted_iota(I32, (tm, tm), 0)
    cc = lax.broadcasted_iota(I32, (tm, tm), 1)
    before = jnp.where(rr < cc, 1.0, 0.0).astype(BF16)
    rank = _dot(sel.astype(BF16), before) + cnt_scr[...]
    r_ref[...] = jnp.concatenate(
        [jnp.sum(jnp.where(eid == p, rank, 0.0), axis=0, keepdims=True) for p in picks], axis=0).astype(I32)
    cnt_scr[...] = cnt_scr[...] + jnp.sum(sel, axis=-1, keepdims=True)
    cnt_ref[...] = cnt_scr[...].astype(I32)


def _router(x, gain, mod, router_w_t, router_b, *, n_prompt, dec_seq, tm):
    N, D = x.shape
    E = N_EXPERTS
    assert D == 2 * SLAB * 128
    assert N % tm == 0 and n_prompt % tm == 0 and dec_seq % tm == 0
    row = functools.partial(_mod_row, tm=tm, n_prompt=n_prompt, dec_seq=dec_seq)
    return pl.pallas_call(
        functools.partial(_router_kernel, tm=tm),
        grid=(N // tm,),
        in_specs=[
            pl.BlockSpec((tm, D), lambda i: (i, 0)),
            pl.BlockSpec((1, D), lambda i: (0, 0)),
            pl.BlockSpec((None, 1, 6 * D), lambda i: (row(i), 0, 0)),
            pl.BlockSpec((E, D), lambda i: (0, 0)),
            pl.BlockSpec((E, 1), lambda i: (0, 0)),
        ],
        out_specs=[
            pl.BlockSpec((tm * SLAB, 128), lambda i: (i, 0)),
            pl.BlockSpec((TOP_K, tm), lambda i: (0, i)),
            pl.BlockSpec((TOP_K, tm), lambda i: (0, i)),
            pl.BlockSpec((TOP_K, tm), lambda i: (0, i)),
            pl.BlockSpec((E, 1), lambda i: (0, 0)),
        ],
        out_shape=[
            jax.ShapeDtypeStruct((N * SLAB, 128), U32),
            jax.ShapeDtypeStruct((TOP_K, N), I32),
            jax.ShapeDtypeStruct((TOP_K, N), F32),
            jax.ShapeDtypeStruct((TOP_K, N), I32),
            jax.ShapeDtypeStruct((E, 1), I32),
        ],
        scratch_shapes=[pltpu.VMEM((E, 1), F32)],
        compiler_params=_cparams(("arbitrary",)),
        name="moe_router",
    )(x, gain.reshape(1, D), mod, router_w_t, router_b.reshape(E, 1))


def _experts_kernel(be_ref, nused_ref, tok_cur, tok_nxt, h_hbm, w1_ref, w3_ref, w2_ref, y_ref,
                    xbuf, sems, w1c, w3c, w2c, prev_e):
    b = pl.program_id(0)
    e = be_ref[b]
    n_used = nused_ref[0]
    BLK = xbuf.shape[1] // SLAB
    slot = b % 2

    FF = w1c.shape[1]
    D = w2c.shape[1]
    CW = 256
    n_chunks = 2 * (FF // CW) + D // CW
    per_chunk = BLK // n_chunks + 1

    def gather(tok_ref, s, lo=0, hi=BLK):
        for r in range(lo, min(hi, BLK)):
            _slab_copy(h_hbm, xbuf.at[s], sems.at[s], tok_ref[0, r], r).start()

    def drain(s):
        for r in range(BLK):
            _slab_copy(h_hbm, xbuf.at[s], sems.at[s], 0, r).wait()

    @pl.when(b == 0)
    def _():
        prev_e[0] = -1
        gather(tok_cur, 0)

    @pl.when(b < n_used)
    def _():
        @pl.when(e != prev_e[0])
        def _():
            w1c[...] = w1_ref[...].astype(BF16)
            w3c[...] = w3_ref[...].astype(BF16)
            w2c[...] = w2_ref[...].astype(BF16)
            prev_e[0] = e

        drain(slot)
        chunk = [0]

        def gather_some():
            gather(tok_nxt, 1 - slot, chunk[0] * per_chunk, (chunk[0] + 1) * per_chunk)
            chunk[0] += 1

        x = _load_rows_bf16(xbuf.at[slot], BLK)
        mids = []
        for c in range(FF // CW):
            sl = slice(c * CW, (c + 1) * CW)
            gather_some()
            a = _dot(x, w1c[:, sl])
            gather_some()
            mids.append((_silu(a) * _dot(x, w3c[:, sl])).astype(BF16))
        mid = jnp.concatenate(mids, axis=1)
        half = D // 2
        for c in range(half // CW):
            gather_some()
            y_lo = _dot(mid, w2c[:, c * CW:(c + 1) * CW])
            gather_some()
            y_hi = _dot(mid, w2c[:, half + c * CW:half + (c + 1) * CW])
            words = _pack_rows(jnp.concatenate([y_lo, y_hi], axis=1))
            for jj in range(CW // 128):
                j = c * (CW // 128) + jj
                y_ref[pl.ds(j, BLK, stride=SLAB), :] = words[:, jj * 128:(jj + 1) * 128]

    @pl.when(b >= n_used)
    def _():
        @pl.when(b == n_used)
        def _():
            drain(slot)

        y_ref[...] = jnp.zeros_like(y_ref)


def _experts(block_expert, n_used, slot_tok8, h, w1, w3, w2, layer):
    n_blocks = block_expert.shape[0]
    BLK = MOE_BLOCK
    _, _, D, FF = w1.shape
    slot_tok3 = slot_tok8.reshape(n_blocks, 1, BLK)
    return pl.pallas_call(
        _experts_kernel,
        grid_spec=pltpu.PrefetchScalarGridSpec(
            num_scalar_prefetch=2,
            grid=(n_blocks,),
            in_specs=[
                pl.BlockSpec((None, 1, BLK), lambda b, be, nu: (b, 0, 0), memory_space=pltpu.SMEM),
                pl.BlockSpec((None, 1, BLK), lambda b, be, nu: (jnp.minimum(b + 1, n_blocks - 1), 0, 0),
                             memory_space=pltpu.SMEM),
                pl.BlockSpec(memory_space=pl.ANY),
                pl.BlockSpec((None, None, D, FF), lambda b, be, nu: (layer, be[b], 0, 0)),
                pl.BlockSpec((None, None, D, FF), lambda b, be, nu: (layer, be[b], 0, 0)),
                pl.BlockSpec((None, None, FF, D), lambda b, be, nu: (layer, be[b], 0, 0)),
            ],
            out_specs=pl.BlockSpec((BLK * SLAB, 128), lambda b, be, nu: (b, 0)),
            scratch_shapes=[
                pltpu.VMEM((2, BLK * SLAB, 128), U32),
                pltpu.SemaphoreType.DMA((2,)),
                pltpu.VMEM((D, FF), BF16),
                pltpu.VMEM((D, FF), BF16),
                pltpu.VMEM((FF, D), BF16),
                pltpu.SMEM((1,), I32),
            ],
        ),
        out_shape=jax.ShapeDtypeStruct((n_blocks * BLK * SLAB, 128), U32),
        compiler_params=_cparams(("arbitrary",)),
        name="moe_experts",
    )(block_expert, n_used, slot_tok3, slot_tok3, h, w1, w3, w2)


def _combine_kernel(dest_cur, dest_nxt, gates_ref, y_hbm, h_ref, x_ref, gate_ref, ws1_ref, ws3_ref, ws2_ref,
                    *rest, prompt_tiles):
    if prompt_tiles is None:
        o_ref, ybuf0, ybuf1, sems = rest
    else:
        op_ref, os_ref, ybuf0, ybuf1, sems = rest
    tm = x_ref.shape[0]
    i = pl.program_id(0)
    last = pl.num_programs(0) - 1

    def gather(dest_ref, buf, sem, lo=0, hi=tm):
        for k in range(TOP_K):
            for r in range(lo, hi):
                _slab_copy(y_hbm, buf.at[k], sem, dest_ref[k, r], r).start()

    def drain(buf, sem):
        for k in range(TOP_K):
            for r in range(tm):
                _slab_copy(y_hbm, buf.at[k], sem, 0, r).wait()

    @pl.when(i == 0)
    def _():
        gather(dest_cur, ybuf0, sems.at[0])

    def step(cur, cur_sem, nxt, nxt_sem):
        drain(cur, cur_sem)
        per_group = tm // (SLAB + 2)
        gather(dest_nxt, nxt, nxt_sem, 0, per_group)
        hb = _load_rows_bf16(h_ref, tm)
        mid = (_silu(_dot(hb, ws1_ref[...])) * _dot(hb, ws3_ref[...])).astype(BF16)
        gather(dest_nxt, nxt, nxt_sem, per_group, 2 * per_group)
        shared = _dot(mid, ws2_ref[...])
        g = [gates_ref[:, k:k + 1] for k in range(TOP_K)]
        lo_parts, hi_parts = [], []
        for j in range(SLAB):
            gather(dest_nxt, nxt, nxt_sem, (j + 2) * per_group, tm if j == SLAB - 1 else (j + 3) * per_group)
            lo_acc = hi_acc = None
            for k in range(TOP_K):
                lo, hi = _load_slab_halves(cur.at[k], tm, j)
                lo_acc = g[k] * lo if lo_acc is None else lo_acc + g[k] * lo
                hi_acc = g[k] * hi if hi_acc is None else hi_acc + g[k] * hi
            lo_parts.append(lo_acc)
            hi_parts.append(hi_acc)
        routed = jnp.concatenate(lo_parts + hi_parts, axis=1)
        out = x_ref[...] + gate_ref[...] * (routed + shared)
        if prompt_tiles is None:
            o_ref[...] = out
        else:
            @pl.when(i < prompt_tiles)
            def _():
                op_ref[...] = out

            @pl.when(i >= prompt_tiles)
            def _():
                os_ref[...] = out

        @pl.when(i == last)
        def _():
            drain(nxt, nxt_sem)

    @pl.when(i % 2 == 0)
    def _():
        step(ybuf0, sems.at[0], ybuf1, sems.at[1])

    @pl.when(i % 2 == 1)
    def _():
        step(ybuf1, sems.at[1], ybuf0, sems.at[0])


def _combine(dest, gates, y, h, x, mod, ws1, ws3, ws2, *, n_prompt, dec_seq, split_outputs):
    N, D = x.shape
    tm = COMBINE_TM
    FF = ws1.shape[1]
    assert N % tm == 0 and n_prompt % tm == 0 and dec_seq % tm == 0
    nt = N // tm
    row = functools.partial(_mod_row, tm=tm, n_prompt=n_prompt, dec_seq=dec_seq)
    dest3 = dest.reshape(TOP_K, nt, tm).transpose(1, 0, 2)
    if split_outputs:
        pt = n_prompt // tm
        out_specs = [pl.BlockSpec((tm, D), lambda i: (jnp.minimum(i, pt - 1), 0)),
                     pl.BlockSpec((tm, D), lambda i: (jnp.maximum(i - pt, 0), 0))]
        out_shape = [jax.ShapeDtypeStruct((n_prompt, D), F32), jax.ShapeDtypeStruct((N - n_prompt, D), F32)]
    else:
        pt = None
        out_specs = pl.BlockSpec((tm, D), lambda i: (i, 0))
        out_shape = jax.ShapeDtypeStruct((N, D), F32)
    return pl.pallas_call(
        functools.partial(_combine_kernel, prompt_tiles=pt),
        grid=(nt,),
        in_specs=[
            pl.BlockSpec((None, TOP_K, tm), lambda i: (i, 0, 0), memory_space=pltpu.SMEM),
            pl.BlockSpec((None, TOP_K, tm), lambda i: (jnp.minimum(i + 1, nt - 1), 0, 0), memory_space=pltpu.SMEM),
            pl.BlockSpec((tm, TOP_K), lambda i: (i, 0)),
            pl.BlockSpec(memory_space=pl.ANY),
            pl.BlockSpec((tm * SLAB, 128), lambda i: (i, 0)),
            pl.BlockSpec((tm, D), lambda i: (i, 0)),
            pl.BlockSpec((None, 1, D), lambda i: (row(i), 0, 5)),
            pl.BlockSpec((D, FF), lambda i: (0, 0)),
            pl.BlockSpec((D, FF), lambda i: (0, 0)),
            pl.BlockSpec((FF, D), lambda i: (0, 0)),
        ],
        out_specs=out_specs,
        out_shape=out_shape,
        scratch_shapes=[pltpu.VMEM((TOP_K, tm * SLAB, 128), U32), pltpu.VMEM((TOP_K, tm * SLAB, 128), U32),
                        pltpu.SemaphoreType.DMA((2,))],
        compiler_params=_cparams(("arbitrary",)),
        name="moe_combine",
    )(dest3, dest3, gates, y, h, x, mod, ws1, ws3, ws2)


def _moe_layer(x, gain, mod, router_w, router_b, w1, w3, w2, layer, ws1, ws3, ws2, *, n_prompt, dec_seq,
               split_outputs=False):
    N = x.shape[0]
    E, BLK = N_EXPERTS, MOE_BLOCK
    h, top_e, gates, rank, counts = _router(x, gain, mod, router_w.T, router_b,
                                            n_prompt=n_prompt, dec_seq=dec_seq,
                                            tm=_row_tile(512, n_prompt, dec_seq))
    counts = counts[:, 0]
    padded = (counts + BLK - 1) // BLK * BLK
    pad_end = jnp.cumsum(padded)
    pad_start = pad_end - padded
    n_blocks = N * TOP_K // BLK + E
    block_expert = jnp.minimum(
        jnp.sum(pad_end[None, :] <= (jnp.arange(n_blocks, dtype=I32) * BLK)[:, None], axis=1), E - 1).astype(I32)
    n_used = (pad_end[-1:] // BLK).astype(I32)
    onehot = top_e[None] == jnp.arange(E, dtype=I32)[:, None, None]
    dest = jnp.sum(jnp.where(onehot, pad_start[:, None, None], 0), axis=0) + rank
    NP = N + BLK
    keys = (top_e * NP + jnp.arange(N, dtype=I32)[None, :]).reshape(-1)
    fill_i = jnp.arange(BLK, dtype=I32)[None, :]
    fill_e = jnp.arange(E, dtype=I32)[:, None]
    fillers = jnp.where(fill_i < (padded - counts)[:, None], fill_e * NP + N + fill_i, E * NP + fill_e * BLK + fill_i)
    slot_t = lax.sort(jnp.concatenate([keys, fillers.reshape(-1)])) % NP
    spread = jnp.arange(n_blocks * BLK, dtype=I32) % N
    slot_tok8 = jnp.where(slot_t < N, slot_t, spread) * SLAB
    y = _experts(block_expert, n_used, slot_tok8, h, w1, w3, w2, layer)
    return _combine(dest * SLAB, gates.T, y, h, x, mod, ws1, ws3, ws2, n_prompt=n_prompt, dec_seq=dec_seq,
                    split_outputs=split_outputs)


def _rope_tables(S):
    quarter = HEAD_DIM // 4
    pos = jnp.arange(S)
    row_id = (pos // GRID_W).astype(F32)
    col_id = (pos % GRID_W).astype(F32)
    inv = ROPE_THETA ** (-jnp.arange(quarter, dtype=F32) / quarter)
    ar, ac = row_id[:, None] * inv, col_id[:, None] * inv
    cos = jnp.concatenate([jnp.cos(ar), jnp.cos(ar), jnp.cos(ac), jnp.cos(ac)], axis=-1)
    sin = jnp.concatenate([-jnp.sin(ar), jnp.sin(ar), -jnp.sin(ac), jnp.sin(ac)], axis=-1)
    return cos, sin


def kernel(x_prompt, x_sample, state_mlstm_C, state_mlstm_n, state_mlstm_m, cache_attn_k, cache_attn_v,
           c, c_ctx, ada_w, ada_b, norm_mix, norm_ffn,
           mlstm_w_in, mlstm_gate_b, mlstm_head_g, mlstm_w_out,
           attn_w_qkv, attn_q_g, attn_k_g, attn_sink, attn_w_o,
           moe_router_w, moe_router_b, moe_w1, moe_w3, moe_w2, shared_w1, shared_w3, shared_w2):
    D = D_MODEL
    Bp, Sp, _ = x_prompt.shape
    Bs, Ss, _ = x_sample.shape
    n_prompt = Bp * Sp
    N = n_prompt + Bs * Ss
    dims = dict(n_prompt=n_prompt, dec_seq=Ss)
    tm512 = _row_tile(512, n_prompt, Ss)

    x_parts = (x_prompt.reshape(n_prompt, D), x_sample.reshape(Bs * Ss, D))
    rows = 16
    cvec = jnp.zeros((rows, D), F32).at[0].set(c_ctx).at[1:1 + Bs].set(c)
    mod_all = _ada_table(cvec, ada_w, ada_b)[:, :1 + Bs].reshape(DEPTH, 1 + Bs, 1, 6 * D)

    H = MLSTM_HEADS
    mod = mod_all[0]
    w_in = mlstm_w_in[0]
    w_main = w_in[:, :MLSTM_MAIN].astype(BF16)
    w_gate = jnp.pad(w_in[:, MLSTM_MAIN:], ((0, 0), (0, GATE_PAD - 4 * H))).astype(BF16)
    gate_b = jnp.pad(mlstm_gate_b[0], (0, GATE_PAD - 4 * H))
    col_scale = jnp.concatenate([jnp.full((MLSTM_QK,), MLSTM_DK ** -0.5, F32),
                                 jnp.ones((MLSTM_MAIN - MLSTM_QK,), F32)])
    qkvo, gates, x = _nm_matmul(x_parts, norm_mix[0], mod, w_main, col_scale, jnp.zeros((MLSTM_MAIN,), F32),
                                shift_idx=0, scale_idx=1, out_dtype=BF16, tm=tm512, tn=1024,
                                aux_w=w_gate, aux_b=gate_b, name="mlstm_proj", **dims)
    L = 256
    state = (state_mlstm_C[:, 0], state_mlstm_n[:, 0], state_mlstm_m[:, 0])
    hf, hb, C_p, n_p, m_p = _mlstm_scan(qkvo, gates, Bp, Sp, Bs, Ss, L, state)
    x = _mm_residual((hf, hb, qkvo, mlstm_head_g[0]), mlstm_w_out[0].astype(BF16), x, mod, gate_idx=2,
                     tm=tm512, tn=1024, mlstm_prologue=True, name="mlstm_out", **dims)
    x = _moe_layer(x, norm_ffn[0], mod, moe_router_w[0], moe_router_b[0], moe_w1, moe_w3, moe_w2, 0,
                   shared_w1[0].astype(BF16), shared_w3[0].astype(BF16), shared_w2[0].astype(BF16), **dims)

    mod = mod_all[1]
    (qkv,) = _nm_matmul(x, norm_mix[1], mod, attn_w_qkv[0].astype(BF16), jnp.ones((ATTN_PROJ,), F32),
                        jnp.zeros((ATTN_PROJ,), F32), shift_idx=0, scale_idx=1, out_dtype=F32, tm=tm512, tn=1024,
                        name="attn_qkv", **dims)
    ident_cos = jnp.ones((Sp, HEAD_DIM), F32)
    ident_sin = jnp.zeros((Sp, HEAD_DIM), F32)
    q_p, k_p, v_p, kf_p, vf_p = _qk_prep(qkv, attn_q_g[0], attn_k_g[0], ident_cos, ident_sin, 0, n_prompt, Sp,
                                         True, "qk_prep_prompt")
    cos, sin = _rope_tables(Ss)
    q_s, k_s, v_s = _qk_prep(qkv, attn_q_g[0], attn_k_g[0], cos, sin, n_prompt, Bs * Ss, 256, False,
                             "qk_prep_latent")
    o_p = _attn_context(attn_sink[0], q_p, k_p, v_p, Bp, Sp)
    P = cache_attn_k.shape[2]
    ctx_k = cache_attn_k[:, 0].reshape(Bs, P, ATTN_KV).astype(BF16)
    ctx_v = cache_attn_v[:, 0].reshape(Bs, P, ATTN_KV).astype(BF16)
    o_s = _attn_latent(attn_sink[0], q_s, k_s, v_s, ctx_k, ctx_v, Bs, Ss)
    x = _mm_residual((o_p, o_s), attn_w_o[0].astype(BF16), x, mod, gate_idx=2,
                     tm=_row_tile(1024, n_prompt, Ss), tn=1024,
                     mlstm_prologue=False, name="attn_out", **dims)
    y_prompt, y_sample = _moe_layer(
        x, norm_ffn[1], mod, moe_router_w[1], moe_router_b[1], moe_w1, moe_w3, moe_w2, 1,
        shared_w1[1].astype(BF16), shared_w3[1].astype(BF16), shared_w2[1].astype(BF16), split_outputs=True, **dims)
    return (y_prompt.reshape(Bp, Sp, D), y_sample.reshape(Bs, Ss, D), C_p[:, None], n_p[:, None], m_p[:, None],
            kf_p.reshape(Bp, 1, Sp, ATTN_KV_HEADS, HEAD_DIM), vf_p.reshape(Bp, 1, Sp, ATTN_KV_HEADS, HEAD_DIM))
```

```python
import functools

import jax
import jax.numpy as jnp
from jax import lax
from jax.experimental import pallas as pl
from jax.experimental.pallas import tpu as pltpu

F32 = jnp.float32
BF16 = jnp.bfloat16
I32 = jnp.int32

D_MODEL = 2048
DEPTH = 2
EPS = 1e-6
GRID_W = 64
MLSTM_HEADS = 8
MLSTM_DK = 128
MLSTM_DV = 256
MLSTM_QK = MLSTM_HEADS * MLSTM_DK
MLSTM_V = MLSTM_HEADS * MLSTM_DV
MLSTM_MAIN = 2 * MLSTM_QK + 2 * MLSTM_V
GATE_PAD = 128
HEAD_DIM = 128
ATTN_Q_HEADS = 16
ATTN_KV_HEADS = 4
ATTN_GROUPS = 4
WINDOW = 128
QBLK = 128
ROPE_THETA = 10000.0
ATTN_Q = ATTN_Q_HEADS * HEAD_DIM
ATTN_KV = ATTN_KV_HEADS * HEAD_DIM
ATTN_PROJ = ATTN_Q + 2 * ATTN_KV
N_EXPERTS = 64
TOP_K = 8
N_EXPERT_GROUPS = 8
TOPK_GROUPS = 4
GROUP_SIZE = N_EXPERTS // N_EXPERT_GROUPS
EXPERT_FF = 512
ROUTED_SCALE = 2.5
MOE_BLOCK = 512
COMBINE_TM = 128

V7X_VMEM_LIMIT = 56 * 1024 * 1024
NEG_INF = float("-inf")


def _cparams(sem):
    return pltpu.CompilerParams(dimension_semantics=("arbitrary",) * len(sem), vmem_limit_bytes=V7X_VMEM_LIMIT)


def _split_hi_lo(a):
    hi = a.astype(BF16)
    lo = (a - hi.astype(F32)).astype(BF16)
    return hi, lo


def _dot(a, b):
    return jnp.dot(a, b, preferred_element_type=F32)


def _dot_nt(a, b):
    return lax.dot_general(a, b, (((1,), (1,)), ((), ())), preferred_element_type=F32)


def _dot_tn(a, b):
    return lax.dot_general(a, b, (((0,), (0,)), ((), ())), preferred_element_type=F32)


def _silu(x):
    return x * jax.nn.sigmoid(x)


def _row_tile(preferred, n_prompt, dec_seq):
    tm = preferred
    while n_prompt % tm or dec_seq % tm:
        tm //= 2
    return tm


def _mod_row(i, tm, n_prompt, dec_seq):
    r0 = i * tm
    return jnp.where(r0 < n_prompt, 0, 1 + (r0 - n_prompt) // dec_seq)


def _norm_modulate(x, gain, mod_ref, shift_idx, scale_idx):
    D = D_MODEL
    y = x * lax.rsqrt(jnp.mean(x * x, axis=-1, keepdims=True) + EPS) * gain
    shift = mod_ref[:, shift_idx * D:(shift_idx + 1) * D]
    scale = mod_ref[:, scale_idx * D:(scale_idx + 1) * D]
    return y * (1.0 + scale) + shift


def _ada_kernel(c_ref, w_ref, b_ref, o_ref):
    s = _silu(c_ref[...])
    s_hi, s_lo = _split_hi_lo(s)
    w_hi, w_lo = _split_hi_lo(w_ref[...])
    o_ref[...] = _dot(s_hi, w_hi) + _dot(s_hi, w_lo) + _dot(s_lo, w_hi) + b_ref[...]


def _ada_table(cvec, ada_w, ada_b):
    D = D_MODEL
    tn = 1024
    rows = cvec.shape[0]
    return pl.pallas_call(
        _ada_kernel,
        grid=(DEPTH, 6 * D // tn),
        in_specs=[
            pl.BlockSpec((rows, D), lambda l, j: (0, 0)),
            pl.BlockSpec((None, D, tn), lambda l, j: (l, 0, j)),
            pl.BlockSpec((None, 1, tn), lambda l, j: (l, 0, j)),
        ],
        out_specs=pl.BlockSpec((None, rows, tn), lambda l, j: (l, 0, j)),
        out_shape=jax.ShapeDtypeStruct((DEPTH, rows, 6 * D), F32),
        compiler_params=_cparams(("parallel", "parallel")),
        name="ada_table",
    )(cvec, ada_w, ada_b.reshape(DEPTH, 1, 6 * D))


def _nm_matmul_kernel(*refs, shift_idx, scale_idx, has_aux, prompt_tiles):
    if prompt_tiles is None:
        x_ref, gain_ref, mod_ref, w_ref, cs_ref, cb_ref, *rest = refs
    else:
        xp_ref, xs_ref, gain_ref, mod_ref, w_ref, cs_ref, cb_ref, *rest = refs
    if has_aux:
        wa_ref, ab_ref, o_ref, aux_ref, *tail = rest
    else:
        o_ref, *tail = rest
    if prompt_tiles is None:
        (h_scr,) = tail
    else:
        xcat_ref, h_scr = tail

    def prologue(x):
        h = _norm_modulate(x, gain_ref[...], mod_ref, shift_idx, scale_idx).astype(BF16)
        h_scr[...] = h
        if has_aux:
            aux_ref[...] = _dot(h, wa_ref[...]) + ab_ref[...]
        if prompt_tiles is not None:
            xcat_ref[...] = x

    first_col = pl.program_id(1) == 0
    if prompt_tiles is None:
        @pl.when(first_col)
        def _():
            prologue(x_ref[...])
    else:
        context = pl.program_id(0) < prompt_tiles

        @pl.when(first_col & context)
        def _():
            prologue(xp_ref[...])

        @pl.when(first_col & jnp.logical_not(context))
        def _():
            prologue(xs_ref[...])

    acc = _dot(h_scr[...], w_ref[...])
    o_ref[...] = (acc * cs_ref[...] + cb_ref[...]).astype(o_ref.dtype)


def _nm_matmul(x, gain, mod, w, col_scale, col_bias, *, shift_idx, scale_idx, n_prompt, dec_seq,
               out_dtype, tm, tn, aux_w=None, aux_b=None, name):
    split = isinstance(x, tuple)
    D = w.shape[0]
    N = sum(p.shape[0] for p in x) if split else x.shape[0]
    P = w.shape[1]
    assert N % tm == 0 and P % tn == 0 and n_prompt % tm == 0 and dec_seq % tm == 0
    has_aux = aux_w is not None
    row = functools.partial(_mod_row, tm=tm, n_prompt=n_prompt, dec_seq=dec_seq)
    pt = n_prompt // tm
    if split:
        x_specs = [pl.BlockSpec((tm, D), lambda i, j: (jnp.minimum(i, pt - 1), 0)),
                   pl.BlockSpec((tm, D), lambda i, j: (jnp.maximum(i - pt, 0), 0))]
        x_args = list(x)
    else:
        x_specs = [pl.BlockSpec((tm, D), lambda i, j: (i, 0))]
        x_args = [x]
    in_specs = x_specs + [
        pl.BlockSpec((1, D), lambda i, j: (0, 0)),
        pl.BlockSpec((None, 1, 6 * D), lambda i, j: (row(i), 0, 0)),
        pl.BlockSpec((D, tn), lambda i, j: (0, j)),
        pl.BlockSpec((1, tn), lambda i, j: (0, j)),
        pl.BlockSpec((1, tn), lambda i, j: (0, j)),
    ]
    args = x_args + [gain.reshape(1, D), mod, w, col_scale.reshape(1, P), col_bias.reshape(1, P)]
    out_specs = [pl.BlockSpec((tm, tn), lambda i, j: (i, j))]
    out_shape = [jax.ShapeDtypeStruct((N, P), out_dtype)]
    if has_aux:
        PA = aux_w.shape[1]
        in_specs += [pl.BlockSpec((D, PA), lambda i, j: (0, 0)), pl.BlockSpec((1, PA), lambda i, j: (0, 0))]
        args += [aux_w, aux_b.reshape(1, PA)]
        out_specs.append(pl.BlockSpec((tm, PA), lambda i, j: (i, 0)))
        out_shape.append(jax.ShapeDtypeStruct((N, PA), F32))
    if split:
        out_specs.append(pl.BlockSpec((tm, D), lambda i, j: (i, 0)))
        out_shape.append(jax.ShapeDtypeStruct((N, D), F32))
    return pl.pallas_call(
        functools.partial(_nm_matmul_kernel, shift_idx=shift_idx, scale_idx=scale_idx, has_aux=has_aux,
                          prompt_tiles=pt if split else None),
        grid=(N // tm, P // tn),
        in_specs=in_specs,
        out_specs=out_specs,
        out_shape=out_shape,
        scratch_shapes=[pltpu.VMEM((tm, D), BF16)],
        compiler_params=_cparams(("parallel", "arbitrary")),
        name=name,
    )(*args)


def _mm_residual_kernel(*refs, mlstm_prologue, prompt_tiles):
    if mlstm_prologue:
        hf_ref, hb_ref, og_ref, hg_ref, w_ref, x_ref, gate_ref, o_ref, l_scr = refs

        @pl.when(pl.program_id(1) == 0)
        def _():
            hs = hf_ref[...].astype(F32) + hb_ref[...].astype(F32)
            og = jax.nn.sigmoid(og_ref[...].astype(F32))
            for h in range(MLSTM_HEADS):
                sl = slice(h * MLSTM_DV, (h + 1) * MLSTM_DV)
                hh = hs[:, sl]
                hn = hh * lax.rsqrt(jnp.mean(hh * hh, axis=-1, keepdims=True) + EPS) * hg_ref[:, sl]
                l_scr[:, sl] = (og[:, sl] * hn).astype(BF16)

        o_ref[...] = x_ref[...] + gate_ref[...] * _dot(l_scr[...], w_ref[...])
    else:
        lp_ref, ls_ref, w_ref, x_ref, gate_ref, o_ref = refs
        context = pl.program_id(0) < prompt_tiles

        @pl.when(context)
        def _():
            o_ref[...] = x_ref[...] + gate_ref[...] * _dot(lp_ref[...], w_ref[...])

        @pl.when(jnp.logical_not(context))
        def _():
            o_ref[...] = x_ref[...] + gate_ref[...] * _dot(ls_ref[...], w_ref[...])


def _mm_residual(lhs_args, w, x, mod, *, gate_idx, n_prompt, dec_seq, tm, tn, mlstm_prologue, name):
    N, D = x.shape
    K = w.shape[0]
    assert N % tm == 0 and D % tn == 0 and n_prompt % tm == 0 and dec_seq % tm == 0
    row = functools.partial(_mod_row, tm=tm, n_prompt=n_prompt, dec_seq=dec_seq)
    pt = n_prompt // tm
    if mlstm_prologue:
        hf, hb, qkvo, head_g = lhs_args
        o_blk = (2 * MLSTM_QK + MLSTM_V) // MLSTM_V
        in_specs = [
            pl.BlockSpec((tm, K), lambda i, j: (i, 0)),
            pl.BlockSpec((tm, K), lambda i, j: (i, 0)),
            pl.BlockSpec((tm, MLSTM_V), lambda i, j: (i, o_blk)),
            pl.BlockSpec((1, K), lambda i, j: (0, 0)),
        ]
        args = [hf, hb, qkvo, head_g.reshape(1, K)]
        scratch = [pltpu.VMEM((tm, K), BF16)]
    else:
        lhs_p, lhs_s = lhs_args
        in_specs = [pl.BlockSpec((tm, K), lambda i, j: (jnp.minimum(i, pt - 1), 0)),
                    pl.BlockSpec((tm, K), lambda i, j: (jnp.maximum(i - pt, 0), 0))]
        args = [lhs_p, lhs_s]
        scratch = []
    in_specs += [
        pl.BlockSpec((K, tn), lambda i, j: (0, j)),
        pl.BlockSpec((tm, tn), lambda i, j: (i, j)),
        pl.BlockSpec((None, 1, tn), lambda i, j: (row(i), 0, gate_idx * (D // tn) + j)),
    ]
    args += [w, x, mod]
    return pl.pallas_call(
        functools.partial(_mm_residual_kernel, mlstm_prologue=mlstm_prologue, prompt_tiles=pt),
        grid=(N // tm, D // tn),
        in_specs=in_specs,
        out_specs=pl.BlockSpec((tm, tn), lambda i, j: (i, j)),
        out_shape=jax.ShapeDtypeStruct((N, D), F32),
        scratch_shapes=scratch,
        compiler_params=_cparams(("parallel", "arbitrary")),
        name=name,
    )(*args)


def _log_sigmoid(x):
    return jnp.minimum(x, 0.0) - jnp.log(1.0 + jnp.exp(-jnp.abs(x)))


def _mlstm_direction(d, q_ref, k_ref, v_ref, g_ref, h_ref, C_scr, n_scr, m_scr, L):
    H, DK, DV = MLSTM_HEADS, MLSTM_DK, MLSTM_DV
    g = g_ref[...]
    lf = _log_sigmoid(g)
    r = lax.broadcasted_iota(I32, (L, L), 0)
    c = lax.broadcasted_iota(I32, (L, L), 1)
    causal = (c <= r) if d == 0 else (c >= r)
    tri = jnp.where(causal, 1.0, 0.0).astype(BF16)
    lf1 = lf.astype(BF16)
    rem = lf - lf1.astype(F32)
    lf2 = rem.astype(BF16)
    lf3 = (rem - lf2.astype(F32)).astype(BF16)
    bsum = _dot(tri, lf1) + _dot(tri, lf2) + _dot(tri, lf3)
    g_t = g.T
    b_t = bsum.T
    end = L - 1 if d == 0 else 0
    m_old = [m_scr[d, h] for h in range(H)]
    n_old = [n_scr[d, h] for h in range(H)]
    m_upd, n_upd = [], []
    for h in range(H):
        ci, cf = d * 2 * H + h, d * 2 * H + H + h
        b_col, i_col = bsum[:, cf:cf + 1], g[:, ci:ci + 1]
        b_row, i_row = b_t[cf:cf + 1, :], g_t[ci:ci + 1, :]
        b_end = bsum[end:end + 1, cf:cf + 1]
        m, n = m_old[h], n_old[h]
        C = C_scr[d * H + h][...]
        qh = q_ref[:, h * DK:(h + 1) * DK]
        kh = k_ref[:, h * DK:(h + 1) * DK]
        vh = v_ref[:, h * DV:(h + 1) * DV]
        dmat = jnp.where(causal, b_col - b_row + i_row, NEG_INF)
        inter = b_col + m
        m_out = jnp.maximum(inter, jnp.max(dmat, axis=-1, keepdims=True))
        w = jnp.exp(dmat - m_out) * _dot_nt(qh, kh)
        dec = jnp.exp(inter - m_out)
        num = _dot(w.astype(BF16), vh) + dec * _dot(qh, C.astype(BF16))
        den = jnp.sum(w, axis=-1, keepdims=True) + dec * jnp.sum(qh.astype(F32) * n, axis=-1, keepdims=True)
        h_ref[:, h * DV:(h + 1) * DV] = (num / jnp.maximum(jnp.abs(den), jnp.exp(-m_out))).astype(h_ref.dtype)
        to_end = b_end - b_col + i_col
        m_new = jnp.maximum(b_end + m, jnp.max(to_end, axis=0, keepdims=True))
        wk = jnp.exp(to_end - m_new)
        dec_end = jnp.exp(b_end + m - m_new)
        kw = kh.astype(F32) * wk
        C_scr[d * H + h][...] = dec_end * C + _dot_tn(kw.astype(BF16), vh)
        n_upd.append(dec_end * n + jnp.sum(kw, axis=0, keepdims=True))
        m_upd.append(m_new)
    for h in range(H):
        n_scr[d, h] = n_upd[h]
        m_scr[d, h] = m_upd[h]


def _mlstm_kernel(fwd_ref, bwd_ref, bidx_ref, flag_ref, qf, kf, vf, gf, qb, kb, vb, gb, C0, n0, m0,
                  hf, hb, Co, no, mo, *scratch, L):
    H = MLSTM_HEADS
    C_scr, (n_scr, m_scr) = scratch[:2 * H], scratch[2 * H:]
    flags = flag_ref[pl.program_id(0)]
    first, last, context = (flags & 1) != 0, (flags & 2) != 0, (flags & 4) != 0

    @pl.when(first & context)
    def _():
        for C in C_scr:
            C[...] = jnp.zeros_like(C)
        n_scr[...] = jnp.zeros_like(n_scr)
        m_scr[...] = jnp.zeros_like(m_scr)

    @pl.when(first & jnp.logical_not(context))
    def _():
        for d in range(2):
            for h in range(H):
                C_scr[d * H + h][...] = C0[d, h]
                n_scr[d, h] = n0[d, h:h + 1, :]
                m_scr[d, h] = m0[d:d + 1, h:h + 1]

    _mlstm_direction(0, qf, kf, vf, gf, hf, C_scr, n_scr, m_scr, L)
    _mlstm_direction(1, qb, kb, vb, gb, hb, C_scr, n_scr, m_scr, L)

    @pl.when(last & context)
    def _():
        for d in range(2):
            for h in range(H):
                Co[d, h] = C_scr[d * H + h][...]
                no[d, h:h + 1, :] = n_scr[d, h]
                mo[d:d + 1, h:h + 1] = m_scr[d, h]


def _mlstm_scan(qkvo, gates, Bp, Sp, Bs, Ss, L, state):
    H, DK, DV = MLSTM_HEADS, MLSTM_DK, MLSTM_DV
    assert Sp % L == 0 and Ss % L == 0
    fwd, bwd, bidx, flags = [], [], [], []
    for context, B, S, base in ((1, Bp, Sp, 0), (0, Bs, Ss, Bp * Sp // L)):
        nc = S // L
        for b in range(B):
            for c in range(nc):
                fwd.append(base + b * nc + c)
                bwd.append(base + b * nc + nc - 1 - c)
                bidx.append(b)
                flags.append((c == 0) * 1 + (c == nc - 1) * 2 + context * 4)
    tables = [jnp.asarray(t, I32) for t in (fwd, bwd, bidx, flags)]
    lat_b = lambda s, fl, bi: jnp.where((fl[s] & 4) != 0, 0, bi[s])
    ctx_b = lambda s, fl, bi: jnp.where((fl[s] & 4) != 0, bi[s], Bp - 1)

    def specs(tbl):
        return [
            pl.BlockSpec((L, MLSTM_QK), lambda s, f, w, bi, fl: ((f, w)[tbl][s], 0)),
            pl.BlockSpec((L, MLSTM_QK), lambda s, f, w, bi, fl: ((f, w)[tbl][s], 1)),
            pl.BlockSpec((L, MLSTM_V), lambda s, f, w, bi, fl: ((f, w)[tbl][s], 1)),
            pl.BlockSpec((L, GATE_PAD), lambda s, f, w, bi, fl: ((f, w)[tbl][s], 0)),
        ]

    def state_specs(which):
        return [
            pl.BlockSpec((None, 2, H, DK, DV), lambda s, f, w, bi, fl: (which(s, fl, bi), 0, 0, 0, 0)),
            pl.BlockSpec((None, 2, H, DK), lambda s, f, w, bi, fl: (which(s, fl, bi), 0, 0, 0)),
            pl.BlockSpec((None, 2, H), lambda s, f, w, bi, fl: (which(s, fl, bi), 0, 0)),
        ]

    N = qkvo.shape[0]
    return pl.pallas_call(
        functools.partial(_mlstm_kernel, L=L),
        grid_spec=pltpu.PrefetchScalarGridSpec(
            num_scalar_prefetch=4,
            grid=(len(fwd),),
            in_specs=specs(0) + specs(1) + state_specs(lat_b),
            out_specs=[
                pl.BlockSpec((L, MLSTM_V), lambda s, f, w, bi, fl: (f[s], 0)),
                pl.BlockSpec((L, MLSTM_V), lambda s, f, w, bi, fl: (w[s], 0)),
            ] + state_specs(ctx_b),
            scratch_shapes=[pltpu.VMEM((DK, DV), F32)] * (2 * H) + [
                pltpu.VMEM((2, H, 1, DK), F32),
                pltpu.VMEM((2, H, 1, 1), F32),
            ],
        ),
        out_shape=[
            jax.ShapeDtypeStruct((N, MLSTM_V), BF16),
            jax.ShapeDtypeStruct((N, MLSTM_V), BF16),
            jax.ShapeDtypeStruct((Bp, 2, H, DK, DV), F32),
            jax.ShapeDtypeStruct((Bp, 2, H, DK), F32),
            jax.ShapeDtypeStruct((Bp, 2, H), F32),
        ],
        compiler_params=_cparams(("arbitrary",)),
        name="mlstm_scan",
    )(*tables, qkvo, qkvo, qkvo, gates, qkvo, qkvo, qkvo, gates, *state)


def _qk_prep_kernel(qkv_ref, qg_ref, kg_ref, cos_ref, sin_ref, *outs, emit_f32):
    if emit_f32:
        q_ref, k_ref, v_ref, kf_ref, vf_ref = outs
    else:
        q_ref, k_ref, v_ref = outs
    cos = cos_ref[...]
    sin = sin_ref[...]
    lane = lax.broadcasted_iota(I32, cos.shape, 1)
    first = (lane % (HEAD_DIM // 2)) < (HEAD_DIM // 4)

    def norm_rope(x, gain):
        xn = x * lax.rsqrt(jnp.mean(x * x, axis=-1, keepdims=True) + EPS) * gain
        partner = jnp.where(first, pltpu.roll(xn, HEAD_DIM - HEAD_DIM // 4, 1), pltpu.roll(xn, HEAD_DIM // 4, 1))
        return xn * cos + partner * sin

    for h in range(ATTN_Q_HEADS):
        sl = slice(h * HEAD_DIM, (h + 1) * HEAD_DIM)
        q_ref[:, sl] = (norm_rope(qkv_ref[:, sl], qg_ref[...]) * HEAD_DIM ** -0.5).astype(BF16)
    for h in range(ATTN_KV_HEADS):
        sl = slice(h * HEAD_DIM, (h + 1) * HEAD_DIM)
        kx = norm_rope(qkv_ref[:, ATTN_Q + h * HEAD_DIM:ATTN_Q + (h + 1) * HEAD_DIM], kg_ref[...])
        k_ref[:, sl] = kx.astype(BF16)
        if emit_f32:
            kf_ref[:, sl] = kx
    vx = qkv_ref[:, ATTN_Q + ATTN_KV:]
    v_ref[...] = vx.astype(BF16)
    if emit_f32:
        vf_ref[...] = vx


def _qk_prep(qkv, q_g, k_g, cos, sin, row0, rows, tm, emit_f32, name):
    assert rows % tm == 0 and row0 % tm == 0 and cos.shape[0] % tm == 0
    base = row0 // tm
    nt = cos.shape[0] // tm
    out_specs = [
        pl.BlockSpec((tm, ATTN_Q), lambda i: (i, 0)),
        pl.BlockSpec((tm, ATTN_KV), lambda i: (i, 0)),
        pl.BlockSpec((tm, ATTN_KV), lambda i: (i, 0)),
    ]
    out_shape = [
        jax.ShapeDtypeStruct((rows, ATTN_Q), BF16),
        jax.ShapeDtypeStruct((rows, ATTN_KV), BF16),
        jax.ShapeDtypeStruct((rows, ATTN_KV), BF16),
    ]
    if emit_f32:
        out_specs += [pl.BlockSpec((tm, ATTN_KV), lambda i: (i, 0))] * 2
        out_shape += [jax.ShapeDtypeStruct((rows, ATTN_KV), F32)] * 2
    return pl.pallas_call(
        functools.partial(_qk_prep_kernel, emit_f32=emit_f32),
        grid=(rows // tm,),
        in_specs=[
            pl.BlockSpec((tm, ATTN_PROJ), lambda i: (base + i, 0)),
            pl.BlockSpec((1, HEAD_DIM), lambda i: (0, 0)),
            pl.BlockSpec((1, HEAD_DIM), lambda i: (0, 0)),
            pl.BlockSpec((tm, HEAD_DIM), lambda i: (i % nt, 0)),
            pl.BlockSpec((tm, HEAD_DIM), lambda i: (i % nt, 0)),
        ],
        out_specs=out_specs,
        out_shape=out_shape,
        compiler_params=_cparams(("parallel",)),
        name=name,
    )(qkv, q_g.reshape(1, HEAD_DIM), k_g.reshape(1, HEAD_DIM), cos, sin)


def _sink_column(sink_ref, kv, rows_per_head):
    parts = [jnp.full((rows_per_head, 1), sink_ref[kv * ATTN_GROUPS + g], F32) for g in range(ATTN_GROUPS)]
    return jnp.concatenate(parts, axis=0)


def _attn_ctx_kernel(sink_ref, q_ref, k_ref, v_ref, o_ref):
    S = q_ref.shape[0]
    for kv in range(ATTN_KV_HEADS):
        q = jnp.concatenate(
            [q_ref[:, (kv * ATTN_GROUPS + g) * HEAD_DIM:(kv * ATTN_GROUPS + g + 1) * HEAD_DIM]
             for g in range(ATTN_GROUPS)], axis=0)
        ksl = slice(kv * HEAD_DIM, (kv + 1) * HEAD_DIM)
        s = _dot_nt(q, k_ref[:, ksl])
        sk = _sink_column(sink_ref, kv, S)
        m = jnp.maximum(jnp.max(s, axis=-1, keepdims=True), sk)
        p = jnp.exp(s - m)
        den = jnp.sum(p, axis=-1, keepdims=True) + jnp.exp(sk - m)
        o = _dot(p.astype(BF16), v_ref[:, ksl]) * (1.0 / den)
        for g in range(ATTN_GROUPS):
            hq = kv * ATTN_GROUPS + g
            o_ref[:, hq * HEAD_DIM:(hq + 1) * HEAD_DIM] = o[g * S:(g + 1) * S].astype(BF16)


def _attn_context(sink, q, k, v, B, S):
    return pl.pallas_call(
        _attn_ctx_kernel,
        grid_spec=pltpu.PrefetchScalarGridSpec(
            num_scalar_prefetch=0,
            grid=(B,),
            in_specs=[
                pl.BlockSpec(memory_space=pltpu.SMEM),
                pl.BlockSpec((S, ATTN_Q), lambda b: (b, 0)),
                pl.BlockSpec((S, ATTN_KV), lambda b: (b, 0)),
                pl.BlockSpec((S, ATTN_KV), lambda b: (b, 0)),
            ],
            out_specs=pl.BlockSpec((S, ATTN_Q), lambda b: (b, 0)),
        ),
        out_shape=jax.ShapeDtypeStruct((B * S, ATTN_Q), BF16),
        compiler_params=_cparams(("parallel",)),
        name="attn_context",
    )(sink, q, k, v)


def _attn_lat_kernel(sink_ref, q_ref, kp_ref, kc_ref, kn_ref, vp_ref, vc_ref, vn_ref, kx_ref, vx_ref, o_ref):
    i = pl.program_id(1)
    nb = pl.num_programs(1)
    R = ATTN_GROUPS * QBLK
    r = lax.broadcasted_iota(I32, (R, QBLK), 0) % QBLK
    c = lax.broadcasted_iota(I32, (R, QBLK), 1)
    ok_prev = c >= r
    ok_next = c <= r
    edge_prev = jnp.where(i > 0, 0.0, NEG_INF)
    edge_next = jnp.where(i < nb - 1, 0.0, NEG_INF)
    for kv in range(ATTN_KV_HEADS):
        q = jnp.concatenate(
            [q_ref[:, (kv * ATTN_GROUPS + g) * HEAD_DIM:(kv * ATTN_GROUPS + g + 1) * HEAD_DIM]
             for g in range(ATTN_GROUPS)], axis=0)
        ksl = slice(kv * HEAD_DIM, (kv + 1) * HEAD_DIM)
        s_p = jnp.where(ok_prev, _dot_nt(q, kp_ref[:, ksl]) + edge_prev, NEG_INF)
        s_c = _dot_nt(q, kc_ref[:, ksl])
        s_n = jnp.where(ok_next, _dot_nt(q, kn_ref[:, ksl]) + edge_next, NEG_INF)
        s_x = _dot_nt(q, kx_ref[:, ksl])
        sk = _sink_column(sink_ref, kv, QBLK)
        m = jnp.maximum(
            jnp.maximum(jnp.max(s_p, axis=-1, keepdims=True), jnp.max(s_c, axis=-1, keepdims=True)),
            jnp.maximum(jnp.max(s_n, axis=-1, keepdims=True), jnp.max(s_x, axis=-1, keepdims=True)))
        m = jnp.maximum(m, sk)
        p_p, p_c, p_n, p_x = jnp.exp(s_p - m), jnp.exp(s_c - m), jnp.exp(s_n - m), jnp.exp(s_x - m)
        den = (jnp.sum(p_p, axis=-1, keepdims=True) + jnp.sum(p_c, axis=-1, keepdims=True)
               + jnp.sum(p_n, axis=-1, keepdims=True) + jnp.sum(p_x, axis=-1, keepdims=True) + jnp.exp(sk - m))
        o = (_dot(p_p.astype(BF16), vp_ref[:, ksl]) + _dot(p_c.astype(BF16), vc_ref[:, ksl])
             + _dot(p_n.astype(BF16), vn_ref[:, ksl]) + _dot(p_x.astype(BF16), vx_ref[:, ksl])) * (1.0 / den)
        for g in range(ATTN_GROUPS):
            hq = kv * ATTN_GROUPS + g
            o_ref[:, hq * HEAD_DIM:(hq + 1) * HEAD_DIM] = o[g * QBLK:(g + 1) * QBLK].astype(BF16)


def _attn_latent(sink, q, k, v, ctx_k, ctx_v, B, S):
    nb = S // QBLK
    P = ctx_k.shape[1]
    prev = lambda b, i: (b * nb + jnp.maximum(i - 1, 0), 0)
    cur = lambda b, i: (b * nb + i, 0)
    nxt = lambda b, i: (b * nb + jnp.minimum(i + 1, nb - 1), 0)
    kv_spec = lambda f: pl.BlockSpec((QBLK, ATTN_KV), f)
    return pl.pallas_call(
        _attn_lat_kernel,
        grid_spec=pltpu.PrefetchScalarGridSpec(
            num_scalar_prefetch=0,
            grid=(B, nb),
            in_specs=[
                pl.BlockSpec(memory_space=pltpu.SMEM),
                pl.BlockSpec((QBLK, ATTN_Q), cur),
                kv_spec(prev), kv_spec(cur), kv_spec(nxt),
                kv_spec(prev), kv_spec(cur), kv_spec(nxt),
                pl.BlockSpec((None, P, ATTN_KV), lambda b, i: (b, 0, 0)),
                pl.BlockSpec((None, P, ATTN_KV), lambda b, i: (b, 0, 0)),
            ],
            out_specs=pl.BlockSpec((QBLK, ATTN_Q), cur),
        ),
        out_shape=jax.ShapeDtypeStruct((B * S, ATTN_Q), BF16),
        compiler_params=_cparams(("parallel", "parallel")),
        name="attn_latent",
    )(sink, q, k, k, k, v, v, v, ctx_k, ctx_v)


SLAB = 8
U32 = jnp.uint32


def _pack_rows(x):
    half = x.shape[1] // 2
    bits = lax.bitcast_convert_type(x.astype(BF16).astype(F32), U32)
    return (bits[:, :half] >> 16) | (bits[:, half:] & jnp.uint32(0xFFFF0000))


def _store_slabs(ref, words):
    R = words.shape[0]
    for j in range(SLAB):
        ref[pl.ds(j, R, stride=SLAB), :] = words[:, j * 128:(j + 1) * 128]


def _load_slab_halves(ref, R, j):
    w = ref[pl.ds(j, R, stride=SLAB), :]
    return lax.bitcast_convert_type(w << 16, F32), lax.bitcast_convert_type(w & jnp.uint32(0xFFFF0000), F32)


def _load_rows_bf16(ref, R):
    halves = [_load_slab_halves(ref, R, j) for j in range(SLAB)]
    return jnp.concatenate([lo for lo, _ in halves] + [hi for _, hi in halves], axis=1).astype(BF16)


def _slab_copy(src, dst, sem, src_row8, dst_row):
    return pltpu.make_async_copy(src.at[pl.ds(pl.multiple_of(src_row8, SLAB), SLAB), :],
                                 dst.at[pl.ds(dst_row * SLAB, SLAB), :], sem)


def _router_kernel(x_ref, gain_ref, mod_ref, wr_ref, rb_ref, h_ref, e_ref, g_ref, r_ref, cnt_ref, cnt_scr, *, tm):
    E, GS = N_EXPERTS, GROUP_SIZE
    i = pl.program_id(0)

    @pl.when(i == 0)
    def _():
        cnt_scr[...] = jnp.zeros_like(cnt_scr)

    h = _norm_modulate(x_ref[...], gain_ref[...], mod_ref, 3, 4)
    _store_slabs(h_ref, _pack_rows(h))
    h_hi, h_lo = _split_hi_lo(h)
    w_hi, w_lo = _split_hi_lo(wr_ref[...])
    logits = _dot_nt(w_hi, h_hi) + _dot_nt(w_lo, h_hi) + _dot_nt(w_hi, h_lo)
    scores = jax.nn.sigmoid(logits)
    biased = scores + rb_ref[...]
    sub = lax.broadcasted_iota(I32, (GS, tm), 0)
    gscore = []
    for gi in range(N_EXPERT_GROUPS):
        xg = biased[gi * GS:(gi + 1) * GS, :]
        m1 = jnp.max(xg, axis=0, keepdims=True)
        first = jnp.min(jnp.where(xg == m1, sub, GS), axis=0, keepdims=True)
        m2 = jnp.max(jnp.where(sub == first, NEG_INF, xg), axis=0, keepdims=True)
        gscore.append(m1 + m2)
    cur = jnp.concatenate(gscore, axis=0)
    gid = lax.broadcasted_iota(I32, (N_EXPERT_GROUPS, tm), 0)
    gsel = jnp.zeros((N_EXPERT_GROUPS, tm), F32)
    for _ in range(TOPK_GROUPS):
        mx = jnp.max(cur, axis=0, keepdims=True)
        first = jnp.min(jnp.where(cur == mx, gid, N_EXPERT_GROUPS), axis=0, keepdims=True)
        hit = gid == first
        gsel = jnp.where(hit, 1.0, gsel)
        cur = jnp.where(hit, NEG_INF, cur)
    ok = jnp.concatenate(
        [jnp.broadcast_to(gsel[gi:gi + 1, :], (GS, tm)) for gi in range(N_EXPERT_GROUPS)], axis=0)
    masked = jnp.where(ok > 0.5, biased, NEG_INF)
    eid = lax.broadcasted_iota(I32, (E, tm), 0)
    sel = jnp.zeros((E, tm), F32)
    picks, pick_scores = [], []
    for _ in range(TOP_K):
        mx = jnp.max(masked, axis=0, keepdims=True)
        first = jnp.min(jnp.where(masked == mx, eid, E), axis=0, keepdims=True)
        hit = eid == first
        picks.append(first)
        pick_scores.append(jnp.sum(jnp.where(hit, scores, 0.0), axis=0, keepdims=True))
        sel = jnp.where(hit, 1.0, sel)
        masked = jnp.where(hit, NEG_INF, masked)
    total = pick_scores[0]
    for s in pick_scores[1:]:
        total = total + s
    g_ref[...] = jnp.concatenate(pick_scores, axis=0) / total * ROUTED_SCALE
    e_ref[...] = jnp.concatenate(picks, axis=0)
    rr = lax.broadcasted_iota(I32, (tm, tm), 0)
    cc = lax.broadcasted_iota(I32, (tm, tm), 1)
    before = jnp.where(rr < cc, 1.0, 0.0).astype(BF16)
    rank = _dot(sel.astype(BF16), before) + cnt_scr[...]
    r_ref[...] = jnp.concatenate(
        [jnp.sum(jnp.where(eid == p, rank, 0.0), axis=0, keepdims=True) for p in picks], axis=0).astype(I32)
    cnt_scr[...] = cnt_scr[...] + jnp.sum(sel, axis=-1, keepdims=True)
    cnt_ref[...] = cnt_scr[...].astype(I32)


def _router(x, gain, mod, router_w_t, router_b, *, n_prompt, dec_seq, tm):
    N, D = x.shape
    E = N_EXPERTS
    assert D == 2 * SLAB * 128
    assert N % tm == 0 and n_prompt % tm == 0 and dec_seq % tm == 0
    row = functools.partial(_mod_row, tm=tm, n_prompt=n_prompt, dec_seq=dec_seq)
    return pl.pallas_call(
        functools.partial(_router_kernel, tm=tm),
        grid=(N // tm,),
        in_specs=[
            pl.BlockSpec((tm, D), lambda i: (i, 0)),
            pl.BlockSpec((1, D), lambda i: (0, 0)),
            pl.BlockSpec((None, 1, 6 * D), lambda i: (row(i), 0, 0)),
            pl.BlockSpec((E, D), lambda i: (0, 0)),
            pl.BlockSpec((E, 1), lambda i: (0, 0)),
        ],
        out_specs=[
            pl.BlockSpec((tm * SLAB, 128), lambda i: (i, 0)),
            pl.BlockSpec((TOP_K, tm), lambda i: (0, i)),
            pl.BlockSpec((TOP_K, tm), lambda i: (0, i)),
            pl.BlockSpec((TOP_K, tm), lambda i: (0, i)),
            pl.BlockSpec((E, 1), lambda i: (0, 0)),
        ],
        out_shape=[
            jax.ShapeDtypeStruct((N * SLAB, 128), U32),
            jax.ShapeDtypeStruct((TOP_K, N), I32),
            jax.ShapeDtypeStruct((TOP_K, N), F32),
            jax.ShapeDtypeStruct((TOP_K, N), I32),
            jax.ShapeDtypeStruct((E, 1), I32),
        ],
        scratch_shapes=[pltpu.VMEM((E, 1), F32)],
        compiler_params=_cparams(("arbitrary",)),
        name="moe_router",
    )(x, gain.reshape(1, D), mod, router_w_t, router_b.reshape(E, 1))


def _experts_kernel(be_ref, nused_ref, tok_cur, tok_nxt, h_hbm, w1_ref, w3_ref, w2_ref, y_ref,
                    xbuf, sems, w1c, w3c, w2c, prev_e):
    b = pl.program_id(0)
    e = be_ref[b]
    n_used = nused_ref[0]
    BLK = xbuf.shape[1] // SLAB
    slot = b % 2

    FF = w1c.shape[1]
    D = w2c.shape[1]
    CW = 256
    n_chunks = 2 * (FF // CW) + D // CW
    per_chunk = BLK // n_chunks + 1

    def gather(tok_ref, s, lo=0, hi=BLK):
        for r in range(lo, min(hi, BLK)):
            _slab_copy(h_hbm, xbuf.at[s], sems.at[s], tok_ref[0, r], r).start()

    def drain(s):
        for r in range(BLK):
            _slab_copy(h_hbm, xbuf.at[s], sems.at[s], 0, r).wait()

    @pl.when(b == 0)
    def _():
        prev_e[0] = -1
        gather(tok_cur, 0)

    @pl.when(b < n_used)
    def _():
        @pl.when(e != prev_e[0])
        def _():
            w1c[...] = w1_ref[...].astype(BF16)
            w3c[...] = w3_ref[...].astype(BF16)
            w2c[...] = w2_ref[...].astype(BF16)
            prev_e[0] = e

        drain(slot)
        chunk = [0]

        def gather_some():
            gather(tok_nxt, 1 - slot, chunk[0] * per_chunk, (chunk[0] + 1) * per_chunk)
            chunk[0] += 1

        x = _load_rows_bf16(xbuf.at[slot], BLK)
        mids = []
        for c in range(FF // CW):
            sl = slice(c * CW, (c + 1) * CW)
            gather_some()
            a = _dot(x, w1c[:, sl])
            gather_some()
            mids.append((_silu(a) * _dot(x, w3c[:, sl])).astype(BF16))
        mid = jnp.concatenate(mids, axis=1)
        half = D // 2
        for c in range(half // CW):
            gather_some()
            y_lo = _dot(mid, w2c[:, c * CW:(c + 1) * CW])
            gather_some()
            y_hi = _dot(mid, w2c[:, half + c * CW:half + (c + 1) * CW])
            words = _pack_rows(jnp.concatenate([y_lo, y_hi], axis=1))
            for jj in range(CW // 128):
                j = c * (CW // 128) + jj
                y_ref[pl.ds(j, BLK, stride=SLAB), :] = words[:, jj * 128:(jj + 1) * 128]

    @pl.when(b >= n_used)
    def _():
        @pl.when(b == n_used)
        def _():
            drain(slot)

        y_ref[...] = jnp.zeros_like(y_ref)


def _experts(block_expert, n_used, slot_tok8, h, w1, w3, w2, layer):
    n_blocks = block_expert.shape[0]
    BLK = MOE_BLOCK
    _, _, D, FF = w1.shape
    slot_tok3 = slot_tok8.reshape(n_blocks, 1, BLK)
    return pl.pallas_call(
        _experts_kernel,
        grid_spec=pltpu.PrefetchScalarGridSpec(
            num_scalar_prefetch=2,
            grid=(n_blocks,),
            in_specs=[
                pl.BlockSpec((None, 1, BLK), lambda b, be, nu: (b, 0, 0), memory_space=pltpu.SMEM),
                pl.BlockSpec((None, 1, BLK), lambda b, be, nu: (jnp.minimum(b + 1, n_blocks - 1), 0, 0),
                             memory_space=pltpu.SMEM),
                pl.BlockSpec(memory_space=pl.ANY),
                pl.BlockSpec((None, None, D, FF), lambda b, be, nu: (layer, be[b], 0, 0)),
                pl.BlockSpec((None, None, D, FF), lambda b, be, nu: (layer, be[b], 0, 0)),
                pl.BlockSpec((None, None, FF, D), lambda b, be, nu: (layer, be[b], 0, 0)),
            ],
            out_specs=pl.BlockSpec((BLK * SLAB, 128), lambda b, be, nu: (b, 0)),
            scratch_shapes=[
                pltpu.VMEM((2, BLK * SLAB, 128), U32),
                pltpu.SemaphoreType.DMA((2,)),
                pltpu.VMEM((D, FF), BF16),
                pltpu.VMEM((D, FF), BF16),
                pltpu.VMEM((FF, D), BF16),
                pltpu.SMEM((1,), I32),
            ],
        ),
        out_shape=jax.ShapeDtypeStruct((n_blocks * BLK * SLAB, 128), U32),
        compiler_params=_cparams(("arbitrary",)),
        name="moe_experts",
    )(block_expert, n_used, slot_tok3, slot_tok3, h, w1, w3, w2)


def _combine_kernel(dest_cur, dest_nxt, gates_ref, y_hbm, h_ref, x_ref, gate_ref, ws1_ref, ws3_ref, ws2_ref,
                    *rest, prompt_tiles):
    if prompt_tiles is None:
        o_ref, ybuf0, ybuf1, sems = rest
    else:
        op_ref, os_ref, ybuf0, ybuf1, sems = rest
    tm = x_ref.shape[0]
    i = pl.program_id(0)
    last = pl.num_programs(0) - 1

    def gather(dest_ref, buf, sem, lo=0, hi=tm):
        for k in range(TOP_K):
            for r in range(lo, hi):
                _slab_copy(y_hbm, buf.at[k], sem, dest_ref[k, r], r).start()

    def drain(buf, sem):
        for k in range(TOP_K):
            for r in range(tm):
                _slab_copy(y_hbm, buf.at[k], sem, 0, r).wait()

    @pl.when(i == 0)
    def _():
        gather(dest_cur, ybuf0, sems.at[0])

    def step(cur, cur_sem, nxt, nxt_sem):
        drain(cur, cur_sem)
        per_group = tm // (SLAB + 2)
        gather(dest_nxt, nxt, nxt_sem, 0, per_group)
        hb = _load_rows_bf16(h_ref, tm)
        mid = (_silu(_dot(hb, ws1_ref[...])) * _dot(hb, ws3_ref[...])).astype(BF16)
        gather(dest_nxt, nxt, nxt_sem, per_group, 2 * per_group)
        shared = _dot(mid, ws2_ref[...])
        g = [gates_ref[:, k:k + 1] for k in range(TOP_K)]
        lo_parts, hi_parts = [], []
        for j in range(SLAB):
            gather(dest_nxt, nxt, nxt_sem, (j + 2) * per_group, tm if j == SLAB - 1 else (j + 3) * per_group)
            lo_acc = hi_acc = None
            for k in range(TOP_K):
                lo, hi = _load_slab_halves(cur.at[k], tm, j)
                lo_acc = g[k] * lo if lo_acc is None else lo_acc + g[k] * lo
                hi_acc = g[k] * hi if hi_acc is None else hi_acc + g[k] * hi
            lo_parts.append(lo_acc)
            hi_parts.append(hi_acc)
        routed = jnp.concatenate(lo_parts + hi_parts, axis=1)
        out = x_ref[...] + gate_ref[...] * (routed + shared)
        if prompt_tiles is None:
            o_ref[...] = out
        else:
            @pl.when(i < prompt_tiles)
            def _():
                op_ref[...] = out

            @pl.when(i >= prompt_tiles)
            def _():
                os_ref[...] = out

        @pl.when(i == last)
        def _():
            drain(nxt, nxt_sem)

    @pl.when(i % 2 == 0)
    def _():
        step(ybuf0, sems.at[0], ybuf1, sems.at[1])

    @pl.when(i % 2 == 1)
    def _():
        step(ybuf1, sems.at[1], ybuf0, sems.at[0])


def _combine(dest, gates, y, h, x, mod, ws1, ws3, ws2, *, n_prompt, dec_seq, split_outputs):
    N, D = x.shape
    tm = COMBINE_TM
    FF = ws1.shape[1]
    assert N % tm == 0 and n_prompt % tm == 0 and dec_seq % tm == 0
    nt = N // tm
    row = functools.partial(_mod_row, tm=tm, n_prompt=n_prompt, dec_seq=dec_seq)
    dest3 = dest.reshape(TOP_K, nt, tm).transpose(1, 0, 2)
    if split_outputs:
        pt = n_prompt // tm
        out_specs = [pl.BlockSpec((tm, D), lambda i: (jnp.minimum(i, pt - 1), 0)),
                     pl.BlockSpec((tm, D), lambda i: (jnp.maximum(i - pt, 0), 0))]
        out_shape = [jax.ShapeDtypeStruct((n_prompt, D), F32), jax.ShapeDtypeStruct((N - n_prompt, D), F32)]
    else:
        pt = None
        out_specs = pl.BlockSpec((tm, D), lambda i: (i, 0))
        out_shape = jax.ShapeDtypeStruct((N, D), F32)
    return pl.pallas_call(
        functools.partial(_combine_kernel, prompt_tiles=pt),
        grid=(nt,),
        in_specs=[
            pl.BlockSpec((None, TOP_K, tm), lambda i: (i, 0, 0), memory_space=pltpu.SMEM),
            pl.BlockSpec((None, TOP_K, tm), lambda i: (jnp.minimum(i + 1, nt - 1), 0, 0), memory_space=pltpu.SMEM),
            pl.BlockSpec((tm, TOP_K), lambda i: (i, 0)),
            pl.BlockSpec(memory_space=pl.ANY),
            pl.BlockSpec((tm * SLAB, 128), lambda i: (i, 0)),
            pl.BlockSpec((tm, D), lambda i: (i, 0)),
            pl.BlockSpec((None, 1, D), lambda i: (row(i), 0, 5)),
            pl.BlockSpec((D, FF), lambda i: (0, 0)),
            pl.BlockSpec((D, FF), lambda i: (0, 0)),
            pl.BlockSpec((FF, D), lambda i: (0, 0)),
        ],
        out_specs=out_specs,
        out_shape=out_shape,
        scratch_shapes=[pltpu.VMEM((TOP_K, tm * SLAB, 128), U32), pltpu.VMEM((TOP_K, tm * SLAB, 128), U32),
                        pltpu.SemaphoreType.DMA((2,))],
        compiler_params=_cparams(("arbitrary",)),
        name="moe_combine",
    )(dest3, dest3, gates, y, h, x, mod, ws1, ws3, ws2)


def _moe_layer(x, gain, mod, router_w, router_b, w1, w3, w2, layer, ws1, ws3, ws2, *, n_prompt, dec_seq,
               split_outputs=False):
    N = x.shape[0]
    E, BLK = N_EXPERTS, MOE_BLOCK
    h, top_e, gates, rank, counts = _router(x, gain, mod, router_w.T, router_b,
                                            n_prompt=n_prompt, dec_seq=dec_seq,
                                            tm=_row_tile(512, n_prompt, dec_seq))
    counts = counts[:, 0]
    padded = (counts + BLK - 1) // BLK * BLK
    pad_end = jnp.cumsum(padded)
    pad_start = pad_end - padded
    n_blocks = N * TOP_K // BLK + E
    block_expert = jnp.minimum(
        jnp.sum(pad_end[None, :] <= (jnp.arange(n_blocks, dtype=I32) * BLK)[:, None], axis=1), E - 1).astype(I32)
    n_used = (pad_end[-1:] // BLK).astype(I32)
    onehot = top_e[None] == jnp.arange(E, dtype=I32)[:, None, None]
    dest = jnp.sum(jnp.where(onehot, pad_start[:, None, None], 0), axis=0) + rank
    NP = N + BLK
    keys = (top_e * NP + jnp.arange(N, dtype=I32)[None, :]).reshape(-1)
    fill_i = jnp.arange(BLK, dtype=I32)[None, :]
    fill_e = jnp.arange(E, dtype=I32)[:, None]
    fillers = jnp.where(fill_i < (padded - counts)[:, None], fill_e * NP + N + fill_i, E * NP + fill_e * BLK + fill_i)
    slot_t = lax.sort(jnp.concatenate([keys, fillers.reshape(-1)])) % NP
    spread = jnp.arange(n_blocks * BLK, dtype=I32) % N
    slot_tok8 = jnp.where(slot_t < N, slot_t, spread) * SLAB
    y = _experts(block_expert, n_used, slot_tok8, h, w1, w3, w2, layer)
    return _combine(dest * SLAB, gates.T, y, h, x, mod, ws1, ws3, ws2, n_prompt=n_prompt, dec_seq=dec_seq,
                    split_outputs=split_outputs)


def _rope_tables(S):
    quarter = HEAD_DIM // 4
    pos = jnp.arange(S)
    row_id = (pos // GRID_W).astype(F32)
    col_id = (pos % GRID_W).astype(F32)
    inv = ROPE_THETA ** (-jnp.arange(quarter, dtype=F32) / quarter)
    ar, ac = row_id[:, None] * inv, col_id[:, None] * inv
    cos = jnp.concatenate([jnp.cos(ar), jnp.cos(ar), jnp.cos(ac), jnp.cos(ac)], axis=-1)
    sin = jnp.concatenate([-jnp.sin(ar), jnp.sin(ar), -jnp.sin(ac), jnp.sin(ac)], axis=-1)
    return cos, sin


def kernel(x_prompt, x_sample, state_mlstm_C, state_mlstm_n, state_mlstm_m, cache_attn_k, cache_attn_v,
           c, c_ctx, ada_w, ada_b, norm_mix, norm_ffn,
           mlstm_w_in, mlstm_gate_b, mlstm_head_g, mlstm_w_out,
           attn_w_qkv, attn_q_g, attn_k_g, attn_sink, attn_w_o,
           moe_router_w, moe_router_b, moe_w1, moe_w3, moe_w2, shared_w1, shared_w3, shared_w2):
    D = D_MODEL
    Bp, Sp, _ = x_prompt.shape
    Bs, Ss, _ = x_sample.shape
    n_prompt = Bp * Sp
    N = n_prompt + Bs * Ss
    dims = dict(n_prompt=n_prompt, dec_seq=Ss)
    tm512 = _row_tile(512, n_prompt, Ss)

    x_parts = (x_prompt.reshape(n_prompt, D), x_sample.reshape(Bs * Ss, D))
    rows = 16
    cvec = jnp.zeros((rows, D), F32).at[0].set(c_ctx).at[1:1 + Bs].set(c)
    mod_all = _ada_table(cvec, ada_w, ada_b)[:, :1 + Bs].reshape(DEPTH, 1 + Bs, 1, 6 * D)

    H = MLSTM_HEADS
    mod = mod_all[0]
    w_in = mlstm_w_in[0]
    w_main = w_in[:, :MLSTM_MAIN].astype(BF16)
    w_gate = jnp.pad(w_in[:, MLSTM_MAIN:], ((0, 0), (0, GATE_PAD - 4 * H))).astype(BF16)
    gate_b = jnp.pad(mlstm_gate_b[0], (0, GATE_PAD - 4 * H))
    col_scale = jnp.concatenate([jnp.full((MLSTM_QK,), MLSTM_DK ** -0.5, F32),
                                 jnp.ones((MLSTM_MAIN - MLSTM_QK,), F32)])
    qkvo, gates, x = _nm_matmul(x_parts, norm_mix[0], mod, w_main, col_scale, jnp.zeros((MLSTM_MAIN,), F32),
                                shift_idx=0, scale_idx=1, out_dtype=BF16, tm=tm512, tn=1024,
                                aux_w=w_gate, aux_b=gate_b, name="mlstm_proj", **dims)
    L = 256
    state = (state_mlstm_C[:, 0], state_mlstm_n[:, 0], state_mlstm_m[:, 0])
    hf, hb, C_p, n_p, m_p = _mlstm_scan(qkvo, gates, Bp, Sp, Bs, Ss, L, state)
    x = _mm_residual((hf, hb, qkvo, mlstm_head_g[0]), mlstm_w_out[0].astype(BF16), x, mod, gate_idx=2,
                     tm=_row_tile(256, n_prompt, Ss), tn=D, mlstm_prologue=True, name="mlstm_out", **dims)
    x = _moe_layer(x, norm_ffn[0], mod, moe_router_w[0], moe_router_b[0], moe_w1, moe_w3, moe_w2, 0,
                   shared_w1[0].astype(BF16), shared_w3[0].astype(BF16), shared_w2[0].astype(BF16), **dims)

    mod = mod_all[1]
    (qkv,) = _nm_matmul(x, norm_mix[1], mod, attn_w_qkv[0].astype(BF16), jnp.ones((ATTN_PROJ,), F32),
                        jnp.zeros((ATTN_PROJ,), F32), shift_idx=0, scale_idx=1, out_dtype=F32, tm=tm512, tn=1024,
                        name="attn_qkv", **dims)
    ident_cos = jnp.ones((Sp, HEAD_DIM), F32)
    ident_sin = jnp.zeros((Sp, HEAD_DIM), F32)
    q_p, k_p, v_p, kf_p, vf_p = _qk_prep(qkv, attn_q_g[0], attn_k_g[0], ident_cos, ident_sin, 0, n_prompt, Sp,
                                         True, "qk_prep_prompt")
    cos, sin = _rope_tables(Ss)
    q_s, k_s, v_s = _qk_prep(qkv, attn_q_g[0], attn_k_g[0], cos, sin, n_prompt, Bs * Ss, 256, False,
                             "qk_prep_latent")
    o_p = _attn_context(attn_sink[0], q_p, k_p, v_p, Bp, Sp)
    P = cache_attn_k.shape[2]
    ctx_k = cache_attn_k[:, 0].reshape(Bs, P, ATTN_KV).astype(BF16)
    ctx_v = cache_attn_v[:, 0].reshape(Bs, P, ATTN_KV).astype(BF16)
    o_s = _attn_latent(attn_sink[0], q_s, k_s, v_s, ctx_k, ctx_v, Bs, Ss)
    x = _mm_residual((o_p, o_s), attn_w_o[0].astype(BF16), x, mod, gate_idx=2,
                     tm=tm512, tn=D,
                     mlstm_prologue=False, name="attn_out", **dims)
    y_prompt, y_sample = _moe_layer(
        x, norm_ffn[1], mod, moe_router_w[1], moe_router_b[1], moe_w1, moe_w3, moe_w2, 1,
        shared_w1[1].astype(BF16), shared_w3[1].astype(BF16), shared_w2[1].astype(BF16), split_outputs=True, **dims)
    return (y_prompt.reshape(Bp, Sp, D), y_sample.reshape(Bs, Ss, D), C_p[:, None], n_p[:, None], m_p[:, None],
            kf_p.reshape(Bp, 1, Sp, ATTN_KV_HEADS, HEAD_DIM), vf_p.reshape(Bp, 1, Sp, ATTN_KV_HEADS, HEAD_DIM))
```

```python
import functools

import jax
import jax.numpy as jnp
from jax import lax
from jax.experimental import pallas as pl
from jax.experimental.pallas import tpu as pltpu

F32 = jnp.float32
BF16 = jnp.bfloat16
I32 = jnp.int32

D_MODEL = 2048
DEPTH = 2
EPS = 1e-6
GRID_W = 64
MLSTM_HEADS = 8
MLSTM_DK = 128
MLSTM_DV = 256
MLSTM_QK = MLSTM_HEADS * MLSTM_DK
MLSTM_V = MLSTM_HEADS * MLSTM_DV
MLSTM_MAIN = 2 * MLSTM_QK + 2 * MLSTM_V
GATE_PAD = 128
HEAD_DIM = 128
ATTN_Q_HEADS = 16
ATTN_KV_HEADS = 4
ATTN_GROUPS = 4
WINDOW = 128
QBLK = 128
ROPE_THETA = 10000.0
ATTN_Q = ATTN_Q_HEADS * HEAD_DIM
ATTN_KV = ATTN_KV_HEADS * HEAD_DIM
ATTN_PROJ = ATTN_Q + 2 * ATTN_KV
N_EXPERTS = 64
TOP_K = 8
N_EXPERT_GROUPS = 8
TOPK_GROUPS = 4
GROUP_SIZE = N_EXPERTS // N_EXPERT_GROUPS
EXPERT_FF = 512
ROUTED_SCALE = 2.5
MOE_BLOCK = 512
COMBINE_TM = 128

V7X_VMEM_LIMIT = 56 * 1024 * 1024
NEG_INF = float("-inf")


def _cparams(sem):
    return pltpu.CompilerParams(dimension_semantics=("arbitrary",) * len(sem), vmem_limit_bytes=V7X_VMEM_LIMIT)


def _split_hi_lo(a):
    hi = a.astype(BF16)
    lo = (a - hi.astype(F32)).astype(BF16)
    return hi, lo


def _dot(a, b):
    return jnp.dot(a, b, preferred_element_type=F32)


def _dot_nt(a, b):
    return lax.dot_general(a, b, (((1,), (1,)), ((), ())), preferred_element_type=F32)


def _dot_tn(a, b):
    return lax.dot_general(a, b, (((0,), (0,)), ((), ())), preferred_element_type=F32)


def _silu(x):
    return x * jax.nn.sigmoid(x)


def _row_tile(preferred, n_prompt, dec_seq):
    tm = preferred
    while n_prompt % tm or dec_seq % tm:
        tm //= 2
    return tm


def _mod_row(i, tm, n_prompt, dec_seq):
    r0 = i * tm
    return jnp.where(r0 < n_prompt, 0, 1 + (r0 - n_prompt) // dec_seq)


def _norm_modulate(x, gain, mod_ref, shift_idx, scale_idx):
    D = D_MODEL
    y = x * lax.rsqrt(jnp.mean(x * x, axis=-1, keepdims=True) + EPS) * gain
    shift = mod_ref[:, shift_idx * D:(shift_idx + 1) * D]
    scale = mod_ref[:, scale_idx * D:(scale_idx + 1) * D]
    return y * (1.0 + scale) + shift


def _ada_kernel(c_ref, w_ref, b_ref, o_ref):
    s = _silu(c_ref[...])
    s_hi, s_lo = _split_hi_lo(s)
    w_hi, w_lo = _split_hi_lo(w_ref[...])
    o_ref[...] = _dot(s_hi, w_hi) + _dot(s_hi, w_lo) + _dot(s_lo, w_hi) + b_ref[...]


def _ada_table(cvec, ada_w, ada_b):
    D = D_MODEL
    tn = 1024
    rows = cvec.shape[0]
    return pl.pallas_call(
        _ada_kernel,
        grid=(DEPTH, 6 * D // tn),
        in_specs=[
            pl.BlockSpec((rows, D), lambda l, j: (0, 0)),
            pl.BlockSpec((None, D, tn), lambda l, j: (l, 0, j)),
            pl.BlockSpec((None, 1, tn), lambda l, j: (l, 0, j)),
        ],
        out_specs=pl.BlockSpec((None, rows, tn), lambda l, j: (l, 0, j)),
        out_shape=jax.ShapeDtypeStruct((DEPTH, rows, 6 * D), F32),
        compiler_params=_cparams(("parallel", "parallel")),
        name="ada_table",
    )(cvec, ada_w, ada_b.reshape(DEPTH, 1, 6 * D))


def _nm_matmul_kernel(*refs, shift_idx, scale_idx, has_aux, prompt_tiles):
    if prompt_tiles is None:
        x_ref, gain_ref, mod_ref, w_ref, cs_ref, cb_ref, *rest = refs
    else:
        xp_ref, xs_ref, gain_ref, mod_ref, w_ref, cs_ref, cb_ref, *rest = refs
    if has_aux:
        wa_ref, ab_ref, o_ref, aux_ref, *tail = rest
    else:
        o_ref, *tail = rest
    if prompt_tiles is None:
        (h_scr,) = tail
    else:
        xcat_ref, h_scr = tail

    def prologue(x):
        h = _norm_modulate(x, gain_ref[...], mod_ref, shift_idx, scale_idx).astype(BF16)
        h_scr[...] = h
        if has_aux:
            aux_ref[...] = _dot(h, wa_ref[...]) + ab_ref[...]
        if prompt_tiles is not None:
            xcat_ref[...] = x

    first_col = pl.program_id(1) == 0
    if prompt_tiles is None:
        @pl.when(first_col)
        def _():
            prologue(x_ref[...])
    else:
        context = pl.program_id(0) < prompt_tiles

        @pl.when(first_col & context)
        def _():
            prologue(xp_ref[...])

        @pl.when(first_col & jnp.logical_not(context))
        def _():
            prologue(xs_ref[...])

    acc = _dot(h_scr[...], w_ref[...])
    o_ref[...] = (acc * cs_ref[...] + cb_ref[...]).astype(o_ref.dtype)


def _nm_matmul(x, gain, mod, w, col_scale, col_bias, *, shift_idx, scale_idx, n_prompt, dec_seq,
               out_dtype, tm, tn, aux_w=None, aux_b=None, name):
    split = isinstance(x, tuple)
    D = w.shape[0]
    N = sum(p.shape[0] for p in x) if split else x.shape[0]
    P = w.shape[1]
    assert N % tm == 0 and P % tn == 0 and n_prompt % tm == 0 and dec_seq % tm == 0
    has_aux = aux_w is not None
    row = functools.partial(_mod_row, tm=tm, n_prompt=n_prompt, dec_seq=dec_seq)
    pt = n_prompt // tm
    if split:
        x_specs = [pl.BlockSpec((tm, D), lambda i, j: (jnp.minimum(i, pt - 1), 0)),
                   pl.BlockSpec((tm, D), lambda i, j: (jnp.maximum(i - pt, 0), 0))]
        x_args = list(x)
    else:
        x_specs = [pl.BlockSpec((tm, D), lambda i, j: (i, 0))]
        x_args = [x]
    in_specs = x_specs + [
        pl.BlockSpec((1, D), lambda i, j: (0, 0)),
        pl.BlockSpec((None, 1, 6 * D), lambda i, j: (row(i), 0, 0)),
        pl.BlockSpec((D, tn), lambda i, j: (0, j)),
        pl.BlockSpec((1, tn), lambda i, j: (0, j)),
        pl.BlockSpec((1, tn), lambda i, j: (0, j)),
    ]
    args = x_args + [gain.reshape(1, D), mod, w, col_scale.reshape(1, P), col_bias.reshape(1, P)]
    out_specs = [pl.BlockSpec((tm, tn), lambda i, j: (i, j))]
    out_shape = [jax.ShapeDtypeStruct((N, P), out_dtype)]
    if has_aux:
        PA = aux_w.shape[1]
        in_specs += [pl.BlockSpec((D, PA), lambda i, j: (0, 0)), pl.BlockSpec((1, PA), lambda i, j: (0, 0))]
        args += [aux_w, aux_b.reshape(1, PA)]
        out_specs.append(pl.BlockSpec((tm, PA), lambda i, j: (i, 0)))
        out_shape.append(jax.ShapeDtypeStruct((N, PA), F32))
    if split:
        out_specs.append(pl.BlockSpec((tm, D), lambda i, j: (i, 0)))
        out_shape.append(jax.ShapeDtypeStruct((N, D), F32))
    return pl.pallas_call(
        functools.partial(_nm_matmul_kernel, shift_idx=shift_idx, scale_idx=scale_idx, has_aux=has_aux,
                          prompt_tiles=pt if split else None),
        grid=(N // tm, P // tn),
        in_specs=in_specs,
        out_specs=out_specs,
        out_shape=out_shape,
        scratch_shapes=[pltpu.VMEM((tm, D), BF16)],
        compiler_params=_cparams(("parallel", "arbitrary")),
        name=name,
    )(*args)


def _mm_residual_kernel(*refs, mlstm_prologue, prompt_tiles):
    if mlstm_prologue:
        hf_ref, hb_ref, og_ref, hg_ref, w_ref, x_ref, gate_ref, o_ref, l_scr = refs

        @pl.when(pl.program_id(1) == 0)
        def _():
            hs = hf_ref[...].astype(F32) + hb_ref[...].astype(F32)
            og = jax.nn.sigmoid(og_ref[...].astype(F32))
            for h in range(MLSTM_HEADS):
                sl = slice(h * MLSTM_DV, (h + 1) * MLSTM_DV)
                hh = hs[:, sl]
                hn = hh * lax.rsqrt(jnp.mean(hh * hh, axis=-1, keepdims=True) + EPS) * hg_ref[:, sl]
                l_scr[:, sl] = (og[:, sl] * hn).astype(BF16)

        o_ref[...] = x_ref[...] + gate_ref[...] * _dot(l_scr[...], w_ref[...])
    else:
        lp_ref, ls_ref, w_ref, x_ref, gate_ref, o_ref = refs
        context = pl.program_id(0) < prompt_tiles

        @pl.when(context)
        def _():
            o_ref[...] = x_ref[...] + gate_ref[...] * _dot(lp_ref[...], w_ref[...])

        @pl.when(jnp.logical_not(context))
        def _():
            o_ref[...] = x_ref[...] + gate_ref[...] * _dot(ls_ref[...], w_ref[...])


def _mm_residual(lhs_args, w, x, mod, *, gate_idx, n_prompt, dec_seq, tm, tn, mlstm_prologue, name):
    N, D = x.shape
    K = w.shape[0]
    assert N % tm == 0 and D % tn == 0 and n_prompt % tm == 0 and dec_seq % tm == 0
    row = functools.partial(_mod_row, tm=tm, n_prompt=n_prompt, dec_seq=dec_seq)
    pt = n_prompt // tm
    if mlstm_prologue:
        hf, hb, qkvo, head_g = lhs_args
        o_blk = (2 * MLSTM_QK + MLSTM_V) // MLSTM_V
        in_specs = [
            pl.BlockSpec((tm, K), lambda i, j: (i, 0)),
            pl.BlockSpec((tm, K), lambda i, j: (i, 0)),
            pl.BlockSpec((tm, MLSTM_V), lambda i, j: (i, o_blk)),
            pl.BlockSpec((1, K), lambda i, j: (0, 0)),
        ]
        args = [hf, hb, qkvo, head_g.reshape(1, K)]
        scratch = [pltpu.VMEM((tm, K), BF16)]
    else:
        lhs_p, lhs_s = lhs_args
        in_specs = [pl.BlockSpec((tm, K), lambda i, j: (jnp.minimum(i, pt - 1), 0)),
                    pl.BlockSpec((tm, K), lambda i, j: (jnp.maximum(i - pt, 0), 0))]
        args = [lhs_p, lhs_s]
        scratch = []
    in_specs += [
        pl.BlockSpec((K, tn), lambda i, j: (0, j)),
        pl.BlockSpec((tm, tn), lambda i, j: (i, j)),
        pl.BlockSpec((None, 1, tn), lambda i, j: (row(i), 0, gate_idx * (D // tn) + j)),
    ]
    args += [w, x, mod]
    return pl.pallas_call(
        functools.partial(_mm_residual_kernel, mlstm_prologue=mlstm_prologue, prompt_tiles=pt),
        grid=(N // tm, D // tn),
        in_specs=in_specs,
        out_specs=pl.BlockSpec((tm, tn), lambda i, j: (i, j)),
        out_shape=jax.ShapeDtypeStruct((N, D), F32),
        scratch_shapes=scratch,
        compiler_params=_cparams(("parallel", "arbitrary")),
        name=name,
    )(*args)


def _log_sigmoid(x):
    return jnp.minimum(x, 0.0) - jnp.log(1.0 + jnp.exp(-jnp.abs(x)))


def _mlstm_direction(d, q_ref, k_ref, v_ref, g_ref, h_ref, C_scr, n_scr, m_scr, L):
    H, DK, DV = MLSTM_HEADS, MLSTM_DK, MLSTM_DV
    g = g_ref[...]
    lf = _log_sigmoid(g)
    r = lax.broadcasted_iota(I32, (L, L), 0)
    c = lax.broadcasted_iota(I32, (L, L), 1)
    causal = (c <= r) if d == 0 else (c >= r)
    tri = jnp.where(causal, 1.0, 0.0).astype(BF16)
    lf1 = lf.astype(BF16)
    rem = lf - lf1.astype(F32)
    lf2 = rem.astype(BF16)
    lf3 = (rem - lf2.astype(F32)).astype(BF16)
    bsum = _dot(tri, lf1) + _dot(tri, lf2) + _dot(tri, lf3)
    g_t = g.T
    b_t = bsum.T
    end = L - 1 if d == 0 else 0
    m_old = [m_scr[d, h] for h in range(H)]
    n_old = [n_scr[d, h] for h in range(H)]
    m_upd, n_upd = [], []
    for h in range(H):
        ci, cf = d * 2 * H + h, d * 2 * H + H + h
        b_col, i_col = bsum[:, cf:cf + 1], g[:, ci:ci + 1]
        b_row, i_row = b_t[cf:cf + 1, :], g_t[ci:ci + 1, :]
        b_end = bsum[end:end + 1, cf:cf + 1]
        m, n = m_old[h], n_old[h]
        C = C_scr[d * H + h][...]
        qh = q_ref[:, h * DK:(h + 1) * DK]
        kh = k_ref[:, h * DK:(h + 1) * DK]
        vh = v_ref[:, h * DV:(h + 1) * DV]
        dmat = jnp.where(causal, b_col - b_row + i_row, NEG_INF)
        inter = b_col + m
        m_out = jnp.maximum(inter, jnp.max(dmat, axis=-1, keepdims=True))
        w = jnp.exp(dmat - m_out) * _dot_nt(qh, kh)
        dec = jnp.exp(inter - m_out)
        num = _dot(w.astype(BF16), vh) + dec * _dot(qh, C.astype(BF16))
        den = jnp.sum(w, axis=-1, keepdims=True) + dec * jnp.sum(qh.astype(F32) * n, axis=-1, keepdims=True)
        h_ref[:, h * DV:(h + 1) * DV] = (num / jnp.maximum(jnp.abs(den), jnp.exp(-m_out))).astype(h_ref.dtype)
        to_end = b_end - b_col + i_col
        m_new = jnp.maximum(b_end + m, jnp.max(to_end, axis=0, keepdims=True))
        wk = jnp.exp(to_end - m_new)
        dec_end = jnp.exp(b_end + m - m_new)
        kw = kh.astype(F32) * wk
        C_scr[d * H + h][...] = dec_end * C + _dot_tn(kw.astype(BF16), vh)
        n_upd.append(dec_end * n + jnp.sum(kw, axis=0, keepdims=True))
        m_upd.append(m_new)
    for h in range(H):
        n_scr[d, h] = n_upd[h]
        m_scr[d, h] = m_upd[h]


def _mlstm_kernel(fwd_ref, bwd_ref, bidx_ref, flag_ref, qf, kf, vf, gf, qb, kb, vb, gb, C0, n0, m0,
                  hf, hb, Co, no, mo, *scratch, L):
    H = MLSTM_HEADS
    C_scr, (n_scr, m_scr) = scratch[:2 * H], scratch[2 * H:]
    flags = flag_ref[pl.program_id(0)]
    first, last, context = (flags & 1) != 0, (flags & 2) != 0, (flags & 4) != 0

    @pl.when(first & context)
    def _():
        for C in C_scr:
            C[...] = jnp.zeros_like(C)
        n_scr[...] = jnp.zeros_like(n_scr)
        m_scr[...] = jnp.zeros_like(m_scr)

    @pl.when(first & jnp.logical_not(context))
    def _():
        for d in range(2):
            for h in range(H):
                C_scr[d * H + h][...] = C0[d, h]
                n_scr[d, h] = n0[d, h:h + 1, :]
                m_scr[d, h] = m0[d:d + 1, h:h + 1]

    _mlstm_direction(0, qf, kf, vf, gf, hf, C_scr, n_scr, m_scr, L)
    _mlstm_direction(1, qb, kb, vb, gb, hb, C_scr, n_scr, m_scr, L)

    @pl.when(last & context)
    def _():
        for d in range(2):
            for h in range(H):
                Co[d, h] = C_scr[d * H + h][...]
                no[d, h:h + 1, :] = n_scr[d, h]
                mo[d:d + 1, h:h + 1] = m_scr[d, h]


def _mlstm_scan(qkvo, gates, Bp, Sp, Bs, Ss, L, state):
    H, DK, DV = MLSTM_HEADS, MLSTM_DK, MLSTM_DV
    assert Sp % L == 0 and Ss % L == 0
    fwd, bwd, bidx, flags = [], [], [], []
    for context, B, S, base in ((1, Bp, Sp, 0), (0, Bs, Ss, Bp * Sp // L)):
        nc = S // L
        for b in range(B):
            for c in range(nc):
                fwd.append(base + b * nc + c)
                bwd.append(base + b * nc + nc - 1 - c)
                bidx.append(b)
                flags.append((c == 0) * 1 + (c == nc - 1) * 2 + context * 4)
    tables = [jnp.asarray(t, I32) for t in (fwd, bwd, bidx, flags)]
    lat_b = lambda s, fl, bi: jnp.where((fl[s] & 4) != 0, 0, bi[s])
    ctx_b = lambda s, fl, bi: jnp.where((fl[s] & 4) != 0, bi[s], Bp - 1)

    def specs(tbl):
        return [
            pl.BlockSpec((L, MLSTM_QK), lambda s, f, w, bi, fl: ((f, w)[tbl][s], 0)),
            pl.BlockSpec((L, MLSTM_QK), lambda s, f, w, bi, fl: ((f, w)[tbl][s], 1)),
            pl.BlockSpec((L, MLSTM_V), lambda s, f, w, bi, fl: ((f, w)[tbl][s], 1)),
            pl.BlockSpec((L, GATE_PAD), lambda s, f, w, bi, fl: ((f, w)[tbl][s], 0)),
        ]

    def state_specs(which):
        return [
            pl.BlockSpec((None, 2, H, DK, DV), lambda s, f, w, bi, fl: (which(s, fl, bi), 0, 0, 0, 0)),
            pl.BlockSpec((None, 2, H, DK), lambda s, f, w, bi, fl: (which(s, fl, bi), 0, 0, 0)),
            pl.BlockSpec((None, 2, H), lambda s, f, w, bi, fl: (which(s, fl, bi), 0, 0)),
        ]

    N = qkvo.shape[0]
    return pl.pallas_call(
        functools.partial(_mlstm_kernel, L=L),
        grid_spec=pltpu.PrefetchScalarGridSpec(
            num_scalar_prefetch=4,
            grid=(len(fwd),),
            in_specs=specs(0) + specs(1) + state_specs(lat_b),
            out_specs=[
                pl.BlockSpec((L, MLSTM_V), lambda s, f, w, bi, fl: (f[s], 0)),
                pl.BlockSpec((L, MLSTM_V), lambda s, f, w, bi, fl: (w[s], 0)),
            ] + state_specs(ctx_b),
            scratch_shapes=[pltpu.VMEM((DK, DV), F32)] * (2 * H) + [
                pltpu.VMEM((2, H, 1, DK), F32),
                pltpu.VMEM((2, H, 1, 1), F32),
            ],
        ),
        out_shape=[
            jax.ShapeDtypeStruct((N, MLSTM_V), BF16),
            jax.ShapeDtypeStruct((N, MLSTM_V), BF16),
            jax.ShapeDtypeStruct((Bp, 2, H, DK, DV), F32),
            jax.ShapeDtypeStruct((Bp, 2, H, DK), F32),
            jax.ShapeDtypeStruct((Bp, 2, H), F32),
        ],
        compiler_params=_cparams(("arbitrary",)),
        name="mlstm_scan",
    )(*tables, qkvo, qkvo, qkvo, gates, qkvo, qkvo, qkvo, gates, *state)


def _qk_prep_kernel(qkv_ref, qg_ref, kg_ref, cos_ref, sin_ref, *outs, emit_f32):
    if emit_f32:
        q_ref, k_ref, v_ref, kf_ref, vf_ref = outs
    else:
        q_ref, k_ref, v_ref = outs
    cos = cos_ref[...]
    sin = sin_ref[...]
    lane = lax.broadcasted_iota(I32, cos.shape, 1)
    first = (lane % (HEAD_DIM // 2)) < (HEAD_DIM // 4)

    def norm_rope(x, gain):
        xn = x * lax.rsqrt(jnp.mean(x * x, axis=-1, keepdims=True) + EPS) * gain
        partner = jnp.where(first, pltpu.roll(xn, HEAD_DIM - HEAD_DIM // 4, 1), pltpu.roll(xn, HEAD_DIM // 4, 1))
        return xn * cos + partner * sin

    for h in range(ATTN_Q_HEADS):
        sl = slice(h * HEAD_DIM, (h + 1) * HEAD_DIM)
        q_ref[:, sl] = (norm_rope(qkv_ref[:, sl], qg_ref[...]) * HEAD_DIM ** -0.5).astype(BF16)
    for h in range(ATTN_KV_HEADS):
        sl = slice(h * HEAD_DIM, (h + 1) * HEAD_DIM)
        kx = norm_rope(qkv_ref[:, ATTN_Q + h * HEAD_DIM:ATTN_Q + (h + 1) * HEAD_DIM], kg_ref[...])
        k_ref[:, sl] = kx.astype(BF16)
        if emit_f32:
            kf_ref[:, sl] = kx
    vx = qkv_ref[:, ATTN_Q + ATTN_KV:]
    v_ref[...] = vx.astype(BF16)
    if emit_f32:
        vf_ref[...] = vx


def _qk_prep(qkv, q_g, k_g, cos, sin, row0, rows, tm, emit_f32, name):
    assert rows % tm == 0 and row0 % tm == 0 and cos.shape[0] % tm == 0
    base = row0 // tm
    nt = cos.shape[0] // tm
    out_specs = [
        pl.BlockSpec((tm, ATTN_Q), lambda i: (i, 0)),
        pl.BlockSpec((tm, ATTN_KV), lambda i: (i, 0)),
        pl.BlockSpec((tm, ATTN_KV), lambda i: (i, 0)),
    ]
    out_shape = [
        jax.ShapeDtypeStruct((rows, ATTN_Q), BF16),
        jax.ShapeDtypeStruct((rows, ATTN_KV), BF16),
        jax.ShapeDtypeStruct((rows, ATTN_KV), BF16),
    ]
    if emit_f32:
        out_specs += [pl.BlockSpec((tm, ATTN_KV), lambda i: (i, 0))] * 2
        out_shape += [jax.ShapeDtypeStruct((rows, ATTN_KV), F32)] * 2
    return pl.pallas_call(
        functools.partial(_qk_prep_kernel, emit_f32=emit_f32),
        grid=(rows // tm,),
        in_specs=[
            pl.BlockSpec((tm, ATTN_PROJ), lambda i: (base + i, 0)),
            pl.BlockSpec((1, HEAD_DIM), lambda i: (0, 0)),
            pl.BlockSpec((1, HEAD_DIM), lambda i: (0, 0)),
            pl.BlockSpec((tm, HEAD_DIM), lambda i: (i % nt, 0)),
            pl.BlockSpec((tm, HEAD_DIM), lambda i: (i % nt, 0)),
        ],
        out_specs=out_specs,
        out_shape=out_shape,
        compiler_params=_cparams(("parallel",)),
        name=name,
    )(qkv, q_g.reshape(1, HEAD_DIM), k_g.reshape(1, HEAD_DIM), cos, sin)


def _sink_column(sink_ref, kv, rows_per_head):
    parts = [jnp.full((rows_per_head, 1), sink_ref[kv * ATTN_GROUPS + g], F32) for g in range(ATTN_GROUPS)]
    return jnp.concatenate(parts, axis=0)


def _attn_ctx_kernel(sink_ref, q_ref, k_ref, v_ref, o_ref):
    S = q_ref.shape[0]
    for kv in range(ATTN_KV_HEADS):
        q = jnp.concatenate(
            [q_ref[:, (kv * ATTN_GROUPS + g) * HEAD_DIM:(kv * ATTN_GROUPS + g + 1) * HEAD_DIM]
             for g in range(ATTN_GROUPS)], axis=0)
        ksl = slice(kv * HEAD_DIM, (kv + 1) * HEAD_DIM)
        s = _dot_nt(q, k_ref[:, ksl])
        sk = _sink_column(sink_ref, kv, S)
        m = jnp.maximum(jnp.max(s, axis=-1, keepdims=True), sk)
        p = jnp.exp(s - m)
        den = jnp.sum(p, axis=-1, keepdims=True) + jnp.exp(sk - m)
        o = _dot(p.astype(BF16), v_ref[:, ksl]) * (1.0 / den)
        for g in range(ATTN_GROUPS):
            hq = kv * ATTN_GROUPS + g
            o_ref[:, hq * HEAD_DIM:(hq + 1) * HEAD_DIM] = o[g * S:(g + 1) * S].astype(BF16)


def _attn_context(sink, q, k, v, B, S):
    return pl.pallas_call(
        _attn_ctx_kernel,
        grid_spec=pltpu.PrefetchScalarGridSpec(
            num_scalar_prefetch=0,
            grid=(B,),
            in_specs=[
                pl.BlockSpec(memory_space=pltpu.SMEM),
                pl.BlockSpec((S, ATTN_Q), lambda b: (b, 0)),
                pl.BlockSpec((S, ATTN_KV), lambda b: (b, 0)),
                pl.BlockSpec((S, ATTN_KV), lambda b: (b, 0)),
            ],
            out_specs=pl.BlockSpec((S, ATTN_Q), lambda b: (b, 0)),
        ),
        out_shape=jax.ShapeDtypeStruct((B * S, ATTN_Q), BF16),
        compiler_params=_cparams(("parallel",)),
        name="attn_context",
    )(sink, q, k, v)


def _attn_lat_kernel(sink_ref, q_ref, kp_ref, kc_ref, kn_ref, vp_ref, vc_ref, vn_ref, kx_ref, vx_ref, o_ref):
    i = pl.program_id(1)
    nb = pl.num_programs(1)
    R = ATTN_GROUPS * QBLK
    r = lax.broadcasted_iota(I32, (R, QBLK), 0) % QBLK
    c = lax.broadcasted_iota(I32, (R, QBLK), 1)
    ok_prev = c >= r
    ok_next = c <= r
    edge_prev = jnp.where(i > 0, 0.0, NEG_INF)
    edge_next = jnp.where(i < nb - 1, 0.0, NEG_INF)
    for kv in range(ATTN_KV_HEADS):
        q = jnp.concatenate(
            [q_ref[:, (kv * ATTN_GROUPS + g) * HEAD_DIM:(kv * ATTN_GROUPS + g + 1) * HEAD_DIM]
             for g in range(ATTN_GROUPS)], axis=0)
        ksl = slice(kv * HEAD_DIM, (kv + 1) * HEAD_DIM)
        s_p = jnp.where(ok_prev, _dot_nt(q, kp_ref[:, ksl]) + edge_prev, NEG_INF)
        s_c = _dot_nt(q, kc_ref[:, ksl])
        s_n = jnp.where(ok_next, _dot_nt(q, kn_ref[:, ksl]) + edge_next, NEG_INF)
        s_x = _dot_nt(q, kx_ref[:, ksl])
        sk = _sink_column(sink_ref, kv, QBLK)
        m = jnp.maximum(
            jnp.maximum(jnp.max(s_p, axis=-1, keepdims=True), jnp.max(s_c, axis=-1, keepdims=True)),
            jnp.maximum(jnp.max(s_n, axis=-1, keepdims=True), jnp.max(s_x, axis=-1, keepdims=True)))
        m = jnp.maximum(m, sk)
        p_p, p_c, p_n, p_x = jnp.exp(s_p - m), jnp.exp(s_c - m), jnp.exp(s_n - m), jnp.exp(s_x - m)
        den = (jnp.sum(p_p, axis=-1, keepdims=True) + jnp.sum(p_c, axis=-1, keepdims=True)
               + jnp.sum(p_n, axis=-1, keepdims=True) + jnp.sum(p_x, axis=-1, keepdims=True) + jnp.exp(sk - m))
        o = (_dot(p_p.astype(BF16), vp_ref[:, ksl]) + _dot(p_c.astype(BF16), vc_ref[:, ksl])
             + _dot(p_n.astype(BF16), vn_ref[:, ksl]) + _dot(p_x.astype(BF16), vx_ref[:, ksl])) * (1.0 / den)
        for g in range(ATTN_GROUPS):
            hq = kv * ATTN_GROUPS + g
            o_ref[:, hq * HEAD_DIM:(hq + 1) * HEAD_DIM] = o[g * QBLK:(g + 1) * QBLK].astype(BF16)


def _attn_latent(sink, q, k, v, ctx_k, ctx_v, B, S):
    nb = S // QBLK
    P = ctx_k.shape[1]
    prev = lambda b, i: (b * nb + jnp.maximum(i - 1, 0), 0)
    cur = lambda b, i: (b * nb + i, 0)
    nxt = lambda b, i: (b * nb + jnp.minimum(i + 1, nb - 1), 0)
    kv_spec = lambda f: pl.BlockSpec((QBLK, ATTN_KV), f)
    return pl.pallas_call(
        _attn_lat_kernel,
        grid_spec=pltpu.PrefetchScalarGridSpec(
            num_scalar_prefetch=0,
            grid=(B, nb),
            in_specs=[
                pl.BlockSpec(memory_space=pltpu.SMEM),
                pl.BlockSpec((QBLK, ATTN_Q), cur),
                kv_spec(prev), kv_spec(cur), kv_spec(nxt),
                kv_spec(prev), kv_spec(cur), kv_spec(nxt),
                pl.BlockSpec((None, P, ATTN_KV), lambda b, i: (b, 0, 0)),
                pl.BlockSpec((None, P, ATTN_KV), lambda b, i: (b, 0, 0)),
            ],
            out_specs=pl.BlockSpec((QBLK, ATTN_Q), cur),
        ),
        out_shape=jax.ShapeDtypeStruct((B * S, ATTN_Q), BF16),
        compiler_params=_cparams(("parallel", "parallel")),
        name="attn_latent",
    )(sink, q, k, k, k, v, v, v, ctx_k, ctx_v)


SLAB = 8
U32 = jnp.uint32


def _pack_rows(x):
    half = x.shape[1] // 2
    bits = lax.bitcast_convert_type(x.astype(BF16).astype(F32), U32)
    return (bits[:, :half] >> 16) | (bits[:, half:] & jnp.uint32(0xFFFF0000))


def _store_slabs(ref, words):
    R = words.shape[0]
    for j in range(SLAB):
        ref[pl.ds(j, R, stride=SLAB), :] = words[:, j * 128:(j + 1) * 128]


def _load_slab_halves(ref, R, j):
    w = ref[pl.ds(j, R, stride=SLAB), :]
    return lax.bitcast_convert_type(w << 16, F32), lax.bitcast_convert_type(w & jnp.uint32(0xFFFF0000), F32)


def _load_rows_bf16(ref, R):
    halves = [_load_slab_halves(ref, R, j) for j in range(SLAB)]
    return jnp.concatenate([lo for lo, _ in halves] + [hi for _, hi in halves], axis=1).astype(BF16)


def _slab_copy(src, dst, sem, src_row8, dst_row):
    return pltpu.make_async_copy(src.at[pl.ds(pl.multiple_of(src_row8, SLAB), SLAB), :],
                                 dst.at[pl.ds(dst_row * SLAB, SLAB), :], sem)


def _router_kernel(x_ref, gain_ref, mod_ref, wr_ref, rb_ref, h_ref, e_ref, g_ref, r_ref, cnt_ref, cnt_scr, *, tm):
    E, GS = N_EXPERTS, GROUP_SIZE
    i = pl.program_id(0)

    @pl.when(i == 0)
    def _():
        cnt_scr[...] = jnp.zeros_like(cnt_scr)

    h = _norm_modulate(x_ref[...], gain_ref[...], mod_ref, 3, 4)
    _store_slabs(h_ref, _pack_rows(h))
    h_hi, h_lo = _split_hi_lo(h)
    w_hi, w_lo = _split_hi_lo(wr_ref[...])
    logits = _dot_nt(w_hi, h_hi) + _dot_nt(w_lo, h_hi) + _dot_nt(w_hi, h_lo)
    scores = jax.nn.sigmoid(logits)
    biased = scores + rb_ref[...]
    sub = lax.broadcasted_iota(I32, (GS, tm), 0)
    gscore = []
    for gi in range(N_EXPERT_GROUPS):
        xg = biased[gi * GS:(gi + 1) * GS, :]
        m1 = jnp.max(xg, axis=0, keepdims=True)
        first = jnp.min(jnp.where(xg == m1, sub, GS), axis=0, keepdims=True)
        m2 = jnp.max(jnp.where(sub == first, NEG_INF, xg), axis=0, keepdims=True)
        gscore.append(m1 + m2)
    cur = jnp.concatenate(gscore, axis=0)
    gid = lax.broadcasted_iota(I32, (N_EXPERT_GROUPS, tm), 0)
    gsel = jnp.zeros((N_EXPERT_GROUPS, tm), F32)
    for _ in range(TOPK_GROUPS):
        mx = jnp.max(cur, axis=0, keepdims=True)
        first = jnp.min(jnp.where(cur == mx, gid, N_EXPERT_GROUPS), axis=0, keepdims=True)
        hit = gid == first
        gsel = jnp.where(hit, 1.0, gsel)
        cur = jnp.where(hit, NEG_INF, cur)
    ok = jnp.concatenate(
        [jnp.broadcast_to(gsel[gi:gi + 1, :], (GS, tm)) for gi in range(N_EXPERT_GROUPS)], axis=0)
    masked = jnp.where(ok > 0.5, biased, NEG_INF)
    eid = lax.broadcasted_iota(I32, (E, tm), 0)
    sel = jnp.zeros((E, tm), F32)
    picks, pick_scores = [], []
    for _ in range(TOP_K):
        mx = jnp.max(masked, axis=0, keepdims=True)
        first = jnp.min(jnp.where(masked == mx, eid, E), axis=0, keepdims=True)
        hit = eid == first
        picks.append(first)
        pick_scores.append(jnp.sum(jnp.where(hit, scores, 0.0), axis=0, keepdims=True))
        sel = jnp.where(hit, 1.0, sel)
        masked = jnp.where(hit, NEG_INF, masked)
    total = pick_scores[0]
    for s in pick_scores[1:]:
        total = total + s
    g_ref[...] = jnp.concatenate(pick_scores, axis=0) / total * ROUTED_SCALE
    e_ref[...] = jnp.concatenate(picks, axis=0)
    rr = lax.broadcasted_iota(I32, (tm, tm), 0)
    cc = lax.broadcasted_iota(I32, (tm, tm), 1)
    before = jnp.where(rr < cc, 1.0, 0.0).astype(BF16)
    rank = _dot(sel.astype(BF16), before) + cnt_scr[...]
    r_ref[...] = jnp.concatenate(
        [jnp.sum(jnp.where(eid == p, rank, 0.0), axis=0, keepdims=True) for p in picks], axis=0).astype(I32)
    cnt_scr[...] = cnt_scr[...] + jnp.sum(sel, axis=-1, keepdims=True)
    cnt_ref[...] = cnt_scr[...].astype(I32)


def _router(x, gain, mod, router_w_t, router_b, *, n_prompt, dec_seq, tm):
    N, D = x.shape
    E = N_EXPERTS
    assert D == 2 * SLAB * 128
    assert N % tm == 0 and n_prompt % tm == 0 and dec_seq % tm == 0
    row = functools.partial(_mod_row, tm=tm, n_prompt=n_prompt, dec_seq=dec_seq)
    return pl.pallas_call(
        functools.partial(_router_kernel, tm=tm),
        grid=(N // tm,),
        in_specs=[
            pl.BlockSpec((tm, D), lambda i: (i, 0)),
            pl.BlockSpec((1, D), lambda i: (0, 0)),
            pl.BlockSpec((None, 1, 6 * D), lambda i: (row(i), 0, 0)),
            pl.BlockSpec((E, D), lambda i: (0, 0)),
            pl.BlockSpec((E, 1), lambda i: (0, 0)),
        ],
        out_specs=[
            pl.BlockSpec((tm * SLAB, 128), lambda i: (i, 0)),
            pl.BlockSpec((TOP_K, tm), lambda i: (0, i)),
            pl.BlockSpec((TOP_K, tm), lambda i: (0, i)),
            pl.BlockSpec((TOP_K, tm), lambda i: (0, i)),
            pl.BlockSpec((E, 1), lambda i: (0, 0)),
        ],
        out_shape=[
            jax.ShapeDtypeStruct((N * SLAB, 128), U32),
            jax.ShapeDtypeStruct((TOP_K, N), I32),
            jax.ShapeDtypeStruct((TOP_K, N), F32),
            jax.ShapeDtypeStruct((TOP_K, N), I32),
            jax.ShapeDtypeStruct((E, 1), I32),
        ],
        scratch_shapes=[pltpu.VMEM((E, 1), F32)],
        compiler_params=_cparams(("arbitrary",)),
        name="moe_router",
    )(x, gain.reshape(1, D), mod, router_w_t, router_b.reshape(E, 1))


def _experts_kernel(be_ref, nused_ref, tok_cur, tok_nxt, h_hbm, w1_ref, w3_ref, w2_ref, y_ref,
                    xbuf, sems, w1c, w3c, w2c, prev_e):
    b = pl.program_id(0)
    e = be_ref[b]
    n_used = nused_ref[0]
    BLK = xbuf.shape[1] // SLAB
    slot = b % 2

    FF = w1c.shape[1]
    D = w2c.shape[1]
    CW = 256
    n_chunks = 2 * (FF // CW) + D // CW
    per_chunk = BLK // n_chunks + 1

    def gather(tok_ref, s, lo=0, hi=BLK):
        for r in range(lo, min(hi, BLK)):
            _slab_copy(h_hbm, xbuf.at[s], sems.at[s], tok_ref[0, r], r).start()

    def drain(s):
        for r in range(BLK):
            _slab_copy(h_hbm, xbuf.at[s], sems.at[s], 0, r).wait()

    @pl.when(b == 0)
    def _():
        prev_e[0] = -1
        gather(tok_cur, 0)

    @pl.when(b < n_used)
    def _():
        @pl.when(e != prev_e[0])
        def _():
            w1c[...] = w1_ref[...].astype(BF16)
            w3c[...] = w3_ref[...].astype(BF16)
            w2c[...] = w2_ref[...].astype(BF16)
            prev_e[0] = e

        drain(slot)
        chunk = [0]

        def gather_some():
            gather(tok_nxt, 1 - slot, chunk[0] * per_chunk, (chunk[0] + 1) * per_chunk)
            chunk[0] += 1

        x = _load_rows_bf16(xbuf.at[slot], BLK)
        mids = []
        for c in range(FF // CW):
            sl = slice(c * CW, (c + 1) * CW)
            gather_some()
            a = _dot(x, w1c[:, sl])
            gather_some()
            mids.append((_silu(a) * _dot(x, w3c[:, sl])).astype(BF16))
        mid = jnp.concatenate(mids, axis=1)
        half = D // 2
        for c in range(half // CW):
            gather_some()
            y_lo = _dot(mid, w2c[:, c * CW:(c + 1) * CW])
            gather_some()
            y_hi = _dot(mid, w2c[:, half + c * CW:half + (c + 1) * CW])
            words = _pack_rows(jnp.concatenate([y_lo, y_hi], axis=1))
            for jj in range(CW // 128):
                j = c * (CW // 128) + jj
                y_ref[pl.ds(j, BLK, stride=SLAB), :] = words[:, jj * 128:(jj + 1) * 128]

    @pl.when(b >= n_used)
    def _():
        @pl.when(b == n_used)
        def _():
            drain(slot)

        y_ref[...] = jnp.zeros_like(y_ref)


def _experts(block_expert, n_used, slot_tok8, h, w1, w3, w2, layer):
    n_blocks = block_expert.shape[0]
    BLK = MOE_BLOCK
    _, _, D, FF = w1.shape
    slot_tok3 = slot_tok8.reshape(n_blocks, 1, BLK)
    return pl.pallas_call(
        _experts_kernel,
        grid_spec=pltpu.PrefetchScalarGridSpec(
            num_scalar_prefetch=2,
            grid=(n_blocks,),
            in_specs=[
                pl.BlockSpec((None, 1, BLK), lambda b, be, nu: (b, 0, 0), memory_space=pltpu.SMEM),
                pl.BlockSpec((None, 1, BLK), lambda b, be, nu: (jnp.minimum(b + 1, n_blocks - 1), 0, 0),
                             memory_space=pltpu.SMEM),
                pl.BlockSpec(memory_space=pl.ANY),
                pl.BlockSpec((None, None, D, FF), lambda b, be, nu: (layer, be[b], 0, 0)),
                pl.BlockSpec((None, None, D, FF), lambda b, be, nu: (layer, be[b], 0, 0)),
                pl.BlockSpec((None, None, FF, D), lambda b, be, nu: (layer, be[b], 0, 0)),
            ],
            out_specs=pl.BlockSpec((BLK * SLAB, 128), lambda b, be, nu: (b, 0)),
            scratch_shapes=[
                pltpu.VMEM((2, BLK * SLAB, 128), U32),
                pltpu.SemaphoreType.DMA((2,)),
                pltpu.VMEM((D, FF), BF16),
                pltpu.VMEM((D, FF), BF16),
                pltpu.VMEM((FF, D), BF16),
                pltpu.SMEM((1,), I32),
            ],
        ),
        out_shape=jax.ShapeDtypeStruct((n_blocks * BLK * SLAB, 128), U32),
        compiler_params=_cparams(("arbitrary",)),
        name="moe_experts",
    )(block_expert, n_used, slot_tok3, slot_tok3, h, w1, w3, w2)


def _combine_kernel(dest_cur, dest_nxt, gates_ref, y_hbm, h_ref, x_ref, gate_ref, ws1_ref, ws3_ref, ws2_ref,
                    *rest, prompt_tiles):
    if prompt_tiles is None:
        o_ref, ybuf0, ybuf1, sems = rest
    else:
        op_ref, os_ref, ybuf0, ybuf1, sems = rest
    tm = x_ref.shape[0]
    i = pl.program_id(0)
    last = pl.num_programs(0) - 1

    def gather(dest_ref, buf, sem, lo=0, hi=tm):
        for k in range(TOP_K):
            for r in range(lo, hi):
                _slab_copy(y_hbm, buf.at[k], sem, dest_ref[k, r], r).start()

    def drain(buf, sem):
        for k in range(TOP_K):
            for r in range(tm):
                _slab_copy(y_hbm, buf.at[k], sem, 0, r).wait()

    @pl.when(i == 0)
    def _():
        gather(dest_cur, ybuf0, sems.at[0])

    def step(cur, cur_sem, nxt, nxt_sem):
        drain(cur, cur_sem)
        per_group = tm // (SLAB + 2)
        gather(dest_nxt, nxt, nxt_sem, 0, per_group)
        hb = _load_rows_bf16(h_ref, tm)
        mid = (_silu(_dot(hb, ws1_ref[...])) * _dot(hb, ws3_ref[...])).astype(BF16)
        gather(dest_nxt, nxt, nxt_sem, per_group, 2 * per_group)
        shared = _dot(mid, ws2_ref[...])
        g = [gates_ref[:, k:k + 1] for k in range(TOP_K)]
        lo_parts, hi_parts = [], []
        for j in range(SLAB):
            gather(dest_nxt, nxt, nxt_sem, (j + 2) * per_group, tm if j == SLAB - 1 else (j + 3) * per_group)
            lo_acc = hi_acc = None
            for k in range(TOP_K):
                lo, hi = _load_slab_halves(cur.at[k], tm, j)
                lo_acc = g[k] * lo if lo_acc is None else lo_acc + g[k] * lo
                hi_acc = g[k] * hi if hi_acc is None else hi_acc + g[k] * hi
            lo_parts.append(lo_acc)
            hi_parts.append(hi_acc)
        routed = jnp.concatenate(lo_parts + hi_parts, axis=1)
        out = x_ref[...] + gate_ref[...] * (routed + shared)
        if prompt_tiles is None:
            o_ref[...] = out
        else:
            @pl.when(i < prompt_tiles)
            def _():
                op_ref[...] = out

            @pl.when(i >= prompt_tiles)
            def _():
                os_ref[...] = out

        @pl.when(i == last)
        def _():
            drain(nxt, nxt_sem)

    @pl.when(i % 2 == 0)
    def _():
        step(ybuf0, sems.at[0], ybuf1, sems.at[1])

    @pl.when(i % 2 == 1)
    def _():
        step(ybuf1, sems.at[1], ybuf0, sems.at[0])


def _combine(dest, gates, y, h, x, mod, ws1, ws3, ws2, *, n_prompt, dec_seq, split_outputs):
    N, D = x.shape
    tm = COMBINE_TM
    FF = ws1.shape[1]
    assert N % tm == 0 and n_prompt % tm == 0 and dec_seq % tm == 0
    nt = N // tm
    row = functools.partial(_mod_row, tm=tm, n_prompt=n_prompt, dec_seq=dec_seq)
    dest3 = dest.reshape(TOP_K, nt, tm).transpose(1, 0, 2)
    if split_outputs:
        pt = n_prompt // tm
        out_specs = [pl.BlockSpec((tm, D), lambda i: (jnp.minimum(i, pt - 1), 0)),
                     pl.BlockSpec((tm, D), lambda i: (jnp.maximum(i - pt, 0), 0))]
        out_shape = [jax.ShapeDtypeStruct((n_prompt, D), F32), jax.ShapeDtypeStruct((N - n_prompt, D), F32)]
    else:
        pt = None
        out_specs = pl.BlockSpec((tm, D), lambda i: (i, 0))
        out_shape = jax.ShapeDtypeStruct((N, D), F32)
    return pl.pallas_call(
        functools.partial(_combine_kernel, prompt_tiles=pt),
        grid=(nt,),
        in_specs=[
            pl.BlockSpec((None, TOP_K, tm), lambda i: (i, 0, 0), memory_space=pltpu.SMEM),
            pl.BlockSpec((None, TOP_K, tm), lambda i: (jnp.minimum(i + 1, nt - 1), 0, 0), memory_space=pltpu.SMEM),
            pl.BlockSpec((tm, TOP_K), lambda i: (i, 0)),
            pl.BlockSpec(memory_space=pl.ANY),
            pl.BlockSpec((tm * SLAB, 128), lambda i: (i, 0)),
            pl.BlockSpec((tm, D), lambda i: (i, 0)),
            pl.BlockSpec((None, 1, D), lambda i: (row(i), 0, 5)),
            pl.BlockSpec((D, FF), lambda i: (0, 0)),
            pl.BlockSpec((D, FF), lambda i: (0, 0)),
            pl.BlockSpec((FF, D), lambda i: (0, 0)),
        ],
        out_specs=out_specs,
        out_shape=out_shape,
        scratch_shapes=[pltpu.VMEM((TOP_K, tm * SLAB, 128), U32), pltpu.VMEM((TOP_K, tm * SLAB, 128), U32),
                        pltpu.SemaphoreType.DMA((2,))],
        compiler_params=_cparams(("arbitrary",)),
        name="moe_combine",
    )(dest3, dest3, gates, y, h, x, mod, ws1, ws3, ws2)


def _moe_layer(x, gain, mod, router_w, router_b, w1, w3, w2, layer, ws1, ws3, ws2, *, n_prompt, dec_seq,
               split_outputs=False):
    N = x.shape[0]
    E, BLK = N_EXPERTS, MOE_BLOCK
    h, top_e, gates, rank, counts = _router(x, gain, mod, router_w.T, router_b,
                                            n_prompt=n_prompt, dec_seq=dec_seq,
                                            tm=_row_tile(512, n_prompt, dec_seq))
    counts = counts[:, 0]
    padded = (counts + BLK - 1) // BLK * BLK
    pad_end = jnp.cumsum(padded)
    pad_start = pad_end - padded
    n_blocks = N * TOP_K // BLK + E
    block_expert = jnp.minimum(
        jnp.sum(pad_end[None, :] <= (jnp.arange(n_blocks, dtype=I32) * BLK)[:, None], axis=1), E - 1).astype(I32)
    n_used = (pad_end[-1:] // BLK).astype(I32)
    onehot = top_e[None] == jnp.arange(E, dtype=I32)[:, None, None]
    dest = jnp.sum(jnp.where(onehot, pad_start[:, None, None], 0), axis=0) + rank
    NP = N + BLK
    keys = (top_e * NP + jnp.arange(N, dtype=I32)[None, :]).reshape(-1)
    fill_i = jnp.arange(BLK, dtype=I32)[None, :]
    fill_e = jnp.arange(E, dtype=I32)[:, None]
    fillers = jnp.where(fill_i < (padded - counts)[:, None], fill_e * NP + N + fill_i, E * NP + fill_e * BLK + fill_i)
    slot_t = lax.sort(jnp.concatenate([keys, fillers.reshape(-1)])) % NP
    spread = jnp.arange(n_blocks * BLK, dtype=I32) % N
    slot_tok8 = jnp.where(slot_t < N, slot_t, spread) * SLAB
    y = _experts(block_expert, n_used, slot_tok8, h, w1, w3, w2, layer)
    return _combine(dest * SLAB, gates.T, y, h, x, mod, ws1, ws3, ws2, n_prompt=n_prompt, dec_seq=dec_seq,
                    split_outputs=split_outputs)


def _rope_tables(S):
    quarter = HEAD_DIM // 4
    pos = jnp.arange(S)
    row_id = (pos // GRID_W).astype(F32)
    col_id = (pos % GRID_W).astype(F32)
    inv = ROPE_THETA ** (-jnp.arange(quarter, dtype=F32) / quarter)
    ar, ac = row_id[:, None] * inv, col_id[:, None] * inv
    cos = jnp.concatenate([jnp.cos(ar), jnp.cos(ar), jnp.cos(ac), jnp.cos(ac)], axis=-1)
    sin = jnp.concatenate([-jnp.sin(ar), jnp.sin(ar), -jnp.sin(ac), jnp.sin(ac)], axis=-1)
    return cos, sin


def kernel(x_prompt, x_sample, state_mlstm_C, state_mlstm_n, state_mlstm_m, cache_attn_k, cache_attn_v,
           c, c_ctx, ada_w, ada_b, norm_mix, norm_ffn,
           mlstm_w_in, mlstm_gate_b, mlstm_head_g, mlstm_w_out,
           attn_w_qkv, attn_q_g, attn_k_g, attn_sink, attn_w_o,
           moe_router_w, moe_router_b, moe_w1, moe_w3, moe_w2, shared_w1, shared_w3, shared_w2):
    D = D_MODEL
    Bp, Sp, _ = x_prompt.shape
    Bs, Ss, _ = x_sample.shape
    n_prompt = Bp * Sp
    N = n_prompt + Bs * Ss
    dims = dict(n_prompt=n_prompt, dec_seq=Ss)
    tm512 = _row_tile(512, n_prompt, Ss)

    x_parts = (x_prompt.reshape(n_prompt, D), x_sample.reshape(Bs * Ss, D))
    rows = 16
    cvec = jnp.zeros((rows, D), F32).at[0].set(c_ctx).at[1:1 + Bs].set(c)
    mod_all = _ada_table(cvec, ada_w, ada_b)[:, :1 + Bs].reshape(DEPTH, 1 + Bs, 1, 6 * D)

    H = MLSTM_HEADS
    mod = mod_all[0]
    w_in = mlstm_w_in[0]
    w_main = w_in[:, :MLSTM_MAIN].astype(BF16)
    w_gate = jnp.pad(w_in[:, MLSTM_MAIN:], ((0, 0), (0, GATE_PAD - 4 * H))).astype(BF16)
    gate_b = jnp.pad(mlstm_gate_b[0], (0, GATE_PAD - 4 * H))
    col_scale = jnp.concatenate([jnp.full((MLSTM_QK,), MLSTM_DK ** -0.5, F32),
                                 jnp.ones((MLSTM_MAIN - MLSTM_QK,), F32)])
    qkvo, gates, x = _nm_matmul(x_parts, norm_mix[0], mod, w_main, col_scale, jnp.zeros((MLSTM_MAIN,), F32),
                                shift_idx=0, scale_idx=1, out_dtype=BF16, tm=tm512, tn=1024,
                                aux_w=w_gate, aux_b=gate_b, name="mlstm_proj", **dims)
    L = 256
    state = (state_mlstm_C[:, 0], state_mlstm_n[:, 0], state_mlstm_m[:, 0])
    hf, hb, C_p, n_p, m_p = _mlstm_scan(qkvo, gates, Bp, Sp, Bs, Ss, L, state)
    x = _mm_residual((hf, hb, qkvo, mlstm_head_g[0]), mlstm_w_out[0].astype(BF16), x, mod, gate_idx=2,
                     tm=_row_tile(256, n_prompt, Ss), tn=D, mlstm_prologue=True, name="mlstm_out", **dims)
    x = _moe_layer(x, norm_ffn[0], mod, moe_router_w[0], moe_router_b[0], moe_w1, moe_w3, moe_w2, 0,
                   shared_w1[0].astype(BF16), shared_w3[0].astype(BF16), shared_w2[0].astype(BF16), **dims)

    mod = mod_all[1]
    (qkv,) = _nm_matmul(x, norm_mix[1], mod, attn_w_qkv[0].astype(BF16), jnp.ones((ATTN_PROJ,), F32),
                        jnp.zeros((ATTN_PROJ,), F32), shift_idx=0, scale_idx=1, out_dtype=F32,
                        tm=_row_tile(1024, n_prompt, Ss), tn=1024,
                        name="attn_qkv", **dims)
    ident_cos = jnp.ones((Sp, HEAD_DIM), F32)
    ident_sin = jnp.zeros((Sp, HEAD_DIM), F32)
    q_p, k_p, v_p, kf_p, vf_p = _qk_prep(qkv, attn_q_g[0], attn_k_g[0], ident_cos, ident_sin, 0, n_prompt, Sp,
                                         True, "qk_prep_prompt")
    cos, sin = _rope_tables(Ss)
    q_s, k_s, v_s = _qk_prep(qkv, attn_q_g[0], attn_k_g[0], cos, sin, n_prompt, Bs * Ss, 256, False,
                             "qk_prep_latent")
    o_p = _attn_context(attn_sink[0], q_p, k_p, v_p, Bp, Sp)
    P = cache_attn_k.shape[2]
    ctx_k = cache_attn_k[:, 0].reshape(Bs, P, ATTN_KV).astype(BF16)
    ctx_v = cache_attn_v[:, 0].reshape(Bs, P, ATTN_KV).astype(BF16)
    o_s = _attn_latent(attn_sink[0], q_s, k_s, v_s, ctx_k, ctx_v, Bs, Ss)
    x = _mm_residual((o_p, o_s), attn_w_o[0].astype(BF16), x, mod, gate_idx=2,
                     tm=tm512, tn=D,
                     mlstm_prologue=False, name="attn_out", **dims)
    y_prompt, y_sample = _moe_layer(
        x, norm_ffn[1], mod, moe_router_w[1], moe_router_b[1], moe_w1, moe_w3, moe_w2, 1,
        shared_w1[1].astype(BF16), shared_w3[1].astype(BF16), shared_w2[1].astype(BF16), split_outputs=True, **dims)
    return (y_prompt.reshape(Bp, Sp, D), y_sample.reshape(Bs, Ss, D), C_p[:, None], n_p[:, None], m_p[:, None],
            kf_p.reshape(Bp, 1, Sp, ATTN_KV_HEADS, HEAD_DIM), vf_p.reshape(Bp, 1, Sp, ATTN_KV_HEADS, HEAD_DIM))
```

```python
import functools

import jax
import jax.numpy as jnp
from jax import lax
from jax.experimental import pallas as pl
from jax.experimental.pallas import tpu as pltpu

F32 = jnp.float32
BF16 = jnp.bfloat16
I32 = jnp.int32

D_MODEL = 2048
DEPTH = 2
EPS = 1e-6
GRID_W = 64
MLSTM_HEADS = 8
MLSTM_DK = 128
MLSTM_DV = 256
MLSTM_QK = MLSTM_HEADS * MLSTM_DK
MLSTM_V = MLSTM_HEADS * MLSTM_DV
MLSTM_MAIN = 2 * MLSTM_QK + 2 * MLSTM_V
GATE_PAD = 128
HEAD_DIM = 128
ATTN_Q_HEADS = 16
ATTN_KV_HEADS = 4
ATTN_GROUPS = 4
WINDOW = 128
QBLK = 128
ROPE_THETA = 10000.0
ATTN_Q = ATTN_Q_HEADS * HEAD_DIM
ATTN_KV = ATTN_KV_HEADS * HEAD_DIM
ATTN_PROJ = ATTN_Q + 2 * ATTN_KV
N_EXPERTS = 64
TOP_K = 8
N_EXPERT_GROUPS = 8
TOPK_GROUPS = 4
GROUP_SIZE = N_EXPERTS // N_EXPERT_GROUPS
EXPERT_FF = 512
ROUTED_SCALE = 2.5
MOE_BLOCK = 512
COMBINE_TM = 128

V7X_VMEM_LIMIT = 56 * 1024 * 1024
NEG_INF = float("-inf")


def _cparams(sem):
    return pltpu.CompilerParams(dimension_semantics=("arbitrary",) * len(sem), vmem_limit_bytes=V7X_VMEM_LIMIT)


def _split_hi_lo(a):
    hi = a.astype(BF16)
    lo = (a - hi.astype(F32)).astype(BF16)
    return hi, lo


def _dot(a, b):
    return jnp.dot(a, b, preferred_element_type=F32)


def _dot_nt(a, b):
    return lax.dot_general(a, b, (((1,), (1,)), ((), ())), preferred_element_type=F32)


def _dot_tn(a, b):
    return lax.dot_general(a, b, (((0,), (0,)), ((), ())), preferred_element_type=F32)


def _silu(x):
    return x * jax.nn.sigmoid(x)


def _row_tile(preferred, n_prompt, dec_seq):
    tm = preferred
    while n_prompt % tm or dec_seq % tm:
        tm //= 2
    return tm


def _mod_row(i, tm, n_prompt, dec_seq):
    r0 = i * tm
    return jnp.where(r0 < n_prompt, 0, 1 + (r0 - n_prompt) // dec_seq)


def _norm_modulate(x, gain, mod_ref, shift_idx, scale_idx):
    D = D_MODEL
    y = x * lax.rsqrt(jnp.mean(x * x, axis=-1, keepdims=True) + EPS) * gain
    shift = mod_ref[:, shift_idx * D:(shift_idx + 1) * D]
    scale = mod_ref[:, scale_idx * D:(scale_idx + 1) * D]
    return y * (1.0 + scale) + shift


def _ada_kernel(c_ref, w_ref, b_ref, o_ref):
    s = _silu(c_ref[...])
    s_hi, s_lo = _split_hi_lo(s)
    w_hi, w_lo = _split_hi_lo(w_ref[...])
    o_ref[...] = _dot(s_hi, w_hi) + _dot(s_hi, w_lo) + _dot(s_lo, w_hi) + b_ref[...]


def _ada_table(cvec, ada_w, ada_b):
    D = D_MODEL
    tn = 1024
    rows = cvec.shape[0]
    return pl.pallas_call(
        _ada_kernel,
        grid=(DEPTH, 6 * D // tn),
        in_specs=[
            pl.BlockSpec((rows, D), lambda l, j: (0, 0)),
            pl.BlockSpec((None, D, tn), lambda l, j: (l, 0, j)),
            pl.BlockSpec((None, 1, tn), lambda l, j: (l, 0, j)),
        ],
        out_specs=pl.BlockSpec((None, rows, tn), lambda l, j: (l, 0, j)),
        out_shape=jax.ShapeDtypeStruct((DEPTH, rows, 6 * D), F32),
        compiler_params=_cparams(("parallel", "parallel")),
        name="ada_table",
    )(cvec, ada_w, ada_b.reshape(DEPTH, 1, 6 * D))


def _nm_matmul_kernel(*refs, shift_idx, scale_idx, has_aux, prompt_tiles):
    if prompt_tiles is None:
        x_ref, gain_ref, mod_ref, w_ref, cs_ref, cb_ref, *rest = refs
    else:
        xp_ref, xs_ref, gain_ref, mod_ref, w_ref, cs_ref, cb_ref, *rest = refs
    if has_aux:
        wa_ref, ab_ref, o_ref, aux_ref, *tail = rest
    else:
        o_ref, *tail = rest
    if prompt_tiles is None:
        (h_scr,) = tail
    else:
        xcat_ref, h_scr = tail

    def prologue(x):
        h = _norm_modulate(x, gain_ref[...], mod_ref, shift_idx, scale_idx).astype(BF16)
        h_scr[...] = h
        if has_aux:
            aux_ref[...] = _dot(h, wa_ref[...]) + ab_ref[...]
        if prompt_tiles is not None:
            xcat_ref[...] = x

    first_col = pl.program_id(1) == 0
    if prompt_tiles is None:
        @pl.when(first_col)
        def _():
            prologue(x_ref[...])
    else:
        context = pl.program_id(0) < prompt_tiles

        @pl.when(first_col & context)
        def _():
            prologue(xp_ref[...])

        @pl.when(first_col & jnp.logical_not(context))
        def _():
            prologue(xs_ref[...])

    acc = _dot(h_scr[...], w_ref[...])
    o_ref[...] = (acc * cs_ref[...] + cb_ref[...]).astype(o_ref.dtype)


def _nm_matmul(x, gain, mod, w, col_scale, col_bias, *, shift_idx, scale_idx, n_prompt, dec_seq,
               out_dtype, tm, tn, aux_w=None, aux_b=None, name):
    split = isinstance(x, tuple)
    D = w.shape[0]
    N = sum(p.shape[0] for p in x) if split else x.shape[0]
    P = w.shape[1]
    assert N % tm == 0 and P % tn == 0 and n_prompt % tm == 0 and dec_seq % tm == 0
    has_aux = aux_w is not None
    row = functools.partial(_mod_row, tm=tm, n_prompt=n_prompt, dec_seq=dec_seq)
    pt = n_prompt // tm
    if split:
        x_specs = [pl.BlockSpec((tm, D), lambda i, j: (jnp.minimum(i, pt - 1), 0)),
                   pl.BlockSpec((tm, D), lambda i, j: (jnp.maximum(i - pt, 0), 0))]
        x_args = list(x)
    else:
        x_specs = [pl.BlockSpec((tm, D), lambda i, j: (i, 0))]
        x_args = [x]
    in_specs = x_specs + [
        pl.BlockSpec((1, D), lambda i, j: (0, 0)),
        pl.BlockSpec((None, 1, 6 * D), lambda i, j: (row(i), 0, 0)),
        pl.BlockSpec((D, tn), lambda i, j: (0, j)),
        pl.BlockSpec((1, tn), lambda i, j: (0, j)),
        pl.BlockSpec((1, tn), lambda i, j: (0, j)),
    ]
    args = x_args + [gain.reshape(1, D), mod, w, col_scale.reshape(1, P), col_bias.reshape(1, P)]
    out_specs = [pl.BlockSpec((tm, tn), lambda i, j: (i, j))]
    out_shape = [jax.ShapeDtypeStruct((N, P), out_dtype)]
    if has_aux:
        PA = aux_w.shape[1]
        in_specs += [pl.BlockSpec((D, PA), lambda i, j: (0, 0)), pl.BlockSpec((1, PA), lambda i, j: (0, 0))]
        args += [aux_w, aux_b.reshape(1, PA)]
        out_specs.append(pl.BlockSpec((tm, PA), lambda i, j: (i, 0)))
        out_shape.append(jax.ShapeDtypeStruct((N, PA), F32))
    if split:
        out_specs.append(pl.BlockSpec((tm, D), lambda i, j: (i, 0)))
        out_shape.append(jax.ShapeDtypeStruct((N, D), F32))
    return pl.pallas_call(
        functools.partial(_nm_matmul_kernel, shift_idx=shift_idx, scale_idx=scale_idx, has_aux=has_aux,
                          prompt_tiles=pt if split else None),
        grid=(N // tm, P // tn),
        in_specs=in_specs,
        out_specs=out_specs,
        out_shape=out_shape,
        scratch_shapes=[pltpu.VMEM((tm, D), BF16)],
        compiler_params=_cparams(("parallel", "arbitrary")),
        name=name,
    )(*args)


def _mm_residual_kernel(*refs, mlstm_prologue, prompt_tiles):
    if mlstm_prologue:
        hf_ref, hb_ref, og_ref, hg_ref, w_ref, x_ref, gate_ref, o_ref, l_scr = refs

        @pl.when(pl.program_id(1) == 0)
        def _():
            hs = hf_ref[...].astype(F32) + hb_ref[...].astype(F32)
            og = jax.nn.sigmoid(og_ref[...].astype(F32))
            for h in range(MLSTM_HEADS):
                sl = slice(h * MLSTM_DV, (h + 1) * MLSTM_DV)
                hh = hs[:, sl]
                hn = hh * lax.rsqrt(jnp.mean(hh * hh, axis=-1, keepdims=True) + EPS) * hg_ref[:, sl]
                l_scr[:, sl] = (og[:, sl] * hn).astype(BF16)

        o_ref[...] = x_ref[...] + gate_ref[...] * _dot(l_scr[...], w_ref[...])
    else:
        lp_ref, ls_ref, w_ref, x_ref, gate_ref, o_ref = refs
        context = pl.program_id(0) < prompt_tiles

        @pl.when(context)
        def _():
            o_ref[...] = x_ref[...] + gate_ref[...] * _dot(lp_ref[...], w_ref[...])

        @pl.when(jnp.logical_not(context))
        def _():
            o_ref[...] = x_ref[...] + gate_ref[...] * _dot(ls_ref[...], w_ref[...])


def _mm_residual(lhs_args, w, x, mod, *, gate_idx, n_prompt, dec_seq, tm, tn, mlstm_prologue, name):
    N, D = x.shape
    K = w.shape[0]
    assert N % tm == 0 and D % tn == 0 and n_prompt % tm == 0 and dec_seq % tm == 0
    row = functools.partial(_mod_row, tm=tm, n_prompt=n_prompt, dec_seq=dec_seq)
    pt = n_prompt // tm
    if mlstm_prologue:
        hf, hb, qkvo, head_g = lhs_args
        o_blk = (2 * MLSTM_QK + MLSTM_V) // MLSTM_V
        in_specs = [
            pl.BlockSpec((tm, K), lambda i, j: (i, 0)),
            pl.BlockSpec((tm, K), lambda i, j: (i, 0)),
            pl.BlockSpec((tm, MLSTM_V), lambda i, j: (i, o_blk)),
            pl.BlockSpec((1, K), lambda i, j: (0, 0)),
        ]
        args = [hf, hb, qkvo, head_g.reshape(1, K)]
        scratch = [pltpu.VMEM((tm, K), BF16)]
    else:
        lhs_p, lhs_s = lhs_args
        in_specs = [pl.BlockSpec((tm, K), lambda i, j: (jnp.minimum(i, pt - 1), 0)),
                    pl.BlockSpec((tm, K), lambda i, j: (jnp.maximum(i - pt, 0), 0))]
        args = [lhs_p, lhs_s]
        scratch = []
    in_specs += [
        pl.BlockSpec((K, tn), lambda i, j: (0, j)),
        pl.BlockSpec((tm, tn), lambda i, j: (i, j)),
        pl.BlockSpec((None, 1, tn), lambda i, j: (row(i), 0, gate_idx * (D // tn) + j)),
    ]
    args += [w, x, mod]
    return pl.pallas_call(
        functools.partial(_mm_residual_kernel, mlstm_prologue=mlstm_prologue, prompt_tiles=pt),
        grid=(N // tm, D // tn),
        in_specs=in_specs,
        out_specs=pl.BlockSpec((tm, tn), lambda i, j: (i, j)),
        out_shape=jax.ShapeDtypeStruct((N, D), F32),
        scratch_shapes=scratch,
        compiler_params=_cparams(("parallel", "arbitrary")),
        name=name,
    )(*args)


def _log_sigmoid(x):
    return jnp.minimum(x, 0.0) - jnp.log(1.0 + jnp.exp(-jnp.abs(x)))


def _mlstm_direction(d, q_ref, k_ref, v_ref, g_ref, h_ref, C_scr, n_scr, m_scr, L):
    H, DK, DV = MLSTM_HEADS, MLSTM_DK, MLSTM_DV
    g = g_ref[...]
    lf = _log_sigmoid(g)
    r = lax.broadcasted_iota(I32, (L, L), 0)
    c = lax.broadcasted_iota(I32, (L, L), 1)
    causal = (c <= r) if d == 0 else (c >= r)
    tri = jnp.where(causal, 1.0, 0.0).astype(BF16)
    lf1 = lf.astype(BF16)
    rem = lf - lf1.astype(F32)
    lf2 = rem.astype(BF16)
    lf3 = (rem - lf2.astype(F32)).astype(BF16)
    bsum = _dot(tri, lf1) + _dot(tri, lf2) + _dot(tri, lf3)
    g_t = g.T
    b_t = bsum.T
    end = L - 1 if d == 0 else 0
    m_old = [m_scr[d, h] for h in range(H)]
    n_old = [n_scr[d, h] for h in range(H)]
    m_upd, n_upd = [], []
    for h in range(H):
        ci, cf = d * 2 * H + h, d * 2 * H + H + h
        b_col, i_col = bsum[:, cf:cf + 1], g[:, ci:ci + 1]
        b_row, i_row = b_t[cf:cf + 1, :], g_t[ci:ci + 1, :]
        b_end = bsum[end:end + 1, cf:cf + 1]
        m, n = m_old[h], n_old[h]
        C = C_scr[d * H + h][...]
        qh = q_ref[:, h * DK:(h + 1) * DK]
        kh = k_ref[:, h * DK:(h + 1) * DK]
        vh = v_ref[:, h * DV:(h + 1) * DV]
        dmat = jnp.where(causal, b_col - b_row + i_row, NEG_INF)
        inter = b_col + m
        m_out = jnp.maximum(inter, jnp.max(dmat, axis=-1, keepdims=True))
        w = jnp.exp(dmat - m_out) * _dot_nt(qh, kh)
        dec = jnp.exp(inter - m_out)
        num = _dot(w.astype(BF16), vh) + dec * _dot(qh, C.astype(BF16))
        den = jnp.sum(w, axis=-1, keepdims=True) + dec * jnp.sum(qh.astype(F32) * n, axis=-1, keepdims=True)
        h_ref[:, h * DV:(h + 1) * DV] = (num / jnp.maximum(jnp.abs(den), jnp.exp(-m_out))).astype(h_ref.dtype)
        to_end = b_end - b_col + i_col
        m_new = jnp.maximum(b_end + m, jnp.max(to_end, axis=0, keepdims=True))
        wk = jnp.exp(to_end - m_new)
        dec_end = jnp.exp(b_end + m - m_new)
        kw = kh.astype(F32) * wk
        C_scr[d * H + h][...] = dec_end * C + _dot_tn(kw.astype(BF16), vh)
        n_upd.append(dec_end * n + jnp.sum(kw, axis=0, keepdims=True))
        m_upd.append(m_new)
    for h in range(H):
        n_scr[d, h] = n_upd[h]
        m_scr[d, h] = m_upd[h]


def _mlstm_kernel(fwd_ref, bwd_ref, bidx_ref, flag_ref, qf, kf, vf, gf, qb, kb, vb, gb, C0, n0, m0,
                  hf, hb, Co, no, mo, *scratch, L):
    H = MLSTM_HEADS
    C_scr, (n_scr, m_scr) = scratch[:2 * H], scratch[2 * H:]
    flags = flag_ref[pl.program_id(0)]
    first, last, context = (flags & 1) != 0, (flags & 2) != 0, (flags & 4) != 0

    @pl.when(first & context)
    def _():
        for C in C_scr:
            C[...] = jnp.zeros_like(C)
        n_scr[...] = jnp.zeros_like(n_scr)
        m_scr[...] = jnp.zeros_like(m_scr)

    @pl.when(first & jnp.logical_not(context))
    def _():
        for d in range(2):
            for h in range(H):
                C_scr[d * H + h][...] = C0[d, h]
                n_scr[d, h] = n0[d, h:h + 1, :]
                m_scr[d, h] = m0[d:d + 1, h:h + 1]

    _mlstm_direction(0, qf, kf, vf, gf, hf, C_scr, n_scr, m_scr, L)
    _mlstm_direction(1, qb, kb, vb, gb, hb, C_scr, n_scr, m_scr, L)

    @pl.when(last & context)
    def _():
        for d in range(2):
            for h in range(H):
                Co[d, h] = C_scr[d * H + h][...]
                no[d, h:h + 1, :] = n_scr[d, h]
                mo[d:d + 1, h:h + 1] = m_scr[d, h]


def _mlstm_scan(qkvo, gates, Bp, Sp, Bs, Ss, L, state):
    H, DK, DV = MLSTM_HEADS, MLSTM_DK, MLSTM_DV
    assert Sp % L == 0 and Ss % L == 0
    fwd, bwd, bidx, flags = [], [], [], []
    for context, B, S, base in ((1, Bp, Sp, 0), (0, Bs, Ss, Bp * Sp // L)):
        nc = S // L
        for b in range(B):
            for c in range(nc):
                fwd.append(base + b * nc + c)
                bwd.append(base + b * nc + nc - 1 - c)
                bidx.append(b)
                flags.append((c == 0) * 1 + (c == nc - 1) * 2 + context * 4)
    tables = [jnp.asarray(t, I32) for t in (fwd, bwd, bidx, flags)]
    lat_b = lambda s, fl, bi: jnp.where((fl[s] & 4) != 0, 0, bi[s])
    ctx_b = lambda s, fl, bi: jnp.where((fl[s] & 4) != 0, bi[s], Bp - 1)

    def specs(tbl):
        return [
            pl.BlockSpec((L, MLSTM_QK), lambda s, f, w, bi, fl: ((f, w)[tbl][s], 0)),
            pl.BlockSpec((L, MLSTM_QK), lambda s, f, w, bi, fl: ((f, w)[tbl][s], 1)),
            pl.BlockSpec((L, MLSTM_V), lambda s, f, w, bi, fl: ((f, w)[tbl][s], 1)),
            pl.BlockSpec((L, GATE_PAD), lambda s, f, w, bi, fl: ((f, w)[tbl][s], 0)),
        ]

    def state_specs(which):
        return [
            pl.BlockSpec((None, 2, H, DK, DV), lambda s, f, w, bi, fl: (which(s, fl, bi), 0, 0, 0, 0)),
            pl.BlockSpec((None, 2, H, DK), lambda s, f, w, bi, fl: (which(s, fl, bi), 0, 0, 0)),
            pl.BlockSpec((None, 2, H), lambda s, f, w, bi, fl: (which(s, fl, bi), 0, 0)),
        ]

    N = qkvo.shape[0]
    return pl.pallas_call(
        functools.partial(_mlstm_kernel, L=L),
        grid_spec=pltpu.PrefetchScalarGridSpec(
            num_scalar_prefetch=4,
            grid=(len(fwd),),
            in_specs=specs(0) + specs(1) + state_specs(lat_b),
            out_specs=[
                pl.BlockSpec((L, MLSTM_V), lambda s, f, w, bi, fl: (f[s], 0)),
                pl.BlockSpec((L, MLSTM_V), lambda s, f, w, bi, fl: (w[s], 0)),
            ] + state_specs(ctx_b),
            scratch_shapes=[pltpu.VMEM((DK, DV), F32)] * (2 * H) + [
                pltpu.VMEM((2, H, 1, DK), F32),
                pltpu.VMEM((2, H, 1, 1), F32),
            ],
        ),
        out_shape=[
            jax.ShapeDtypeStruct((N, MLSTM_V), BF16),
            jax.ShapeDtypeStruct((N, MLSTM_V), BF16),
            jax.ShapeDtypeStruct((Bp, 2, H, DK, DV), F32),
            jax.ShapeDtypeStruct((Bp, 2, H, DK), F32),
            jax.ShapeDtypeStruct((Bp, 2, H), F32),
        ],
        compiler_params=_cparams(("arbitrary",)),
        name="mlstm_scan",
    )(*tables, qkvo, qkvo, qkvo, gates, qkvo, qkvo, qkvo, gates, *state)


def _qk_prep_kernel(qkv_ref, qg_ref, kg_ref, cos_ref, sin_ref, *outs, emit_f32):
    if emit_f32:
        q_ref, k_ref, v_ref, kf_ref, vf_ref = outs
    else:
        q_ref, k_ref, v_ref = outs
    cos = cos_ref[...]
    sin = sin_ref[...]
    lane = lax.broadcasted_iota(I32, cos.shape, 1)
    first = (lane % (HEAD_DIM // 2)) < (HEAD_DIM // 4)

    def norm_rope(x, gain):
        xn = x * lax.rsqrt(jnp.mean(x * x, axis=-1, keepdims=True) + EPS) * gain
        partner = jnp.where(first, pltpu.roll(xn, HEAD_DIM - HEAD_DIM // 4, 1), pltpu.roll(xn, HEAD_DIM // 4, 1))
        return xn * cos + partner * sin

    for h in range(ATTN_Q_HEADS):
        sl = slice(h * HEAD_DIM, (h + 1) * HEAD_DIM)
        q_ref[:, sl] = (norm_rope(qkv_ref[:, sl], qg_ref[...]) * HEAD_DIM ** -0.5).astype(BF16)
    for h in range(ATTN_KV_HEADS):
        sl = slice(h * HEAD_DIM, (h + 1) * HEAD_DIM)
        kx = norm_rope(qkv_ref[:, ATTN_Q + h * HEAD_DIM:ATTN_Q + (h + 1) * HEAD_DIM], kg_ref[...])
        k_ref[:, sl] = kx.astype(BF16)
        if emit_f32:
            kf_ref[:, sl] = kx
    vx = qkv_ref[:, ATTN_Q + ATTN_KV:]
    v_ref[...] = vx.astype(BF16)
    if emit_f32:
        vf_ref[...] = vx


def _qk_prep(qkv, q_g, k_g, cos, sin, row0, rows, tm, emit_f32, name):
    assert rows % tm == 0 and row0 % tm == 0 and cos.shape[0] % tm == 0
    base = row0 // tm
    nt = cos.shape[0] // tm
    out_specs = [
        pl.BlockSpec((tm, ATTN_Q), lambda i: (i, 0)),
        pl.BlockSpec((tm, ATTN_KV), lambda i: (i, 0)),
        pl.BlockSpec((tm, ATTN_KV), lambda i: (i, 0)),
    ]
    out_shape = [
        jax.ShapeDtypeStruct((rows, ATTN_Q), BF16),
        jax.ShapeDtypeStruct((rows, ATTN_KV), BF16),
        jax.ShapeDtypeStruct((rows, ATTN_KV), BF16),
    ]
    if emit_f32:
        out_specs += [pl.BlockSpec((tm, ATTN_KV), lambda i: (i, 0))] * 2
        out_shape += [jax.ShapeDtypeStruct((rows, ATTN_KV), F32)] * 2
    return pl.pallas_call(
        functools.partial(_qk_prep_kernel, emit_f32=emit_f32),
        grid=(rows // tm,),
        in_specs=[
            pl.BlockSpec((tm, ATTN_PROJ), lambda i: (base + i, 0)),
            pl.BlockSpec((1, HEAD_DIM), lambda i: (0, 0)),
            pl.BlockSpec((1, HEAD_DIM), lambda i: (0, 0)),
            pl.BlockSpec((tm, HEAD_DIM), lambda i: (i % nt, 0)),
            pl.BlockSpec((tm, HEAD_DIM), lambda i: (i % nt, 0)),
        ],
        out_specs=out_specs,
        out_shape=out_shape,
        compiler_params=_cparams(("parallel",)),
        name=name,
    )(qkv, q_g.reshape(1, HEAD_DIM), k_g.reshape(1, HEAD_DIM), cos, sin)


def _sink_column(sink_ref, kv, rows_per_head):
    parts = [jnp.full((rows_per_head, 1), sink_ref[kv * ATTN_GROUPS + g], F32) for g in range(ATTN_GROUPS)]
    return jnp.concatenate(parts, axis=0)


def _attn_ctx_kernel(sink_ref, q_ref, k_ref, v_ref, o_ref):
    S = q_ref.shape[0]
    for kv in range(ATTN_KV_HEADS):
        q = jnp.concatenate(
            [q_ref[:, (kv * ATTN_GROUPS + g) * HEAD_DIM:(kv * ATTN_GROUPS + g + 1) * HEAD_DIM]
             for g in range(ATTN_GROUPS)], axis=0)
        ksl = slice(kv * HEAD_DIM, (kv + 1) * HEAD_DIM)
        s = _dot_nt(q, k_ref[:, ksl])
        sk = _sink_column(sink_ref, kv, S)
        m = jnp.maximum(jnp.max(s, axis=-1, keepdims=True), sk)
        p = jnp.exp(s - m)
        den = jnp.sum(p, axis=-1, keepdims=True) + jnp.exp(sk - m)
        o = _dot(p.astype(BF16), v_ref[:, ksl]) * (1.0 / den)
        for g in range(ATTN_GROUPS):
            hq = kv * ATTN_GROUPS + g
            o_ref[:, hq * HEAD_DIM:(hq + 1) * HEAD_DIM] = o[g * S:(g + 1) * S].astype(BF16)


def _attn_context(sink, q, k, v, B, S):
    return pl.pallas_call(
        _attn_ctx_kernel,
        grid_spec=pltpu.PrefetchScalarGridSpec(
            num_scalar_prefetch=0,
            grid=(B,),
            in_specs=[
                pl.BlockSpec(memory_space=pltpu.SMEM),
                pl.BlockSpec((S, ATTN_Q), lambda b: (b, 0)),
                pl.BlockSpec((S, ATTN_KV), lambda b: (b, 0)),
                pl.BlockSpec((S, ATTN_KV), lambda b: (b, 0)),
            ],
            out_specs=pl.BlockSpec((S, ATTN_Q), lambda b: (b, 0)),
        ),
        out_shape=jax.ShapeDtypeStruct((B * S, ATTN_Q), BF16),
        compiler_params=_cparams(("parallel",)),
        name="attn_context",
    )(sink, q, k, v)


def _attn_lat_kernel(sink_ref, q_ref, kp_ref, kc_ref, kn_ref, vp_ref, vc_ref, vn_ref, kx_ref, vx_ref, o_ref):
    i = pl.program_id(1)
    nb = pl.num_programs(1)
    R = ATTN_GROUPS * QBLK
    r = lax.broadcasted_iota(I32, (R, QBLK), 0) % QBLK
    c = lax.broadcasted_iota(I32, (R, QBLK), 1)
    ok_prev = c >= r
    ok_next = c <= r
    edge_prev = jnp.where(i > 0, 0.0, NEG_INF)
    edge_next = jnp.where(i < nb - 1, 0.0, NEG_INF)
    for kv in range(ATTN_KV_HEADS):
        q = jnp.concatenate(
            [q_ref[:, (kv * ATTN_GROUPS + g) * HEAD_DIM:(kv * ATTN_GROUPS + g + 1) * HEAD_DIM]
             for g in range(ATTN_GROUPS)], axis=0)
        ksl = slice(kv * HEAD_DIM, (kv + 1) * HEAD_DIM)
        s_p = jnp.where(ok_prev, _dot_nt(q, kp_ref[:, ksl]) + edge_prev, NEG_INF)
        s_c = _dot_nt(q, kc_ref[:, ksl])
        s_n = jnp.where(ok_next, _dot_nt(q, kn_ref[:, ksl]) + edge_next, NEG_INF)
        s_x = _dot_nt(q, kx_ref[:, ksl])
        sk = _sink_column(sink_ref, kv, QBLK)
        m = jnp.maximum(
            jnp.maximum(jnp.max(s_p, axis=-1, keepdims=True), jnp.max(s_c, axis=-1, keepdims=True)),
            jnp.maximum(jnp.max(s_n, axis=-1, keepdims=True), jnp.max(s_x, axis=-1, keepdims=True)))
        m = jnp.maximum(m, sk)
        p_p, p_c, p_n, p_x = jnp.exp(s_p - m), jnp.exp(s_c - m), jnp.exp(s_n - m), jnp.exp(s_x - m)
        den = (jnp.sum(p_p, axis=-1, keepdims=True) + jnp.sum(p_c, axis=-1, keepdims=True)
               + jnp.sum(p_n, axis=-1, keepdims=True) + jnp.sum(p_x, axis=-1, keepdims=True) + jnp.exp(sk - m))
        o = (_dot(p_p.astype(BF16), vp_ref[:, ksl]) + _dot(p_c.astype(BF16), vc_ref[:, ksl])
             + _dot(p_n.astype(BF16), vn_ref[:, ksl]) + _dot(p_x.astype(BF16), vx_ref[:, ksl])) * (1.0 / den)
        for g in range(ATTN_GROUPS):
            hq = kv * ATTN_GROUPS + g
            o_ref[:, hq * HEAD_DIM:(hq + 1) * HEAD_DIM] = o[g * QBLK:(g + 1) * QBLK].astype(BF16)


def _attn_latent(sink, q, k, v, ctx_k, ctx_v, B, S):
    nb = S // QBLK
    P = ctx_k.shape[1]
    prev = lambda b, i: (b * nb + jnp.maximum(i - 1, 0), 0)
    cur = lambda b, i: (b * nb + i, 0)
    nxt = lambda b, i: (b * nb + jnp.minimum(i + 1, nb - 1), 0)
    kv_spec = lambda f: pl.BlockSpec((QBLK, ATTN_KV), f)
    return pl.pallas_call(
        _attn_lat_kernel,
        grid_spec=pltpu.PrefetchScalarGridSpec(
            num_scalar_prefetch=0,
            grid=(B, nb),
            in_specs=[
                pl.BlockSpec(memory_space=pltpu.SMEM),
                pl.BlockSpec((QBLK, ATTN_Q), cur),
                kv_spec(prev), kv_spec(cur), kv_spec(nxt),
                kv_spec(prev), kv_spec(cur), kv_spec(nxt),
                pl.BlockSpec((None, P, ATTN_KV), lambda b, i: (b, 0, 0)),
                pl.BlockSpec((None, P, ATTN_KV), lambda b, i: (b, 0, 0)),
            ],
            out_specs=pl.BlockSpec((QBLK, ATTN_Q), cur),
        ),
        out_shape=jax.ShapeDtypeStruct((B * S, ATTN_Q), BF16),
        compiler_params=_cparams(("parallel", "parallel")),
        name="attn_latent",
    )(sink, q, k, k, k, v, v, v, ctx_k, ctx_v)


SLAB = 8
U32 = jnp.uint32


def _pack_rows(x):
    half = x.shape[1] // 2
    bits = lax.bitcast_convert_type(x.astype(BF16).astype(F32), U32)
    return (bits[:, :half] >> 16) | (bits[:, half:] & jnp.uint32(0xFFFF0000))


def _store_slabs(ref, words):
    R = words.shape[0]
    for j in range(SLAB):
        ref[pl.ds(j, R, stride=SLAB), :] = words[:, j * 128:(j + 1) * 128]


def _load_slab_halves(ref, R, j):
    w = ref[pl.ds(j, R, stride=SLAB), :]
    return lax.bitcast_convert_type(w << 16, F32), lax.bitcast_convert_type(w & jnp.uint32(0xFFFF0000), F32)


def _load_rows_bf16(ref, R):
    halves = [_load_slab_halves(ref, R, j) for j in range(SLAB)]
    return jnp.concatenate([lo for lo, _ in halves] + [hi for _, hi in halves], axis=1).astype(BF16)


def _slab_copy(src, dst, sem, src_row8, dst_row):
    return pltpu.make_async_copy(src.at[pl.ds(pl.multiple_of(src_row8, SLAB), SLAB), :],
                                 dst.at[pl.ds(dst_row * SLAB, SLAB), :], sem)


def _router_kernel(x_ref, gain_ref, mod_ref, wr_ref, rb_ref, h_ref, e_ref, g_ref, r_ref, cnt_ref, cnt_scr, *, tm):
    E, GS = N_EXPERTS, GROUP_SIZE
    i = pl.program_id(0)

    @pl.when(i == 0)
    def _():
        cnt_scr[...] = jnp.zeros_like(cnt_scr)

    h = _norm_modulate(x_ref[...], gain_ref[...], mod_ref, 3, 4)
    _store_slabs(h_ref, _pack_rows(h))
    h_hi, h_lo = _split_hi_lo(h)
    w_hi, w_lo = _split_hi_lo(wr_ref[...])
    logits = _dot_nt(w_hi, h_hi) + _dot_nt(w_lo, h_hi) + _dot_nt(w_hi, h_lo)
    scores = jax.nn.sigmoid(logits)
    biased = scores + rb_ref[...]
    sub = lax.broadcasted_iota(I32, (GS, tm), 0)
    gscore = []
    for gi in range(N_EXPERT_GROUPS):
        xg = biased[gi * GS:(gi + 1) * GS, :]
        m1 = jnp.max(xg, axis=0, keepdims=True)
        first = jnp.min(jnp.where(xg == m1, sub, GS), axis=0, keepdims=True)
        m2 = jnp.max(jnp.where(sub == first, NEG_INF, xg), axis=0, keepdims=True)
        gscore.append(m1 + m2)
    cur = jnp.concatenate(gscore, axis=0)
    gid = lax.broadcasted_iota(I32, (N_EXPERT_GROUPS, tm), 0)
    gsel = jnp.zeros((N_EXPERT_GROUPS, tm), F32)
    for _ in range(TOPK_GROUPS):
        mx = jnp.max(cur, axis=0, keepdims=True)
        first = jnp.min(jnp.where(cur == mx, gid, N_EXPERT_GROUPS), axis=0, keepdims=True)
        hit = gid == first
        gsel = jnp.where(hit, 1.0, gsel)
        cur = jnp.where(hit, NEG_INF, cur)
    ok = jnp.concatenate(
        [jnp.broadcast_to(gsel[gi:gi + 1, :], (GS, tm)) for gi in range(N_EXPERT_GROUPS)], axis=0)
    masked = jnp.where(ok > 0.5, biased, NEG_INF)
    eid = lax.broadcasted_iota(I32, (E, tm), 0)
    sel = jnp.zeros((E, tm), F32)
    picks, pick_scores = [], []
    for _ in range(TOP_K):
        mx = jnp.max(masked, axis=0, keepdims=True)
        first = jnp.min(jnp.where(masked == mx, eid, E), axis=0, keepdims=True)
        hit = eid == first
        picks.append(first)
        pick_scores.append(jnp.sum(jnp.where(hit, scores, 0.0), axis=0, keepdims=True))
        sel = jnp.where(hit, 1.0, sel)
        masked = jnp.where(hit, NEG_INF, masked)
    total = pick_scores[0]
    for s in pick_scores[1:]:
        total = total + s
    g_ref[...] = jnp.concatenate(pick_scores, axis=0) / total * ROUTED_SCALE
    e_ref[...] = jnp.concatenate(picks, axis=0)
    rr = lax.broadcasted_iota(I32, (tm, tm), 0)
    cc = lax.broadcasted_iota(I32, (tm, tm), 1)
    before = jnp.where(rr < cc, 1.0, 0.0).astype(BF16)
    rank = _dot(sel.astype(BF16), before) + cnt_scr[...]
    r_ref[...] = jnp.concatenate(
        [jnp.sum(jnp.where(eid == p, rank, 0.0), axis=0, keepdims=True) for p in picks], axis=0).astype(I32)
    cnt_scr[...] = cnt_scr[...] + jnp.sum(sel, axis=-1, keepdims=True)
    cnt_ref[...] = cnt_scr[...].astype(I32)


def _router(x, gain, mod, router_w_t, router_b, *, n_prompt, dec_seq, tm):
    N, D = x.shape
    E = N_EXPERTS
    assert D == 2 * SLAB * 128
    assert N % tm == 0 and n_prompt % tm == 0 and dec_seq % tm == 0
    row = functools.partial(_mod_row, tm=tm, n_prompt=n_prompt, dec_seq=dec_seq)
    return pl.pallas_call(
        functools.partial(_router_kernel, tm=tm),
        grid=(N // tm,),
        in_specs=[
            pl.BlockSpec((tm, D), lambda i: (i, 0)),
            pl.BlockSpec((1, D), lambda i: (0, 0)),
            pl.BlockSpec((None, 1, 6 * D), lambda i: (row(i), 0, 0)),
            pl.BlockSpec((E, D), lambda i: (0, 0)),
            pl.BlockSpec((E, 1), lambda i: (0, 0)),
        ],
        out_specs=[
            pl.BlockSpec((tm * SLAB, 128), lambda i: (i, 0)),
            pl.BlockSpec((TOP_K, tm), lambda i: (0, i)),
            pl.BlockSpec((TOP_K, tm), lambda i: (0, i)),
            pl.BlockSpec((TOP_K, tm), lambda i: (0, i)),
            pl.BlockSpec((E, 1), lambda i: (0, 0)),
        ],
        out_shape=[
            jax.ShapeDtypeStruct((N * SLAB, 128), U32),
            jax.ShapeDtypeStruct((TOP_K, N), I32),
            jax.ShapeDtypeStruct((TOP_K, N), F32),
            jax.ShapeDtypeStruct((TOP_K, N), I32),
            jax.ShapeDtypeStruct((E, 1), I32),
        ],
        scratch_shapes=[pltpu.VMEM((E, 1), F32)],
        compiler_params=_cparams(("arbitrary",)),
        name="moe_router",
    )(x, gain.reshape(1, D), mod, router_w_t, router_b.reshape(E, 1))


def _experts_kernel(be_ref, nused_ref, tok_cur, tok_nxt, h_hbm, w1_ref, w3_ref, w2_ref, y_ref,
                    xbuf, sems, w1c, w3c, w2c, prev_e):
    b = pl.program_id(0)
    e = be_ref[b]
    n_used = nused_ref[0]
    BLK = xbuf.shape[1] // SLAB
    slot = b % 2

    FF = w1c.shape[1]
    D = w2c.shape[1]
    CW = 256
    n_chunks = 2 * (FF // CW) + D // CW
    per_chunk = BLK // n_chunks + 1

    def gather(tok_ref, s, lo=0, hi=BLK):
        for r in range(lo, min(hi, BLK)):
            _slab_copy(h_hbm, xbuf.at[s], sems.at[s], tok_ref[0, r], r).start(priority=r % 2)

    def drain(s):
        for r in range(BLK):
            _slab_copy(h_hbm, xbuf.at[s], sems.at[s], 0, r).wait()

    @pl.when(b == 0)
    def _():
        prev_e[0] = -1
        gather(tok_cur, 0)

    @pl.when(b < n_used)
    def _():
        @pl.when(e != prev_e[0])
        def _():
            w1c[...] = w1_ref[...].astype(BF16)
            w3c[...] = w3_ref[...].astype(BF16)
            w2c[...] = w2_ref[...].astype(BF16)
            prev_e[0] = e

        drain(slot)
        chunk = [0]

        def gather_some():
            gather(tok_nxt, 1 - slot, chunk[0] * per_chunk, (chunk[0] + 1) * per_chunk)
            chunk[0] += 1

        x = _load_rows_bf16(xbuf.at[slot], BLK)
        mids = []
        for c in range(FF // CW):
            sl = slice(c * CW, (c + 1) * CW)
            gather_some()
            a = _dot(x, w1c[:, sl])
            gather_some()
            mids.append((_silu(a) * _dot(x, w3c[:, sl])).astype(BF16))
        mid = jnp.concatenate(mids, axis=1)
        half = D // 2
        for c in range(half // CW):
            gather_some()
            y_lo = _dot(mid, w2c[:, c * CW:(c + 1) * CW])
            gather_some()
            y_hi = _dot(mid, w2c[:, half + c * CW:half + (c + 1) * CW])
            words = _pack_rows(jnp.concatenate([y_lo, y_hi], axis=1))
            for jj in range(CW // 128):
                j = c * (CW // 128) + jj
                y_ref[pl.ds(j, BLK, stride=SLAB), :] = words[:, jj * 128:(jj + 1) * 128]

    @pl.when(b >= n_used)
    def _():
        @pl.when(b == n_used)
        def _():
            drain(slot)

        y_ref[...] = jnp.zeros_like(y_ref)


def _experts(block_expert, n_used, slot_tok8, h, w1, w3, w2, layer):
    n_blocks = block_expert.shape[0]
    BLK = MOE_BLOCK
    _, _, D, FF = w1.shape
    slot_tok3 = slot_tok8.reshape(n_blocks, 1, BLK)
    return pl.pallas_call(
        _experts_kernel,
        grid_spec=pltpu.PrefetchScalarGridSpec(
            num_scalar_prefetch=2,
            grid=(n_blocks,),
            in_specs=[
                pl.BlockSpec((None, 1, BLK), lambda b, be, nu: (b, 0, 0), memory_space=pltpu.SMEM),
                pl.BlockSpec((None, 1, BLK), lambda b, be, nu: (jnp.minimum(b + 1, n_blocks - 1), 0, 0),
                             memory_space=pltpu.SMEM),
                pl.BlockSpec(memory_space=pl.ANY),
                pl.BlockSpec((None, None, D, FF), lambda b, be, nu: (layer, be[b], 0, 0)),
                pl.BlockSpec((None, None, D, FF), lambda b, be, nu: (layer, be[b], 0, 0)),
                pl.BlockSpec((None, None, FF, D), lambda b, be, nu: (layer, be[b], 0, 0)),
            ],
            out_specs=pl.BlockSpec((BLK * SLAB, 128), lambda b, be, nu: (b, 0)),
            scratch_shapes=[
                pltpu.VMEM((2, BLK * SLAB, 128), U32),
                pltpu.SemaphoreType.DMA((2,)),
                pltpu.VMEM((D, FF), BF16),
                pltpu.VMEM((D, FF), BF16),
                pltpu.VMEM((FF, D), BF16),
                pltpu.SMEM((1,), I32),
            ],
        ),
        out_shape=jax.ShapeDtypeStruct((n_blocks * BLK * SLAB, 128), U32),
        compiler_params=_cparams(("arbitrary",)),
        name="moe_experts",
    )(block_expert, n_used, slot_tok3, slot_tok3, h, w1, w3, w2)


def _combine_kernel(dest_cur, dest_nxt, gates_ref, y_hbm, h_ref, x_ref, gate_ref, ws1_ref, ws3_ref, ws2_ref,
                    *rest, prompt_tiles):
    if prompt_tiles is None:
        o_ref, ybuf0, ybuf1, sems = rest
    else:
        op_ref, os_ref, ybuf0, ybuf1, sems = rest
    tm = x_ref.shape[0]
    i = pl.program_id(0)
    last = pl.num_programs(0) - 1

    def gather(dest_ref, buf, sem, lo=0, hi=tm):
        for k in range(TOP_K):
            for r in range(lo, hi):
                _slab_copy(y_hbm, buf.at[k], sem, dest_ref[k, r], r).start(priority=r % 2)

    def drain(buf, sem):
        for k in range(TOP_K):
            for r in range(tm):
                _slab_copy(y_hbm, buf.at[k], sem, 0, r).wait()

    @pl.when(i == 0)
    def _():
        gather(dest_cur, ybuf0, sems.at[0])

    def step(cur, cur_sem, nxt, nxt_sem):
        drain(cur, cur_sem)
        per_group = tm // (SLAB + 2)
        gather(dest_nxt, nxt, nxt_sem, 0, per_group)
        hb = _load_rows_bf16(h_ref, tm)
        mid = (_silu(_dot(hb, ws1_ref[...])) * _dot(hb, ws3_ref[...])).astype(BF16)
        gather(dest_nxt, nxt, nxt_sem, per_group, 2 * per_group)
        shared = _dot(mid, ws2_ref[...])
        g = [gates_ref[:, k:k + 1] for k in range(TOP_K)]
        lo_parts, hi_parts = [], []
        for j in range(SLAB):
            gather(dest_nxt, nxt, nxt_sem, (j + 2) * per_group, tm if j == SLAB - 1 else (j + 3) * per_group)
            lo_acc = hi_acc = None
            for k in range(TOP_K):
                lo, hi = _load_slab_halves(cur.at[k], tm, j)
                lo_acc = g[k] * lo if lo_acc is None else lo_acc + g[k] * lo
                hi_acc = g[k] * hi if hi_acc is None else hi_acc + g[k] * hi
            lo_parts.append(lo_acc)
            hi_parts.append(hi_acc)
        routed = jnp.concatenate(lo_parts + hi_parts, axis=1)
        out = x_ref[...] + gate_ref[...] * (routed + shared)
        if prompt_tiles is None:
            o_ref[...] = out
        else:
            @pl.when(i < prompt_tiles)
            def _():
                op_ref[...] = out

            @pl.when(i >= prompt_tiles)
            def _():
                os_ref[...] = out

        @pl.when(i == last)
        def _():
            drain(nxt, nxt_sem)

    @pl.when(i % 2 == 0)
    def _():
        step(ybuf0, sems.at[0], ybuf1, sems.at[1])

    @pl.when(i % 2 == 1)
    def _():
        step(ybuf1, sems.at[1], ybuf0, sems.at[0])


def _combine(dest, gates, y, h, x, mod, ws1, ws3, ws2, *, n_prompt, dec_seq, split_outputs):
    N, D = x.shape
    tm = COMBINE_TM
    FF = ws1.shape[1]
    assert N % tm == 0 and n_prompt % tm == 0 and dec_seq % tm == 0
    nt = N // tm
    row = functools.partial(_mod_row, tm=tm, n_prompt=n_prompt, dec_seq=dec_seq)
    dest3 = dest.reshape(TOP_K, nt, tm).transpose(1, 0, 2)
    if split_outputs:
        pt = n_prompt // tm
        out_specs = [pl.BlockSpec((tm, D), lambda i: (jnp.minimum(i, pt - 1), 0)),
                     pl.BlockSpec((tm, D), lambda i: (jnp.maximum(i - pt, 0), 0))]
        out_shape = [jax.ShapeDtypeStruct((n_prompt, D), F32), jax.ShapeDtypeStruct((N - n_prompt, D), F32)]
    else:
        pt = None
        out_specs = pl.BlockSpec((tm, D), lambda i: (i, 0))
        out_shape = jax.ShapeDtypeStruct((N, D), F32)
    return pl.pallas_call(
        functools.partial(_combine_kernel, prompt_tiles=pt),
        grid=(nt,),
        in_specs=[
            pl.BlockSpec((None, TOP_K, tm), lambda i: (i, 0, 0), memory_space=pltpu.SMEM),
            pl.BlockSpec((None, TOP_K, tm), lambda i: (jnp.minimum(i + 1, nt - 1), 0, 0), memory_space=pltpu.SMEM),
            pl.BlockSpec((tm, TOP_K), lambda i: (i, 0)),
            pl.BlockSpec(memory_space=pl.ANY),
            pl.BlockSpec((tm * SLAB, 128), lambda i: (i, 0)),
            pl.BlockSpec((tm, D), lambda i: (i, 0)),
            pl.BlockSpec((None, 1, D), lambda i: (row(i), 0, 5)),
            pl.BlockSpec((D, FF), lambda i: (0, 0)),
            pl.BlockSpec((D, FF), lambda i: (0, 0)),
            pl.BlockSpec((FF, D), lambda i: (0, 0)),
        ],
        out_specs=out_specs,
        out_shape=out_shape,
        scratch_shapes=[pltpu.VMEM((TOP_K, tm * SLAB, 128), U32), pltpu.VMEM((TOP_K, tm * SLAB, 128), U32),
                        pltpu.SemaphoreType.DMA((2,))],
        compiler_params=_cparams(("arbitrary",)),
        name="moe_combine",
    )(dest3, dest3, gates, y, h, x, mod, ws1, ws3, ws2)


def _moe_layer(x, gain, mod, router_w, router_b, w1, w3, w2, layer, ws1, ws3, ws2, *, n_prompt, dec_seq,
               split_outputs=False):
    N = x.shape[0]
    E, BLK = N_EXPERTS, MOE_BLOCK
    h, top_e, gates, rank, counts = _router(x, gain, mod, router_w.T, router_b,
                                            n_prompt=n_prompt, dec_seq=dec_seq,
                                            tm=_row_tile(512, n_prompt, dec_seq))
    counts = counts[:, 0]
    padded = (counts + BLK - 1) // BLK * BLK
    pad_end = jnp.cumsum(padded)
    pad_start = pad_end - padded
    n_blocks = N * TOP_K // BLK + E
    block_expert = jnp.minimum(
        jnp.sum(pad_end[None, :] <= (jnp.arange(n_blocks, dtype=I32) * BLK)[:, None], axis=1), E - 1).astype(I32)
    n_used = (pad_end[-1:] // BLK).astype(I32)
    onehot = top_e[None] == jnp.arange(E, dtype=I32)[:, None, None]
    dest = jnp.sum(jnp.where(onehot, pad_start[:, None, None], 0), axis=0) + rank
    NP = N + BLK
    keys = (top_e * NP + jnp.arange(N, dtype=I32)[None, :]).reshape(-1)
    fill_i = jnp.arange(BLK, dtype=I32)[None, :]
    fill_e = jnp.arange(E, dtype=I32)[:, None]
    fillers = jnp.where(fill_i < (padded - counts)[:, None], fill_e * NP + N + fill_i, E * NP + fill_e * BLK + fill_i)
    slot_t = lax.sort(jnp.concatenate([keys, fillers.reshape(-1)])) % NP
    spread = jnp.arange(n_blocks * BLK, dtype=I32) % N
    slot_tok8 = jnp.where(slot_t < N, slot_t, spread) * SLAB
    y = _experts(block_expert, n_used, slot_tok8, h, w1, w3, w2, layer)
    return _combine(dest * SLAB, gates.T, y, h, x, mod, ws1, ws3, ws2, n_prompt=n_prompt, dec_seq=dec_seq,
                    split_outputs=split_outputs)


def _rope_tables(S):
    quarter = HEAD_DIM // 4
    pos = jnp.arange(S)
    row_id = (pos // GRID_W).astype(F32)
    col_id = (pos % GRID_W).astype(F32)
    inv = ROPE_THETA ** (-jnp.arange(quarter, dtype=F32) / quarter)
    ar, ac = row_id[:, None] * inv, col_id[:, None] * inv
    cos = jnp.concatenate([jnp.cos(ar), jnp.cos(ar), jnp.cos(ac), jnp.cos(ac)], axis=-1)
    sin = jnp.concatenate([-jnp.sin(ar), jnp.sin(ar), -jnp.sin(ac), jnp.sin(ac)], axis=-1)
    return cos, sin


def kernel(x_prompt, x_sample, state_mlstm_C, state_mlstm_n, state_mlstm_m, cache_attn_k, cache_attn_v,
           c, c_ctx, ada_w, ada_b, norm_mix, norm_ffn,
           mlstm_w_in, mlstm_gate_b, mlstm_head_g, mlstm_w_out,
           attn_w_qkv, attn_q_g, attn_k_g, attn_sink, attn_w_o,
           moe_router_w, moe_router_b, moe_w1, moe_w3, moe_w2, shared_w1, shared_w3, shared_w2):
    D = D_MODEL
    Bp, Sp, _ = x_prompt.shape
    Bs, Ss, _ = x_sample.shape
    n_prompt = Bp * Sp
    N = n_prompt + Bs * Ss
    dims = dict(n_prompt=n_prompt, dec_seq=Ss)
    tm512 = _row_tile(512, n_prompt, Ss)

    x_parts = (x_prompt.reshape(n_prompt, D), x_sample.reshape(Bs * Ss, D))
    rows = 16
    cvec = jnp.zeros((rows, D), F32).at[0].set(c_ctx).at[1:1 + Bs].set(c)
    mod_all = _ada_table(cvec, ada_w, ada_b)[:, :1 + Bs].reshape(DEPTH, 1 + Bs, 1, 6 * D)

    H = MLSTM_HEADS
    mod = mod_all[0]
    w_in = mlstm_w_in[0]
    w_main = w_in[:, :MLSTM_MAIN].astype(BF16)
    w_gate = jnp.pad(w_in[:, MLSTM_MAIN:], ((0, 0), (0, GATE_PAD - 4 * H))).astype(BF16)
    gate_b = jnp.pad(mlstm_gate_b[0], (0, GATE_PAD - 4 * H))
    col_scale = jnp.concatenate([jnp.full((MLSTM_QK,), MLSTM_DK ** -0.5, F32),
                                 jnp.ones((MLSTM_MAIN - MLSTM_QK,), F32)])
    qkvo, gates, x = _nm_matmul(x_parts, norm_mix[0], mod, w_main, col_scale, jnp.zeros((MLSTM_MAIN,), F32),
                                shift_idx=0, scale_idx=1, out_dtype=BF16, tm=tm512, tn=1024,
                                aux_w=w_gate, aux_b=gate_b, name="mlstm_proj", **dims)
    L = 256
    state = (state_mlstm_C[:, 0], state_mlstm_n[:, 0], state_mlstm_m[:, 0])
    hf, hb, C_p, n_p, m_p = _mlstm_scan(qkvo, gates, Bp, Sp, Bs, Ss, L, state)
    x = _mm_residual((hf, hb, qkvo, mlstm_head_g[0]), mlstm_w_out[0].astype(BF16), x, mod, gate_idx=2,
                     tm=_row_tile(256, n_prompt, Ss), tn=D, mlstm_prologue=True, name="mlstm_out", **dims)
    x = _moe_layer(x, norm_ffn[0], mod, moe_router_w[0], moe_router_b[0], moe_w1, moe_w3, moe_w2, 0,
                   shared_w1[0].astype(BF16), shared_w3[0].astype(BF16), shared_w2[0].astype(BF16), **dims)

    mod = mod_all[1]
    (qkv,) = _nm_matmul(x, norm_mix[1], mod, attn_w_qkv[0].astype(BF16), jnp.ones((ATTN_PROJ,), F32),
                        jnp.zeros((ATTN_PROJ,), F32), shift_idx=0, scale_idx=1, out_dtype=F32,
                        tm=_row_tile(1024, n_prompt, Ss), tn=1024,
                        name="attn_qkv", **dims)
    ident_cos = jnp.ones((Sp, HEAD_DIM), F32)
    ident_sin = jnp.zeros((Sp, HEAD_DIM), F32)
    q_p, k_p, v_p, kf_p, vf_p = _qk_prep(qkv, attn_q_g[0], attn_k_g[0], ident_cos, ident_sin, 0, n_prompt, Sp,
                                         True, "qk_prep_prompt")
    cos, sin = _rope_tables(Ss)
    q_s, k_s, v_s = _qk_prep(qkv, attn_q_g[0], attn_k_g[0], cos, sin, n_prompt, Bs * Ss, 256, False,
                             "qk_prep_latent")
    o_p = _attn_context(attn_sink[0], q_p, k_p, v_p, Bp, Sp)
    P = cache_attn_k.shape[2]
    ctx_k = cache_attn_k[:, 0].reshape(Bs, P, ATTN_KV).astype(BF16)
    ctx_v = cache_attn_v[:, 0].reshape(Bs, P, ATTN_KV).astype(BF16)
    o_s = _attn_latent(attn_sink[0], q_s, k_s, v_s, ctx_k, ctx_v, Bs, Ss)
    x = _mm_residual((o_p, o_s), attn_w_o[0].astype(BF16), x, mod, gate_idx=2,
                     tm=tm512, tn=D,
                     mlstm_prologue=False, name="attn_out", **dims)
    y_prompt, y_sample = _moe_layer(
        x, norm_ffn[1], mod, moe_router_w[1], moe_router_b[1], moe_w1, moe_w3, moe_w2, 1,
        shared_w1[1].astype(BF16), shared_w3[1].astype(BF16), shared_w2[1].astype(BF16), split_outputs=True, **dims)
    return (y_prompt.reshape(Bp, Sp, D), y_sample.reshape(Bs, Ss, D), C_p[:, None], n_p[:, None], m_p[:, None],
            kf_p.reshape(Bp, 1, Sp, ATTN_KV_HEADS, HEAD_DIM), vf_p.reshape(Bp, 1, Sp, ATTN_KV_HEADS, HEAD_DIM))
```
